```python
import math
import jax
import jax.numpy as jnp
from jax import lax
import numpy as np

D_MODEL = 1024
BATCH = 2
SEQ = 8192
DEPTH = 2
DEC_BATCH = 128
DEC_SEQ = 4
PAST_LEN = 8192
PAGE_SIZE = 128

D_HEAD = 64
ROPE_THETA = 10000.0
EPS = 1e-6
NEG = -1e30
NSA_HEADS = 4
NSA_BLK = 64
NSA_TOPN = 16
NSA_WINDOW = 512
NSA_FORCE = 1e9
MLA_HEADS = 8
MLA_Q_RANK = 256
MLA_KV_RANK = 256
MLA_NOPE = 64
MLA_ROPE = 32
MLA_V = 64
MLA_QK = MLA_NOPE + MLA_ROPE
DSA_HEADS = 4
IDX_HEADS = 8
IDX_DIM = 32
DSA_TOPK_MAX = 256
D_FF = 2816
CONV_W = 3
PLE_DIM = 256
Q_BLOCK = 128
SAMPLE_CHUNK = 16

IN_SPLITS = (NSA_HEADS * D_HEAD, D_HEAD, D_HEAD, D_HEAD, D_HEAD, D_HEAD, D_HEAD, 3 * NSA_HEADS,
             MLA_Q_RANK, MLA_KV_RANK, MLA_ROPE,
             DSA_HEADS * D_HEAD, D_HEAD, D_HEAD, IDX_HEADS * IDX_DIM, IDX_DIM, IDX_HEADS)
IN_COLS = sum(IN_SPLITS)
MIX_WIDTH = NSA_HEADS * D_HEAD + MLA_HEADS * MLA_V + DSA_HEADS * D_HEAD
NSA_ROW = 4 * D_HEAD
MLA_ROW = MLA_KV_RANK + MLA_ROPE
DSA_ROW = 2 * D_HEAD + IDX_DIM
WIN_ROW = 2 * D_HEAD

kernel_name = 'hybrid_nsa_mla_dsa_convffn_step'


def rms_norm(x, g):
    xf = x.astype(jnp.float32)
    y = xf * lax.rsqrt(jnp.mean(xf * xf, axis=-1, keepdims=True) + EPS)
    return (y * g.astype(jnp.float32)).astype(x.dtype)


def rope(x, pos):
    half = x.shape[-1] // 2
    inv = ROPE_THETA ** (-jnp.arange(half, dtype=jnp.float32) / half)
    ang = pos.astype(jnp.float32)[:, None] * inv[None, :]
    cos = jnp.cos(ang)[:, None, :]
    sin = jnp.sin(ang)[:, None, :]
    xf = x.astype(jnp.float32)
    x1, x2 = xf[..., :half], xf[..., half:]
    return jnp.concatenate([x1 * cos - x2 * sin, x2 * cos + x1 * sin], axis=-1).astype(x.dtype)


def rope_rows(x, pos):
    return rope(x[:, :, None, :], pos)[:, :, 0, :]


def split_last(x, sizes):
    return jnp.split(x, np.cumsum(sizes)[:-1].tolist(), axis=-1)


def masked_softmax(s, mask):
    s = jnp.where(mask, s.astype(jnp.float32), NEG)
    return jnp.where(mask, jax.nn.softmax(s, axis=-1), 0.0)


def gather_pages(pool, pt):
    g = pool[pt]
    return g.reshape(pt.shape[0], -1, pool.shape[-1])


def project(xn, pos, lp):
    B, T, _ = xn.shape
    z = xn @ lp['w_in']
    (qa, kc, vc, ks, vs, kw, vw, ga, cq, ckv, kpe, qc, kd, vd, qi, ki, wi) = split_last(z, IN_SPLITS)
    qa = rope(rms_norm(qa.reshape(B, T, NSA_HEADS, D_HEAD), lp['nsa_qn']), pos)
    kc = rms_norm(kc, lp['nsa_kn'][0])
    ks = rope_rows(rms_norm(ks, lp['nsa_kn'][1]), pos)
    kw = rope_rows(rms_norm(kw, lp['nsa_kn'][2]), pos)
    ga = jax.nn.sigmoid(ga).reshape(B, T, NSA_HEADS, 3)
    q = (rms_norm(cq, lp['mla_cqn']) @ lp['mla_w_uq']).reshape(B, T, MLA_HEADS, MLA_QK)
    qm = rms_norm(jnp.concatenate([q[..., :MLA_NOPE], rope(q[..., MLA_NOPE:], pos)], axis=-1), lp['mla_qn'])
    ckv = rms_norm(ckv, lp['mla_ckvn'])
    kpe = rope_rows(kpe, pos)
    qc = rope(rms_norm(qc.reshape(B, T, DSA_HEADS, D_HEAD), lp['dsa_qn']), pos)
    kd = rope_rows(rms_norm(kd, lp['dsa_kn']), pos)
    qi = rope(qi.reshape(B, T, IDX_HEADS, IDX_DIM), pos)
    ki = rope_rows(ki, pos)
    wi = wi * (IDX_HEADS ** -0.5)
    qs = (qa, ga, qm, qc, qi, wi)
    rows_nsa = jnp.concatenate([kc, vc, ks, vs], axis=-1)
    rows_mla = jnp.concatenate([ckv, kpe], axis=-1)
    rows_dsa = jnp.concatenate([kd, vd, ki], axis=-1)
    rows_win = jnp.concatenate([kw, vw], axis=-1)
    return qs, rows_nsa, rows_mla, rows_dsa, rows_win


def prep_keys(rows_nsa, rows_mla, rows_dsa, lp):
    kc, vc, ks, vs = split_last(rows_nsa, (D_HEAD, D_HEAD, D_HEAD, D_HEAD))
    B, L, _ = kc.shape
    nb = -(-L // NSA_BLK)
    pad = ((0, 0), (0, nb * NSA_BLK - L), (0, 0))

    def blocks(a):
        return jnp.pad(a, pad).reshape(B, nb, NSA_BLK, D_HEAD)

    kcmp = jnp.einsum('bnjd,jd->bnd', blocks(kc), lp['nsa_cmp_pos'][0]) / NSA_BLK
    vcmp = jnp.einsum('bnjd,jd->bnd', blocks(vc), lp['nsa_cmp_pos'][1]) / NSA_BLK
    kcmp = rope_rows(kcmp, jnp.arange(nb) * NSA_BLK + (NSA_BLK - 1))
    ksb, vsb = blocks(ks), blocks(vs)
    ckv, kpe = split_last(rows_mla, (MLA_KV_RANK, MLA_ROPE))
    kv = (ckv @ lp['mla_w_ukv']).reshape(B, L, MLA_HEADS, MLA_NOPE + MLA_V)
    km = jnp.concatenate([kv[..., :MLA_NOPE],
                          jnp.broadcast_to(kpe[:, :, None, :], (B, L, MLA_HEADS, MLA_ROPE))], axis=-1)
    km = rms_norm(km, lp['mla_kn'])
    vm = kv[..., MLA_NOPE:]
    kd, vd, ki = split_last(rows_dsa, (D_HEAD, D_HEAD, IDX_DIM))
    return (kcmp, vcmp, ksb, vsb, km, vm, kd, vd, ki)


def nsa_branch(q, g, q_pos, kcmp, vcmp, ksb, vsb, kw, vw, w_pos):
    B, T, H, D = q.shape
    nb = kcmp.shape[1]
    scale = D ** -0.5
    blk = jnp.arange(nb)
    s = jnp.einsum('bthd,bnd->bhtn', q, kcmp) * scale
    p_c = masked_softmax(s, (blk * NSA_BLK + NSA_BLK - 1)[None, :] <= q_pos[:, None])
    o_c = jnp.einsum('bhtn,bnd->bthd', p_c.astype(vcmp.dtype), vcmp)
    cur = (q_pos // NSA_BLK)[None, :, None]
    imp = jnp.sum(p_c, axis=1)
    imp = jnp.where(blk == cur, NSA_FORCE, jnp.where(blk < cur, imp, -1.0))
    n_sel = min(NSA_TOPN, nb)
    _, idx = lax.top_k(imp, n_sel)
    take = jax.vmap(lambda a, i: a[i])
    k_sel = take(ksb, idx).reshape(B, T, n_sel * NSA_BLK, D)
    v_sel = take(vsb, idx).reshape(B, T, n_sel * NSA_BLK, D)
    tok_pos = (idx[..., None] * NSA_BLK + jnp.arange(NSA_BLK)).reshape(B, T, n_sel * NSA_BLK)
    s = jnp.einsum('bthd,btsd->bhts', q, k_sel) * scale
    p_s = masked_softmax(s, (tok_pos <= q_pos[None, :, None])[:, None])
    o_s = jnp.einsum('bhts,btsd->bthd', p_s.astype(v_sel.dtype), v_sel)
    s = jnp.einsum('bthd,bsd->bhts', q, kw) * scale
    rel = q_pos[:, None] - w_pos[None, :]
    p_w = masked_softmax(s, (rel >= 0) & (rel < NSA_WINDOW) & (w_pos[None, :] >= 0))
    o_w = jnp.einsum('bhts,bsd->bthd', p_w.astype(vw.dtype), vw)
    o = g[..., 0:1] * o_c + g[..., 1:2] * o_s + g[..., 2:3] * o_w
    return o.reshape(B, T, H * D)


def mla_attend(q, k, v, q_pos):
    k_pos = jnp.arange(k.shape[1])
    s = jnp.einsum('bthd,bshd->bhts', q, k) * (MLA_QK ** -0.5)
    p = masked_softmax(s, k_pos[None, :] <= q_pos[:, None])
    o = jnp.einsum('bhts,bshd->bthd', p.astype(v.dtype), v)
    return o.reshape(o.shape[0], o.shape[1], -1)


def dsa_branch(q, qi, wi, q_pos, kd, vd, ki):
    B, T, H, D = q.shape
    L = kd.shape[1]
    k_pos = jnp.arange(L)
    rel = jax.nn.relu(jnp.einsum('btgi,bsi->btgs', qi, ki).astype(jnp.float32))
    score = jnp.einsum('btg,btgs->bts', wi.astype(jnp.float32), rel)
    score = jnp.where(k_pos[None, None, :] <= q_pos[None, :, None], score, NEG)
    n_top = min(DSA_TOPK_MAX, L // 4)
    _, idx = lax.top_k(score, n_top)
    take = jax.vmap(lambda a, i: a[i])
    k_sel = take(kd, idx)
    v_sel = take(vd, idx)
    s = jnp.einsum('bthd,btsd->bhts', q, k_sel) * (D ** -0.5)
    p = masked_softmax(s, (idx <= q_pos[None, :, None])[:, None])
    o = jnp.einsum('bhts,btsd->bthd', p.astype(v_sel.dtype), v_sel)
    return o.reshape(B, T, H * D)


def attend_block(qs, q_pos, keys, win, w_pos):
    qa, ga, qm, qc, qi, wi = qs
    kcmp, vcmp, ksb, vsb, km, vm, kd, vd, ki = keys
    kw, vw = split_last(win, (D_HEAD, D_HEAD))
    o_a = nsa_branch(qa, ga, q_pos, kcmp, vcmp, ksb, vsb, kw, vw, w_pos)
    o_m = mla_attend(qm, km, vm, q_pos)
    o_c = dsa_branch(qc, qi, wi, q_pos, kd, vd, ki)
    return jnp.concatenate([o_a, o_m, o_c], axis=-1)


def prompt_mixers(xn, lp):
    B, T, _ = xn.shape
    pos = jnp.arange(T)
    qs, r_n, r_m, r_d, r_w = project(xn, pos, lp)
    keys = prep_keys(r_n, r_m, r_d, lp)
    w_pad = jnp.pad(r_w, ((0, 0), (NSA_WINDOW, 0), (0, 0)))

    def q_block(j):
        s0 = j * Q_BLOCK
        qs_b = tuple(lax.dynamic_slice_in_dim(a, s0, Q_BLOCK, axis=1) for a in qs)
        q_pos = s0 + jnp.arange(Q_BLOCK)
        win = lax.dynamic_slice_in_dim(w_pad, s0, NSA_WINDOW + Q_BLOCK, axis=1)
        w_pos = s0 - NSA_WINDOW + jnp.arange(NSA_WINDOW + Q_BLOCK)
        return attend_block(qs_b, q_pos, keys, win, w_pos)

    o = lax.map(q_block, jnp.arange(T // Q_BLOCK))
    o = jnp.swapaxes(o, 0, 1).reshape(B, T, MIX_WIDTH)
    w_keep = min(NSA_WINDOW, T)
    return o, r_n, r_m, r_d, r_w[:, T - w_keep:]


def sample_mixers(xn, past_len, page_table, pool_nsa, pool_mla, pool_dsa, win_state, lp):
    DB, T, _ = xn.shape
    pos = past_len + jnp.arange(T)
    qs, r_n, r_m, r_d, r_w = project(xn, pos, lp)
    w_buf = win_state.shape[1]
    win_all = jnp.concatenate([win_state.astype(r_w.dtype), r_w], axis=1)
    w_pos = past_len - w_buf + jnp.arange(w_buf + T)
    cs = math.gcd(DB, SAMPLE_CHUNK)

    def chunked(a):
        return a.reshape((DB // cs, cs) + a.shape[1:])

    def one_chunk(args):
        pt, qa, ga, qm, qc, qi, wi, rn, rm, rd, rw = args
        full_n = jnp.concatenate([gather_pages(pool_nsa, pt).astype(rn.dtype), rn], axis=1)
        full_m = jnp.concatenate([gather_pages(pool_mla, pt).astype(rm.dtype), rm], axis=1)
        full_d = jnp.concatenate([gather_pages(pool_dsa, pt).astype(rd.dtype), rd], axis=1)
        keys = prep_keys(full_n, full_m, full_d, lp)
        return attend_block((qa, ga, qm, qc, qi, wi), pos, keys, rw, w_pos)

    o = lax.map(one_chunk, tuple(chunked(a) for a in (page_table, *qs, r_n, r_m, r_d, win_all)))
    return o.reshape(DB, T, MIX_WIDTH), r_n, r_m, r_d, win_all[:, T:]


def conv_ffn(xn, prefix, lp):
    a = xn @ lp['ffn_w_up']
    b = xn @ lp['ffn_w_gate']
    T = a.shape[1]
    a_ext = jnp.concatenate([prefix.astype(a.dtype), a], axis=1)
    c = lp['ffn_conv_b'] + lp['ffn_conv_w'][CONV_W - 1] * a
    for j in range(CONV_W - 1):
        c = c + lp['ffn_conv_w'][j] * a_ext[:, j:j + T]
    y = (jax.nn.silu(c) * b) @ lp['ffn_w_down']
    return y, a_ext[:, T:]


def ffn_and_ple(h, prefix, p_i, lp):
    y, new_conv = conv_ffn(rms_norm(h, lp['norm_ffn']), prefix, lp)
    h = h + y
    gate = jax.nn.sigmoid(rms_norm(h, lp['norm_ple']) @ lp['ple_w_gate'])
    h = h + gate * (p_i @ lp['ple_w_proj'])
    return h, new_conv


def setup_inputs(seed: int = 0) -> dict:
    key = jax.random.key(seed)
    k = jax.random.split(key, 33)

    def nrm(j, shape, scale=1.0):
        return jax.random.normal(k[j], shape, jnp.float32) * scale

    def gain(j, shape):
        return 1.0 + 0.1 * nrm(j, shape)

    n_pages = PAST_LEN // PAGE_SIZE
    used = DEC_BATCH * n_pages
    n_pool = used + max(1, used // 4)
    w_buf = min(NSA_WINDOW, PAST_LEN)
    page_table = jax.random.permutation(k[7], n_pool)[:used].reshape(DEC_BATCH, n_pages).astype(jnp.int32)
    return {
        'x_prompt': nrm(0, (BATCH, SEQ, D_MODEL)),
        'x_sample': nrm(1, (DEC_BATCH, DEC_SEQ, D_MODEL)),
        'cache_nsa': nrm(2, (DEPTH, n_pool, PAGE_SIZE, NSA_ROW)),
        'cache_mla': nrm(3, (DEPTH, n_pool, PAGE_SIZE, MLA_ROW)),
        'cache_dsa': nrm(4, (DEPTH, n_pool, PAGE_SIZE, DSA_ROW)),
        'state_win': nrm(5, (DEPTH, DEC_BATCH, w_buf, WIN_ROW)),
        'state_conv': nrm(6, (DEPTH, DEC_BATCH, CONV_W - 1, D_FF)),
        'page_table': page_table,
        'p_prompt': nrm(8, (DEPTH, BATCH, SEQ, PLE_DIM)),
        'p_sample': nrm(9, (DEPTH, DEC_BATCH, DEC_SEQ, PLE_DIM)),
        'norm_mix': gain(10, (DEPTH, D_MODEL)),
        'w_in': nrm(11, (DEPTH, D_MODEL, IN_COLS), D_MODEL ** -0.5),
        'nsa_qn': gain(12, (DEPTH, D_HEAD)),
        'nsa_kn': gain(13, (DEPTH, 3, D_HEAD)),
        'nsa_cmp_pos': gain(14, (DEPTH, 2, NSA_BLK, D_HEAD)),
        'mla_cqn': gain(15, (DEPTH, MLA_Q_RANK)),
        'mla_ckvn': gain(16, (DEPTH, MLA_KV_RANK)),
        'mla_w_uq': nrm(17, (DEPTH, MLA_Q_RANK, MLA_HEADS * MLA_QK), MLA_Q_RANK ** -0.5),
        'mla_w_ukv': nrm(18, (DEPTH, MLA_KV_RANK, MLA_HEADS * (MLA_NOPE + MLA_V)), MLA_KV_RANK ** -0.5),
        'mla_qn': gain(19, (DEPTH, MLA_QK)),
        'mla_kn': gain(20, (DEPTH, MLA_QK)),
        'dsa_qn': gain(21, (DEPTH, D_HEAD)),
        'dsa_kn': gain(22, (DEPTH, D_HEAD)),
        'w_out': nrm(23, (DEPTH, MIX_WIDTH, D_MODEL), MIX_WIDTH ** -0.5),
        'norm_ffn': gain(24, (DEPTH, D_MODEL)),
        'ffn_w_up': nrm(25, (DEPTH, D_MODEL, D_FF), D_MODEL ** -0.5),
        'ffn_w_gate': nrm(26, (DEPTH, D_MODEL, D_FF), D_MODEL ** -0.5),
        'ffn_conv_w': nrm(27, (DEPTH, CONV_W, D_FF), CONV_W ** -0.5),
        'ffn_conv_b': nrm(28, (DEPTH, D_FF), 0.01),
        'ffn_w_down': nrm(29, (DEPTH, D_FF, D_MODEL), D_FF ** -0.5),
        'norm_ple': gain(30, (DEPTH, D_MODEL)),
        'ple_w_gate': nrm(31, (DEPTH, D_MODEL, D_MODEL), D_MODEL ** -0.5),
        'ple_w_proj': nrm(32, (DEPTH, PLE_DIM, D_MODEL), PLE_DIM ** -0.5),
    }


def reference(x_prompt, x_sample, cache_nsa, cache_mla, cache_dsa, state_win, state_conv, page_table,
              p_prompt, p_sample, norm_mix, w_in, nsa_qn, nsa_kn, nsa_cmp_pos, mla_cqn, mla_ckvn,
              mla_w_uq, mla_w_ukv, mla_qn, mla_kn, dsa_qn, dsa_kn, w_out, norm_ffn, ffn_w_up,
              ffn_w_gate, ffn_conv_w, ffn_conv_b, ffn_w_down, norm_ple, ple_w_gate, ple_w_proj):
    hp, hs = x_prompt, x_sample
    B = x_prompt.shape[0]
    past_len = page_table.shape[1] * cache_nsa.shape[2]
    nsa_p, nsa_s, mla_p, mla_s, dsa_p, dsa_s = [], [], [], [], [], []
    win_p, win_s, conv_p, conv_s = [], [], [], []
    for i in range(DEPTH):
        lp = {
            'w_in': w_in[i], 'nsa_qn': nsa_qn[i], 'nsa_kn': nsa_kn[i], 'nsa_cmp_pos': nsa_cmp_pos[i],
            'mla_cqn': mla_cqn[i], 'mla_ckvn': mla_ckvn[i], 'mla_w_uq': mla_w_uq[i],
            'mla_w_ukv': mla_w_ukv[i], 'mla_qn': mla_qn[i], 'mla_kn': mla_kn[i],
            'dsa_qn': dsa_qn[i], 'dsa_kn': dsa_kn[i],
            'norm_ffn': norm_ffn[i], 'ffn_w_up': ffn_w_up[i], 'ffn_w_gate': ffn_w_gate[i],
            'ffn_conv_w': ffn_conv_w[i], 'ffn_conv_b': ffn_conv_b[i], 'ffn_w_down': ffn_w_down[i],
            'norm_ple': norm_ple[i], 'ple_w_gate': ple_w_gate[i], 'ple_w_proj': ple_w_proj[i],
        }
        o, r_n, r_m, r_d, r_w = prompt_mixers(rms_norm(hp, norm_mix[i]), lp)
        hp = hp + o @ w_out[i]
        hp, c_new = ffn_and_ple(hp, jnp.zeros((B, CONV_W - 1, D_FF), hp.dtype), p_prompt[i], lp)
        nsa_p.append(r_n); mla_p.append(r_m); dsa_p.append(r_d); win_p.append(r_w); conv_p.append(c_new)
        o, r_n, r_m, r_d, r_w = sample_mixers(rms_norm(hs, norm_mix[i]), past_len, page_table,
                                              cache_nsa[i], cache_mla[i], cache_dsa[i], state_win[i], lp)
        hs = hs + o @ w_out[i]
        hs, c_new = ffn_and_ple(hs, state_conv[i], p_sample[i], lp)
        nsa_s.append(r_n); mla_s.append(r_m); dsa_s.append(r_d); win_s.append(r_w); conv_s.append(c_new)
    return (hp, hs,
            jnp.stack(nsa_p), jnp.stack(nsa_s), jnp.stack(mla_p), jnp.stack(mla_s),
            jnp.stack(dsa_p), jnp.stack(dsa_s), jnp.stack(win_p), jnp.stack(win_s),
            jnp.stack(conv_p), jnp.stack(conv_s))
```

```python
import functools
import math

import numpy as np
import jax
import jax.numpy as jnp
from jax import lax
from jax.experimental import pallas as pl
from jax.experimental.pallas import tpu as pltpu

F32 = jnp.float32
I32 = jnp.int32
MXU_DTYPE = jnp.bfloat16

D_MODEL = 1024
D_HEAD = 64
ROPE_THETA = 10000.0
EPS = 1e-6
NEG = -1e30
NSA_HEADS = 4
NSA_BLK = 64
NSA_TOPN = 16
NSA_WINDOW = 512
NSA_FORCE = 1e9
MLA_HEADS = 8
MLA_Q_RANK = 256
MLA_KV_RANK = 256
MLA_NOPE = 64
MLA_ROPE = 32
MLA_V = 64
MLA_QK = MLA_NOPE + MLA_ROPE
DSA_HEADS = 4
IDX_HEADS = 8
IDX_DIM = 32
DSA_TOPK_MAX = 256
D_FF = 2816
CONV_W = 3
PLE_DIM = 256
PAGE = 128
LANE = 128
SUB = 8
SAMPLE_T = 8

C_QA, C_KC, C_VC, C_KS, C_VS, C_KW, C_VW, C_GA = 0, 256, 320, 384, 448, 512, 576, 640
C_CQ, C_CKV, C_KPE, C_QC, C_KD, C_VD, C_QI, C_KI, C_WI = 652, 908, 1164, 1196, 1452, 1516, 1580, 1836, 1868

(Z_QA, Z_QAS, Z_NSA, Z_NSAS, Z_WIN, Z_WINS, Z_MISC, Z_CQ, Z_CKV, Z_KPE, Z_KPES, Z_QC, Z_QCS,
 Z_DKV, Z_DKVS, Z_KI, Z_KIS, Z_QI, Z_QIS, Z_TOTAL) = (
    0, 256, 512, 768, 1024, 1152, 1280, 1408, 1664, 1920, 2048, 2176, 2432, 2688, 2816, 2944,
    3072, 3200, 3456, 3712)

(T_64C, T_64S, T_NSAC, T_NSAS, T_KVC, T_KVS, T_QC, T_QS, T_32C, T_32S, T_328C, T_328S, T_TOTAL) = (
    0, 256, 512, 768, 1024, 1152, 1280, 2304, 3328, 3456, 3584, 3840, 4096)

(GV_NMIX, GV_QA_A, GV_QA_B, GV_NSA_A, GV_NSA_B, GV_WIN_A, GV_WIN_B, GV_CQ, GV_CKV, GV_QC_A, GV_QC_B,
 GV_DKV_A, GV_DKV_B, GV_Q, GV_TOTAL) = (
    0, 1024, 1280, 1536, 1792, 2048, 2176, 2304, 2560, 2816, 3072, 3328, 3456, 3584, 4608)


def _swap_idx(base, d, n):
    l = np.arange(n * d)
    return base + (l // d) * d + ((l % d) + d // 2) % d


def _win_cols():
    cols = np.full((Z_TOTAL,), -1, np.int64)

    def put(off, idx, at=0):
        cols[off + at: off + at + len(idx)] = idx

    put(Z_QA, C_QA + np.arange(256))
    put(Z_QAS, _swap_idx(C_QA, 64, 4))
    put(Z_NSA, C_KC + np.arange(256))
    put(Z_NSAS, _swap_idx(C_KS, 64, 1), at=128)
    put(Z_WIN, C_KW + np.arange(128))
    put(Z_WINS, _swap_idx(C_KW, 64, 1))
    put(Z_MISC, C_GA + np.arange(12))
    put(Z_MISC, C_WI + np.arange(8), at=12)
    put(Z_CQ, C_CQ + np.arange(256))
    put(Z_CKV, C_CKV + np.arange(256))
    put(Z_KPE, C_KPE + np.arange(32))
    put(Z_KPES, _swap_idx(C_KPE, 32, 1))
    put(Z_QC, C_QC + np.arange(256))
    put(Z_QCS, _swap_idx(C_QC, 64, 4))
    put(Z_DKV, C_KD + np.arange(128))
    put(Z_DKVS, _swap_idx(C_KD, 64, 1))
    put(Z_KI, C_KI + np.arange(32))
    put(Z_KIS, _swap_idx(C_KI, 32, 1))
    put(Z_QI, C_QI + np.arange(256))
    put(Z_QIS, _swap_idx(C_QI, 32, 8))
    return cols


def _wuq_cols():
    cols = np.full((2 * MLA_HEADS * LANE,), -1, np.int64)
    for h in range(MLA_HEADS):
        cols[h * LANE: h * LANE + MLA_QK] = h * MLA_QK + np.arange(MLA_QK)
        cols[1024 + h * LANE + MLA_NOPE: 1024 + h * LANE + MLA_QK] = _swap_idx(h * MLA_QK + MLA_NOPE, MLA_ROPE, 1)
    return cols


def _take_cols(w, cols):
    g = jnp.take(w, jnp.asarray(np.maximum(cols, 0)), axis=1)
    return jnp.where(jnp.asarray(cols >= 0)[None, :], g, 0.0)


def _blockdiag(n, d):
    i = np.arange(n)
    return (i[:, None] // d == i[None, :] // d).astype(np.float32)


def _placement(n_in, n_out, pairs):
    m = np.zeros((n_in, n_out), np.float32)
    for s, d in pairs:
        m[s, d] = 1.0
    return m


def _q_placement(n_heads, d, width, at):
    pairs = [(h * d + j, h * width + at + j) for h in range(n_heads) for j in range(d)]
    return _placement(n_heads * d, n_heads * width, pairs)


def _rope_tables(pos):
    pos = pos.astype(F32)[:, None]
    P = pos.shape[0]

    def cs(d):
        half = d // 2
        inv = ROPE_THETA ** (-jnp.arange(half, dtype=F32) / half)
        ang = pos * inv[None, :]
        c, s = jnp.cos(ang), jnp.sin(ang)
        return jnp.concatenate([c, c], axis=1), jnp.concatenate([-s, s], axis=1)

    c64, s64 = cs(64)
    c32, s32 = cs(32)
    one = lambda n: jnp.ones((P, n), F32)
    zero = lambda n: jnp.zeros((P, n), F32)
    segs = [
        jnp.tile(c64, (1, 4)), jnp.tile(s64, (1, 4)),
        jnp.concatenate([one(128), c64, one(64)], 1), jnp.concatenate([zero(128), s64, zero(64)], 1),
        jnp.concatenate([c64, one(64)], 1), jnp.concatenate([s64, zero(64)], 1),
        jnp.tile(jnp.concatenate([one(64), c32, one(32)], 1), (1, 8)),
        jnp.tile(jnp.concatenate([zero(64), s32, zero(32)], 1), (1, 8)),
        jnp.concatenate([c32, one(96)], 1), jnp.concatenate([s32, zero(96)], 1),
        jnp.tile(c32, (1, 8)), jnp.tile(s32, (1, 8)),
    ]
    return jnp.concatenate(segs, axis=1)


def _cmp_tables(nbp):
    pos = (jnp.arange(nbp) * NSA_BLK + (NSA_BLK - 1)).astype(F32)[:, None]
    inv = ROPE_THETA ** (-jnp.arange(32, dtype=F32) / 32)
    ang = pos * inv[None, :]
    c, s = jnp.cos(ang), jnp.sin(ang)
    one = jnp.ones((nbp, 192), F32)
    zero = jnp.zeros((nbp, 192), F32)
    return jnp.concatenate([c, c, one, -s, s, zero], axis=1)


def _swap64(g):
    return jnp.concatenate([g[32:], g[:32]])


def _gain_vector(norm_mix, nsa_qn, nsa_kn, mla_cqn, mla_ckvn, dsa_qn, dsa_kn, mla_qn):
    o64, z64 = jnp.ones((64,), F32), jnp.zeros((64,), F32)
    qpad = jnp.concatenate([mla_qn, jnp.zeros((32,), F32)])
    segs = [
        norm_mix,
        jnp.tile(nsa_qn, 4), jnp.tile(_swap64(nsa_qn), 4),
        jnp.concatenate([nsa_kn[0], o64, nsa_kn[1], o64]), jnp.concatenate([z64, z64, _swap64(nsa_kn[1]), z64]),
        jnp.concatenate([nsa_kn[2], o64]), jnp.concatenate([_swap64(nsa_kn[2]), z64]),
        mla_cqn, mla_ckvn,
        jnp.tile(dsa_qn, 4), jnp.tile(_swap64(dsa_qn), 4),
        jnp.concatenate([dsa_kn, o64]), jnp.concatenate([_swap64(dsa_kn), z64]),
        jnp.tile(qpad, 8),
    ]
    return jnp.concatenate(segs)[None, :]


def _mm(a, b):
    return jnp.dot(a.astype(MXU_DTYPE), b.astype(MXU_DTYPE), preferred_element_type=F32)


def _mm_nt(a, b):
    return lax.dot_general(a.astype(MXU_DTYPE), b.astype(MXU_DTYPE), (((1,), (1,)), ((), ())),
                           preferred_element_type=F32)


def _split3(x):
    x1 = x.astype(jnp.bfloat16)
    r1 = x - x1.astype(F32)
    x2 = r1.astype(jnp.bfloat16)
    x3 = (r1 - x2.astype(F32)).astype(jnp.bfloat16)
    return x1, x2, x3


def _place(x, p):
    x1, x2, x3 = _split3(x)
    d = lambda a: jnp.dot(a, p, preferred_element_type=F32)
    return d(x1) + d(x2) + d(x3)


def _gsum(x2, g):
    hi = x2.astype(jnp.bfloat16)
    lo = (x2 - hi.astype(F32)).astype(jnp.bfloat16)
    return jnp.dot(hi, g, preferred_element_type=F32) + jnp.dot(lo, g, preferred_element_type=F32)


def _iota(shape, dim):
    return lax.broadcasted_iota(I32, shape, dim)


def _stack_heads(x, n, w):
    return jnp.concatenate([x[:, h * w:(h + 1) * w] for h in range(n)], axis=0)


def _softmax_rows(s, mask):
    s = jnp.where(mask, s, NEG)
    m = jnp.max(s, axis=-1, keepdims=True)
    e = jnp.where(mask, jnp.exp(s - m), 0.0)
    l = jnp.sum(e, axis=-1, keepdims=True)
    return e / jnp.where(l > 0.0, l, 1.0)


def _online_update(carry, s, mask, v):
    m, l, acc = carry
    if mask is not None:
        s = jnp.where(mask, s, NEG)
    m_new = jnp.maximum(m, jnp.max(s, axis=-1, keepdims=True))
    alpha = jnp.exp(m - m_new)
    p = jnp.exp(s - m_new)
    if mask is not None:
        p = jnp.where(mask, p, 0.0)
    l = alpha * l + jnp.sum(p, axis=-1, keepdims=True)
    acc = alpha * acc + _mm(p, v)
    return m_new, l, acc


def _finish(l, acc):
    return acc / jnp.where(l > 0.0, l, 1.0)


def _sortable(x):
    b = lax.bitcast_convert_type(x, I32)
    return jnp.where(b < 0, b ^ jnp.int32(0x7FFFFFFF), b)


def _cparams(sem, vmem_mb):
    return pltpu.CompilerParams(dimension_semantics=sem, vmem_limit_bytes=vmem_mb * 1024 * 1024)


def _proj_kernel(x_ref, tab_ref, gv_ref, w_ref, wuq_ref, g256_ref, g1024_ref,
                 qa_o, qm_o, qc_o, qi_o, misc_o, rn_o, rm_o, rd_o, rw_o, nk_o, wk_o, dk_o, kp_o):
    gv = lambda off, w: gv_ref[:, off:off + w]
    tab = lambda off, w: tab_ref[:, off:off + w]
    x = x_ref[...]
    xn = x * lax.rsqrt(jnp.mean(x * x, axis=-1, keepdims=True) + EPS) * gv(GV_NMIX, D_MODEL)
    z = _mm(xn, w_ref[...])
    zs = lambda off, w: z[:, off:off + w]
    g256 = g256_ref[...]
    g128 = g256_ref[0:128, 0:128]

    def group(zo, zso, w, gmat, d, ga, gb, tc, ts, normmask=None):
        a, asw = zs(zo, w), zs(zso, w)
        if gmat is None:
            rs = None
        else:
            rs = lax.rsqrt(_gsum(a * a, gmat) / d + EPS)
            if normmask is not None:
                rs = jnp.where(normmask, rs, 1.0)
        ca, cb = tab(tc, w), tab(ts, w)
        if ga is not None:
            ca, cb = gv(ga, w) * ca, gv(gb, w) * cb
        if rs is not None:
            ca, cb = rs * ca, rs * cb
        return a * ca + asw * cb

    qa_o[...] = group(Z_QA, Z_QAS, 256, g256, 64.0, GV_QA_A, GV_QA_B, T_64C, T_64S).astype(qa_o.dtype)
    lane = _iota((1, 256), 1)
    nm = (lane < 64) | ((lane >= 128) & (lane < 192))
    rn = group(Z_NSA, Z_NSAS, 256, g256, 64.0, GV_NSA_A, GV_NSA_B, T_NSAC, T_NSAS, nm)
    rn_o[...] = rn
    nk_o[...] = rn.astype(nk_o.dtype)
    nm128 = _iota((1, 128), 1) < 64
    rw = group(Z_WIN, Z_WINS, 128, g128, 64.0, GV_WIN_A, GV_WIN_B, T_KVC, T_KVS, nm128)
    rw_o[...] = rw
    wk_o[...] = rw.astype(wk_o.dtype)
    zm = zs(Z_MISC, 128)
    misc_o[...] = jnp.where(_iota((1, 128), 1) < 12, jax.nn.sigmoid(zm), zm * (IDX_HEADS ** -0.5))
    cq = zs(Z_CQ, 256)
    cqn = cq * lax.rsqrt(jnp.mean(cq * cq, axis=-1, keepdims=True) + EPS) * gv(GV_CQ, 256)
    q2 = _mm(cqn, wuq_ref[...])
    qr = q2[:, 0:1024] * tab(T_QC, 1024) + q2[:, 1024:2048] * tab(T_QS, 1024)
    rs = lax.rsqrt(_gsum(qr * qr, g1024_ref[...]) / float(MLA_QK) + EPS)
    qm_o[...] = (qr * rs * gv(GV_Q, 1024)).astype(qm_o.dtype)
    ckv = zs(Z_CKV, 256)
    rm_o[:, 0:256] = ckv * lax.rsqrt(jnp.mean(ckv * ckv, axis=-1, keepdims=True) + EPS) * gv(GV_CKV, 256)
    kp = group(Z_KPE, Z_KPES, 128, None, 0.0, None, None, T_32C, T_32S)
    rm_o[:, 256:288] = kp[:, 0:32]
    kp_o[...] = kp
    qc_o[...] = group(Z_QC, Z_QCS, 256, g256, 64.0, GV_QC_A, GV_QC_B, T_64C, T_64S).astype(qc_o.dtype)
    dkv = group(Z_DKV, Z_DKVS, 128, g128, 64.0, GV_DKV_A, GV_DKV_B, T_KVC, T_KVS, nm128)
    ki = group(Z_KI, Z_KIS, 128, None, 0.0, None, None, T_32C, T_32S)
    rd_o[:, 0:128] = dkv
    rd_o[:, 128:160] = ki[:, 0:32]
    dk_o[:, 0:128] = dkv.astype(dk_o.dtype)
    dk_o[:, 128:256] = ki.astype(dk_o.dtype)
    qi_o[...] = group(Z_QI, Z_QIS, 256, None, 0.0, None, None, T_328C, T_328S).astype(qi_o.dtype)


def _proj(x, tab, n_pos_tiles, gv, wbig, wuq, g256, g1024, tm):
    n = x.shape[0]
    row = lambda w: pl.BlockSpec((tm, w), lambda i: (i, 0))
    full = lambda a: pl.BlockSpec(a.shape, lambda i: (0, 0))
    widths = [(256, MXU_DTYPE), (1024, MXU_DTYPE), (256, MXU_DTYPE), (256, MXU_DTYPE), (128, F32),
              (256, F32), (288, F32), (160, F32), (128, F32),
              (256, MXU_DTYPE), (128, MXU_DTYPE), (256, MXU_DTYPE), (128, F32)]
    return pl.pallas_call(
        _proj_kernel,
        grid=(n // tm,),
        in_specs=[row(D_MODEL),
                  pl.BlockSpec((tm, T_TOTAL), lambda i: (i % n_pos_tiles, 0)),
                  full(gv), full(wbig), full(wuq), full(g256), full(g1024)],
        out_specs=[row(w) for w, _ in widths],
        out_shape=[jax.ShapeDtypeStruct((n, w), dt) for w, dt in widths],
        compiler_params=_cparams(("parallel",), 56),
        name="proj",
    )(x, tab, gv, wbig, wuq, g256, g1024)


def _mla_keys(ckv, kpe128, wk, wv, pk, g8, gk):
    kraw = _mm(ckv, wk) + _place(kpe128, pk)
    ss = _gsum(kraw * kraw, g8)
    rs = lax.rsqrt(ss / float(MLA_QK) + EPS)
    parts = [kraw[:, h * LANE:(h + 1) * LANE] * rs[:, h:h + 1] for h in range(MLA_HEADS)]
    km = jnp.concatenate(parts, axis=1) * gk
    return km.astype(MXU_DTYPE), _mm(ckv, wv).astype(MXU_DTYPE)


def _mla_prep_kernel(rm_ref, kp_ref, wk_ref, wv_ref, pk_ref, g8_ref, gk_ref, km_o, vm_o):
    km, vm = _mla_keys(rm_ref[:, 0:256], kp_ref[...], wk_ref[...], wv_ref[...], pk_ref[...],
                       g8_ref[...], gk_ref[...])
    km_o[...] = km
    vm_o[...] = vm


def _mla_prep(rm, kp, wk, wv, pk, g8, gk, tk):
    n = rm.shape[0]
    full = lambda a: pl.BlockSpec(a.shape, lambda i: (0, 0))
    return pl.pallas_call(
        _mla_prep_kernel,
        grid=(n // tk,),
        in_specs=[pl.BlockSpec((tk, 288), lambda i: (i, 0)), pl.BlockSpec((tk, 128), lambda i: (i, 0)),
                  full(wk), full(wv), full(pk), full(g8), full(gk)],
        out_specs=[pl.BlockSpec((tk, 1024), lambda i: (i, 0)), pl.BlockSpec((tk, 512), lambda i: (i, 0))],
        out_shape=[jax.ShapeDtypeStruct((n, 1024), MXU_DTYPE), jax.ShapeDtypeStruct((n, 512), MXU_DTYPE)],
        compiler_params=_cparams(("parallel",), 40),
        name="mla_prep",
    )(rm, kp, wk, wv, pk, g8, gk)


def _mla_flash_kernel(q_ref, k_ref, v_ref, o_ref, *, tq, tk):
    i = pl.program_id(2)
    scale = MLA_QK ** -0.5
    q_pos = i * tq + _iota((tq, 1), 0)
    outs = []
    for hh in range(2):
        q = q_ref[:, hh * LANE:(hh + 1) * LANE]

        def chunk(c, carry, masked):
            k0 = pl.multiple_of(c * tk, tk)
            k = k_ref[pl.ds(k0, tk), hh * LANE:(hh + 1) * LANE]
            v = v_ref[pl.ds(k0, tk), :]
            s = _mm_nt(q, k) * scale
            mask = ((k0 + _iota((1, tk), 1)) <= q_pos) if masked else None
            return _online_update(carry, s, mask, v)

        init = (jnp.full((tq, 1), NEG, F32), jnp.zeros((tq, 1), F32), jnp.zeros((tq, LANE), F32))
        n_full = (i * tq) // tk
        n_all = ((i + 1) * tq + tk - 1) // tk
        carry = lax.fori_loop(0, n_full, lambda c, cr: chunk(c, cr, False), init)
        carry = lax.fori_loop(n_full, n_all, lambda c, cr: chunk(c, cr, True), carry)
        outs.append(_finish(carry[1], carry[2]))
    lane = _iota((1, LANE), 1)
    o_ref[...] = jnp.where(lane < MLA_V, outs[0], outs[1])


def _mla_flash(qm, km, vm, nb, t, tq, tk):
    nq = t // tq
    kern = functools.partial(_mla_flash_kernel, tq=tq, tk=tk)
    return pl.pallas_call(
        kern,
        grid=(nb, MLA_HEADS // 2, nq),
        in_specs=[pl.BlockSpec((tq, 256), lambda b, h, i: (b * nq + i, h)),
                  pl.BlockSpec((t, 256), lambda b, h, i: (b, h)),
                  pl.BlockSpec((t, 128), lambda b, h, i: (b, h))],
        out_specs=pl.BlockSpec((tq, 128), lambda b, h, i: (b * nq + i, h)),
        out_shape=jax.ShapeDtypeStruct((nb * t, 512), F32),
        compiler_params=_cparams(("parallel", "parallel", "arbitrary"), 48),
        name="mla_flash",
    )(qm, km, vm)


def _mla_sample_kernel(pt_ref, *refs, pps, past):
    pages = refs[:pps]
    (q_ref, newm_ref, newkp_ref, wk_ref, wv_ref, pk_ref, g8_ref, gk_ref, o_ref,
     pad_s, m_s, l_s, acc_s) = refs[pps:]
    j = pl.program_id(1)
    nsteps = pl.num_programs(1)
    rows = SAMPLE_T * MLA_HEADS
    q = q_ref[0].astype(F32)
    head_of_lane = _iota((SUB, 1024), 1) // LANE
    sub = _iota((SUB, 1024), 0)
    qbd = jnp.concatenate(
        [jnp.where(head_of_lane == sub, jnp.broadcast_to(q[t:t + 1, :], (SUB, 1024)), 0.0)
         for t in range(SAMPLE_T)], axis=0).astype(MXU_DTYPE)
    q_pos = past + _iota((rows, 1), 0) // MLA_HEADS
    scale = MLA_QK ** -0.5

    @pl.when(j == 0)
    def _():
        m_s[...] = jnp.full(m_s.shape, NEG, F32)
        l_s[...] = jnp.zeros(l_s.shape, F32)
        acc_s[...] = jnp.zeros(acc_s.shape, F32)
        pad_s[...] = jnp.zeros(pad_s.shape, F32)

    def attend(ckv, kpe128, k0, n):
        km, vm = _mla_keys(ckv, kpe128, wk_ref[...], wv_ref[...], pk_ref[...], g8_ref[...], gk_ref[...])
        s = _mm_nt(qbd, km) * scale
        mask = (k0 + _iota((1, n), 1)) <= q_pos
        m, l, acc = _online_update((m_s[...], l_s[...], acc_s[...]), s, mask, vm)
        m_s[...] = m
        l_s[...] = l
        acc_s[...] = acc

    @pl.when(j < nsteps - 1)
    def _():
        for g in range(pps // 4):
            for p in range(4):
                pad_s[p * PAGE:(p + 1) * PAGE, 0:288] = pages[g * 4 + p][...]
            attend(pad_s[:, 0:256], pad_s[:, 256:384], (j * pps + g * 4) * PAGE, 4 * PAGE)

    @pl.when(j == nsteps - 1)
    def _():
        ckv = jnp.concatenate([newm_ref[0][:, 0:256], jnp.zeros((PAGE - SAMPLE_T, 256), F32)], axis=0)
        kpe = jnp.concatenate([newkp_ref[0], jnp.zeros((PAGE - SAMPLE_T, 128), F32)], axis=0)
        attend(ckv, kpe, past, PAGE)
        o = _finish(l_s[...], acc_s[...])
        lane_head = _iota((SUB, 512), 1) // MLA_V
        sub8 = _iota((SUB, 512), 0)
        out = jnp.zeros((SAMPLE_T, 512), F32)
        for t in range(SAMPLE_T):
            blk = jnp.where(lane_head == sub8, o[t * SUB:(t + 1) * SUB, :], 0.0)
            r = jnp.sum(blk, axis=0, keepdims=True)
            out = jnp.where(_iota((SAMPLE_T, 512), 0) == t, jnp.broadcast_to(r, (SAMPLE_T, 512)), out)
        o_ref[0] = out


def _mla_sample(page_table, cache, layer, qm, newm, newkp, wk, wv, pk, g8, gk, pps):
    db, n_pages = page_table.shape
    past = n_pages * PAGE
    nsteps = n_pages // pps + 1
    last = n_pages - 1

    def page_spec(p):
        return pl.BlockSpec((None, None, PAGE, 288),
                            lambda b, j, pt: (layer, pt[b, jnp.minimum(j * pps + p, last)], 0, 0))

    full = lambda a: pl.BlockSpec(a.shape, lambda b, j, pt: (0, 0))
    per_b = lambda w: pl.BlockSpec((1, SAMPLE_T, w), lambda b, j, pt: (b, 0, 0))
    rows = SAMPLE_T * MLA_HEADS
    kern = functools.partial(_mla_sample_kernel, pps=pps, past=past)
    return pl.pallas_call(
        kern,
        grid_spec=pltpu.PrefetchScalarGridSpec(
            num_scalar_prefetch=1,
            grid=(db, nsteps),
            in_specs=[page_spec(p) for p in range(pps)] + [per_b(1024), per_b(288), per_b(128),
                                                          full(wk), full(wv), full(pk), full(g8), full(gk)],
            out_specs=per_b(512),
            scratch_shapes=[pltpu.VMEM((4 * PAGE, 384), F32), pltpu.VMEM((rows, 1), F32),
                            pltpu.VMEM((rows, 1), F32), pltpu.VMEM((rows, 512), F32)],
        ),
        out_shape=jax.ShapeDtypeStruct((db, SAMPLE_T, 512), F32),
        compiler_params=_cparams(("parallel", "arbitrary"), 40),
        name="mla_sample",
    )(page_table, *([cache] * pps), qm, newm, newkp, wk, wv, pk, g8, gk)


def _cmp_blocks(rows, wpos):
    n = rows.shape[0] // NSA_BLK
    x = rows.reshape(n, NSA_BLK, 256) * wpos[None]
    return jnp.sum(x, axis=1) / float(NSA_BLK)


def _cmp_rope(craw, ctab, swc):
    return craw * ctab[:, 0:256] + _place(craw, swc) * ctab[:, 256:512]


def _nsa_cmp_kernel(rn_ref, wpos_ref, ctab_ref, swc_ref, o_ref):
    craw = _cmp_blocks(rn_ref[...], wpos_ref[...])
    o_ref[...] = _cmp_rope(craw, ctab_ref[...], swc_ref[...]).astype(o_ref.dtype)


def _nsa_cmp(rn, wpos, ctab, swc, nb, t):
    nblk = t // NSA_BLK
    tb = min(nblk, 16)
    nt = nblk // tb
    return pl.pallas_call(
        _nsa_cmp_kernel,
        grid=(nb, nt),
        in_specs=[pl.BlockSpec((tb * NSA_BLK, 256), lambda b, i: (b * nt + i, 0)),
                  pl.BlockSpec(wpos.shape, lambda b, i: (0, 0)),
                  pl.BlockSpec((tb, 512), lambda b, i: (i, 0)),
                  pl.BlockSpec(swc.shape, lambda b, i: (0, 0))],
        out_specs=pl.BlockSpec((tb, 256), lambda b, i: (b * nt + i, 0)),
        out_shape=jax.ShapeDtypeStruct((nb * nblk, 256), MXU_DTYPE),
        compiler_params=_cparams(("parallel", "parallel"), 32),
        name="nsa_cmp",
    )(rn, wpos, ctab, swc)


def _top_n_mask(x, n_sel):
    t, w = x.shape
    lane = _iota((t, w), 1).astype(F32)

    def body(_, carry):
        x, sel = carry
        m = jnp.max(x, axis=-1, keepdims=True)
        first = jnp.min(jnp.where(x == m, lane, float(w)), axis=-1, keepdims=True)
        hit = lane == first
        return jnp.where(hit, -jnp.inf, x), jnp.where(hit, 1.0, sel)

    _, sel = lax.fori_loop(0, n_sel, body, (x, jnp.zeros((t, w), F32)))
    return sel


def _nsa_attend(qa, misc, q_pos, cmp, nblk, key_ref, n_chunks, tkc, win, w_pos, pqc, pqs, pqw):
    tq = qa.shape[0]
    h = NSA_HEADS
    scale = D_HEAD ** -0.5
    nbp = cmp.shape[0]
    qp4 = jnp.concatenate([q_pos] * h, axis=0)
    qc_st = _stack_heads(_mm(qa, pqc), h, 256)
    blk = _iota((1, nbp), 1)
    s = _mm_nt(qc_st, cmp) * scale
    p_c = _softmax_rows(s, ((blk * NSA_BLK + (NSA_BLK - 1)) <= qp4) & (blk < nblk))
    o_c = _mm(p_c, cmp)
    imp = p_c[0:tq]
    for hh in range(1, h):
        imp = imp + p_c[hh * tq:(hh + 1) * tq]
    cur = q_pos // NSA_BLK
    imp = jnp.where(blk == cur, NSA_FORCE, jnp.where(blk < cur, imp, -1.0))
    imp = jnp.where(blk < nblk, imp, -jnp.inf)
    sel = _top_n_mask(imp, min(NSA_TOPN, nblk))
    qs_st = _stack_heads(_mm(qa, pqs), h, 256)
    blk_col = _iota((nbp, tkc), 0)
    tok_col = _iota((nbp, tkc), 1)

    def chunk(c, carry):
        k0 = pl.multiple_of(c * tkc, tkc)
        keys = key_ref[pl.ds(k0, tkc), :]
        expand = jnp.where(((k0 + tok_col) // NSA_BLK) == blk_col, 1.0, 0.0)
        tok = _mm(sel, expand)
        mask1 = (tok > 0.5) & ((k0 + _iota((1, tkc), 1)) <= q_pos)
        mask = jnp.concatenate([mask1] * h, axis=0)
        s = _mm_nt(qs_st, keys) * scale
        return _online_update(carry, s, mask, keys)

    init = (jnp.full((h * tq, 1), NEG, F32), jnp.zeros((h * tq, 1), F32), jnp.zeros((h * tq, 256), F32))
    _, l, acc = lax.fori_loop(0, n_chunks, chunk, init)
    o_s = _finish(l, acc)
    qw_st = _stack_heads(_mm(qa, pqw), h, 128)
    rel = qp4 - w_pos
    s = _mm_nt(qw_st, win) * scale
    p_w = _softmax_rows(s, (rel >= 0) & (rel < NSA_WINDOW) & (w_pos >= 0))
    o_w = _mm(p_w, win)
    gate = lambda jj: jnp.concatenate([misc[:, 3 * hh + jj:3 * hh + jj + 1] for hh in range(h)], axis=0)
    tsum = gate(0) * o_c[:, 0:128] + gate(1) * o_s[:, 128:256] + gate(2) * o_w
    tsum = jnp.where(_iota((1, 128), 1) >= D_HEAD, tsum, 0.0)
    return jnp.concatenate([tsum[hh * tq:(hh + 1) * tq] for hh in range(h)], axis=1)


def _nsa_prompt_kernel(qa_ref, misc_ref, cmp_ref, nk_ref, wk_ref, pqc_ref, pqs_ref, pqw_ref, o_ref,
                       *, tq, tkc, t):
    i = pl.program_id(1)
    s0 = i * tq
    q_pos = s0 + _iota((tq, 1), 0)
    w = NSA_WINDOW + tq
    kstart = pl.multiple_of(jnp.clip(s0 - NSA_WINDOW, 0, t - w), SUB)
    win = wk_ref[pl.ds(kstart, w), :]
    w_pos = kstart + _iota((1, w), 1)
    n_chunks = (s0 + tq + tkc - 1) // tkc
    o_ref[...] = _nsa_attend(qa_ref[...], misc_ref[...], q_pos, cmp_ref[...], t // NSA_BLK, nk_ref, n_chunks,
                             tkc, win, w_pos, pqc_ref[...], pqs_ref[...], pqw_ref[...])


def _nsa_prompt(qa, misc, cmp, nk, wk, pqc, pqs, pqw, nb, t, tq, tkc):
    nq = t // tq
    nblk = t // NSA_BLK
    kern = functools.partial(_nsa_prompt_kernel, tq=tq, tkc=tkc, t=t)
    full = lambda a: pl.BlockSpec(a.shape, lambda b, i: (0, 0))
    return pl.pallas_call(
        kern,
        grid=(nb, nq),
        in_specs=[pl.BlockSpec((tq, 256), lambda b, i: (b * nq + i, 0)),
                  pl.BlockSpec((tq, 128), lambda b, i: (b * nq + i, 0)),
                  pl.BlockSpec((nblk, 256), lambda b, i: (b, 0)),
                  pl.BlockSpec((t, 256), lambda b, i: (b, 0)),
                  pl.BlockSpec((t, 128), lambda b, i: (b, 0)),
                  full(pqc), full(pqs), full(pqw)],
        out_specs=pl.BlockSpec((tq, 512), lambda b, i: (b * nq + i, 0)),
        out_shape=jax.ShapeDtypeStruct((nb * t, 512), F32),
        compiler_params=_cparams(("parallel", "arbitrary"), 48),
        name="nsa_prompt",
    )(qa, misc, cmp, nk, wk, pqc, pqs, pqw)


def _nsa_sample_kernel(pt_ref, *refs, pps, past, tkc):
    pages = refs[:pps]
    (qa_ref, misc_ref, newn_ref, sw_ref, neww_ref, wpos_ref, ctab_ref, swc_ref, pqc_ref, pqs_ref, pqw_ref,
     o_ref, key_s, cmp_s, win_s) = refs[pps:]
    j = pl.program_id(1)
    nsteps = pl.num_programs(1)
    nblk = past // NSA_BLK + 1
    bps = pps * PAGE // NSA_BLK

    @pl.when(j == 0)
    def _():
        key_s[...] = jnp.zeros(key_s.shape, key_s.dtype)
        cmp_s[...] = jnp.zeros(cmp_s.shape, F32)

    @pl.when(j < nsteps - 1)
    def _():
        rows = jnp.concatenate([pages[p][...] for p in range(pps)], axis=0)
        r0 = pl.multiple_of(j * pps * PAGE, pps * PAGE)
        key_s[pl.ds(r0, pps * PAGE), :] = rows.astype(key_s.dtype)
        b0 = pl.multiple_of(j * bps, bps)
        cmp_s[pl.ds(b0, bps), :] = _cmp_blocks(rows, wpos_ref[...])

    @pl.when(j == nsteps - 1)
    def _():
        newn = newn_ref[0]
        tail = jnp.concatenate([newn, jnp.zeros((NSA_BLK - SAMPLE_T, 256), F32)], axis=0)
        key_s[past:past + NSA_BLK, :] = tail.astype(key_s.dtype)
        cmp_s[past // NSA_BLK: past // NSA_BLK + 1, :] = _cmp_blocks(tail, wpos_ref[...])
        cmp = _cmp_rope(cmp_s[...], ctab_ref[...], swc_ref[...]).astype(MXU_DTYPE)
        win_s[0:NSA_WINDOW, :] = sw_ref[0].astype(win_s.dtype)
        win_s[NSA_WINDOW:NSA_WINDOW + 2 * SAMPLE_T, :] = jnp.concatenate(
            [neww_ref[0], jnp.zeros((SAMPLE_T, 128), F32)], axis=0).astype(win_s.dtype)
        w = NSA_WINDOW + 2 * SAMPLE_T
        w_pos = past - NSA_WINDOW + _iota((1, w), 1)
        q_pos = past + _iota((SAMPLE_T, 1), 0)
        n_chunks = (past + SAMPLE_T + tkc - 1) // tkc
        o_ref[0] = _nsa_attend(qa_ref[0], misc_ref[0], q_pos, cmp, nblk, key_s, n_chunks, tkc, win_s[...], w_pos,
                               pqc_ref[...], pqs_ref[...], pqw_ref[...])


def _nsa_sample(page_table, cache, layer, qa, misc, newn, state_win, neww, wpos, ctab, swc, pqc, pqs, pqw,
                pps, tkc):
    db, n_pages = page_table.shape
    past = n_pages * PAGE
    nsteps = n_pages // pps + 1
    last = n_pages - 1
    nbp = ctab.shape[0]
    lpad = ((past + SAMPLE_T + tkc - 1) // tkc) * tkc

    def page_spec(p):
        return pl.BlockSpec((None, None, PAGE, 256),
                            lambda b, j, pt: (layer, pt[b, jnp.minimum(j * pps + p, last)], 0, 0))

    full = lambda a: pl.BlockSpec(a.shape, lambda b, j, pt: (0, 0))
    per_b = lambda w: pl.BlockSpec((1, SAMPLE_T, w), lambda b, j, pt: (b, 0, 0))
    kern = functools.partial(_nsa_sample_kernel, pps=pps, past=past, tkc=tkc)
    return pl.pallas_call(
        kern,
        grid_spec=pltpu.PrefetchScalarGridSpec(
            num_scalar_prefetch=1,
            grid=(db, nsteps),
            in_specs=[page_spec(p) for p in range(pps)] + [
                per_b(256), per_b(128), per_b(256),
                pl.BlockSpec((None, 1, NSA_WINDOW, 128), lambda b, j, pt: (layer, b, 0, 0)),
                per_b(128), full(wpos), full(ctab), full(swc), full(pqc), full(pqs), full(pqw)],
            out_specs=per_b(512),
            scratch_shapes=[pltpu.VMEM((lpad, 256), MXU_DTYPE), pltpu.VMEM((nbp, 256), F32),
                            pltpu.VMEM((NSA_WINDOW + 2 * SAMPLE_T, 128), MXU_DTYPE)],
        ),
        out_shape=jax.ShapeDtypeStruct((db, SAMPLE_T, 512), F32),
        compiler_params=_cparams(("parallel", "arbitrary"), 48),
        name="nsa_sample",
    )(page_table, *([cache] * pps), qa, misc, newn, state_win, neww, wpos, ctab, swc, pqc, pqs, pqw)


def _dsa_attend(qc, qi, misc, q_pos, key_ref, n_chunks, tkc, n_top, pqd, pqi, tri, key_s):
    tq = qc.shape[0]
    h = DSA_HEADS
    qi_st = _stack_heads(_mm(qi, pqi), IDX_HEADS, 256)
    wcol = [misc[:, 12 + g:13 + g] for g in range(IDX_HEADS)]

    def score_chunk(c, _):
        k0 = pl.multiple_of(c * tkc, tkc)
        keys = key_ref[pl.ds(k0, tkc), :]
        rel = jnp.maximum(_mm_nt(qi_st, keys), 0.0)
        sc = wcol[0] * rel[0:tq]
        for g in range(1, IDX_HEADS):
            sc = sc + wcol[g] * rel[g * tq:(g + 1) * tq]
        sc = jnp.where(sc == 0.0, 0.0, sc)
        sc = jnp.where((k0 + _iota((1, tkc), 1)) <= q_pos, sc, NEG)
        key_s[c] = _sortable(sc)
        return 0

    lax.fori_loop(0, n_chunks, score_chunk, 0)

    def count(pred_fn):
        def body(c, acc):
            hit = jnp.where(pred_fn(key_s[c]), 1.0, 0.0)
            for u in range(tkc // LANE):
                acc = acc + hit[:, u * LANE:(u + 1) * LANE]
            return acc
        acc = lax.fori_loop(0, n_chunks, body, jnp.zeros((tq, LANE), F32))
        return jnp.sum(acc, axis=-1, keepdims=True)

    def bit_step(it, thr):
        cand = thr + jnp.left_shift(jnp.int32(1), jnp.int32(31) - it)
        cnt = count(lambda k: k >= cand)
        return jnp.where(cnt >= float(n_top), cand, thr)

    thr = lax.fori_loop(0, 32, bit_step, jnp.full((tq, 1), -2 ** 31, I32))
    need = float(n_top) - count(lambda k: k > thr)

    qd_st = _stack_heads(_mm(qc, pqd), h, 256)
    scale = D_HEAD ** -0.5

    def chunk(c, carry):
        eq_seen, fl = carry
        k0 = pl.multiple_of(c * tkc, tkc)
        keys = key_ref[pl.ds(k0, tkc), :]
        kk = key_s[c]
        eq = jnp.where(kk == thr, 1.0, 0.0)
        rank = eq_seen + _mm(eq, tri)
        sel = (kk > thr) | ((kk == thr) & (rank <= need))
        mask1 = sel & ((k0 + _iota((1, tkc), 1)) <= q_pos)
        mask = jnp.concatenate([mask1] * h, axis=0)
        s = _mm_nt(qd_st, keys) * scale
        return eq_seen + jnp.sum(eq, axis=-1, keepdims=True), _online_update(fl, s, mask, keys)

    init = (jnp.zeros((tq, 1), F32),
            (jnp.full((h * tq, 1), NEG, F32), jnp.zeros((h * tq, 1), F32), jnp.zeros((h * tq, 256), F32)))
    _, (_, l, acc) = lax.fori_loop(0, n_chunks, chunk, init)
    o = _finish(l, acc)[:, 0:128]
    o = jnp.where(_iota((1, 128), 1) >= D_HEAD, o, 0.0)
    return jnp.concatenate([o[hh * tq:(hh + 1) * tq] for hh in range(h)], axis=1)


def _dsa_prompt_kernel(qc_ref, qi_ref, misc_ref, dk_ref, pqd_ref, pqi_ref, tri_ref, o_ref, key_s,
                       *, tq, tkc, n_top):
    i = pl.program_id(1)
    s0 = i * tq
    q_pos = s0 + _iota((tq, 1), 0)
    n_chunks = (s0 + tq + tkc - 1) // tkc
    o_ref[...] = _dsa_attend(qc_ref[...], qi_ref[...], misc_ref[...], q_pos, dk_ref, n_chunks, tkc, n_top,
                             pqd_ref[...], pqi_ref[...], tri_ref[...], key_s)


def _dsa_prompt(qc, qi, misc, dk, pqd, pqi, tri, nb, t, tq, tkc):
    nq = t // tq
    n_top = min(DSA_TOPK_MAX, t // 4)
    kern = functools.partial(_dsa_prompt_kernel, tq=tq, tkc=tkc, n_top=n_top)
    full = lambda a: pl.BlockSpec(a.shape, lambda b, i: (0, 0))
    return pl.pallas_call(
        kern,
        grid=(nb, nq),
        in_specs=[pl.BlockSpec((tq, 256), lambda b, i: (b * nq + i, 0)),
                  pl.BlockSpec((tq, 256), lambda b, i: (b * nq + i, 0)),
                  pl.BlockSpec((tq, 128), lambda b, i: (b * nq + i, 0)),
                  pl.BlockSpec((t, 256), lambda b, i: (b, 0)),
                  full(pqd), full(pqi), full(tri)],
        out_specs=pl.BlockSpec((tq, 512), lambda b, i: (b * nq + i, 0)),
        out_shape=jax.ShapeDtypeStruct((nb * t, 512), F32),
        scratch_shapes=[pltpu.VMEM((t // tkc, tq, tkc), I32)],
        compiler_params=_cparams(("parallel", "arbitrary"), 48),
        name="dsa_prompt",
    )(qc, qi, misc, dk, pqd, pqi, tri)


def _dsa_sample_kernel(pt_ref, *refs, pps, past, tkc, n_top):
    pages = refs[:pps]
    (qc_ref, qi_ref, misc_ref, newd_ref, pqd_ref, pqi_ref, tri_ref, o_ref, key_rows, key_s) = refs[pps:]
    j = pl.program_id(1)
    nsteps = pl.num_programs(1)

    @pl.when(j == 0)
    def _():
        key_rows[...] = jnp.zeros(key_rows.shape, key_rows.dtype)

    @pl.when(j < nsteps - 1)
    def _():
        for p in range(pps):
            r0 = pl.multiple_of((j * pps + p) * PAGE, PAGE)
            key_rows[pl.ds(r0, PAGE), 0:160] = pages[p][...].astype(key_rows.dtype)

    @pl.when(j == nsteps - 1)
    def _():
        key_rows[past:past + 2 * SAMPLE_T, :] = jnp.concatenate(
            [newd_ref[0], jnp.zeros((SAMPLE_T, 256), newd_ref.dtype)], axis=0)
        q_pos = past + _iota((SAMPLE_T, 1), 0)
        n_chunks = (past + SAMPLE_T + tkc - 1) // tkc
        o_ref[0] = _dsa_attend(qc_ref[0], qi_ref[0], misc_ref[0], q_pos, key_rows, n_chunks, tkc, n_top,
                               pqd_ref[...], pqi_ref[...], tri_ref[...], key_s)


def _dsa_sample(page_table, cache, layer, qc, qi, misc, newd, pqd, pqi, tri, pps, tkc, ts):
    db, n_pages = page_table.shape
    past = n_pages * PAGE
    nsteps = n_pages // pps + 1
    last = n_pages - 1
    lpad = ((past + SAMPLE_T + tkc - 1) // tkc) * tkc
    n_top = min(DSA_TOPK_MAX, (past + ts) // 4)

    def page_spec(p):
        return pl.BlockSpec((None, None, PAGE, 160),
                            lambda b, j, pt: (layer, pt[b, jnp.minimum(j * pps + p, last)], 0, 0))

    full = lambda a: pl.BlockSpec(a.shape, lambda b, j, pt: (0, 0))
    per_b = lambda w: pl.BlockSpec((1, SAMPLE_T, w), lambda b, j, pt: (b, 0, 0))
    kern = functools.partial(_dsa_sample_kernel, pps=pps, past=past, tkc=tkc, n_top=n_top)
    return pl.pallas_call(
        kern,
        grid_spec=pltpu.PrefetchScalarGridSpec(
            num_scalar_prefetch=1,
            grid=(db, nsteps),
            in_specs=[page_spec(p) for p in range(pps)] + [
                per_b(256), per_b(256), per_b(128), per_b(256), full(pqd), full(pqi), full(tri)],
            out_specs=per_b(512),
            scratch_shapes=[pltpu.VMEM((lpad, 256), MXU_DTYPE), pltpu.VMEM((lpad // tkc, SAMPLE_T, tkc), I32)],
        ),
        out_shape=jax.ShapeDtypeStruct((db, SAMPLE_T, 512), F32),
        compiler_params=_cparams(("parallel", "arbitrary"), 48),
        name="dsa_sample",
    )(page_table, *([cache] * pps), qc, qi, misc, newd, pqd, pqi, tri)


def _out_kernel(h_ref, oa_ref, om_ref, od_ref, wa_ref, wm_ref, wd_ref, g_ref, h1_o, xn_o):
    h1 = h_ref[...] + (_mm(oa_ref[...], wa_ref[...]) + _mm(om_ref[...], wm_ref[...]) + _mm(od_ref[...], wd_ref[...]))
    h1_o[...] = h1
    xn = h1 * lax.rsqrt(jnp.mean(h1 * h1, axis=-1, keepdims=True) + EPS) * g_ref[...]
    xn_o[...] = xn.astype(xn_o.dtype)


def _out_proj(h, oa, om, od, wa, wm, wd, g, tm):
    n = h.shape[0]
    row = lambda w: pl.BlockSpec((tm, w), lambda i: (i, 0))
    full = lambda a: pl.BlockSpec(a.shape, lambda i: (0, 0))
    return pl.pallas_call(
        _out_kernel,
        grid=(n // tm,),
        in_specs=[row(D_MODEL), row(512), row(512), row(512), full(wa), full(wm), full(wd), full(g)],
        out_specs=[row(D_MODEL), row(D_MODEL)],
        out_shape=[jax.ShapeDtypeStruct((n, D_MODEL), F32), jax.ShapeDtypeStruct((n, D_MODEL), MXU_DTYPE)],
        compiler_params=_cparams(("parallel",), 40),
        name="out_proj",
    )(h, oa, om, od, wa, wm, wd, g)


def _ffn_kernel(*refs, tm, seq, sample):
    if sample:
        (xn_ref, h1_ref, p_ref, wu_ref, wg_ref, wd_ref, cw_ref, cb_ref, gp_ref, wpg_ref, wpp_ref,
         p1_ref, p2_ref, h_o, a_o, acc_s) = refs
    else:
        (xn_ref, halo_ref, h1_ref, p_ref, wu_ref, wg_ref, wd_ref, cw_ref, cb_ref, gp_ref, wpg_ref, wpp_ref,
         h_o, a_o, acc_s) = refs
    i = pl.program_id(0)
    j = pl.program_id(1)
    xn = xn_ref[...]
    a = _mm(xn, wu_ref[...])
    b = _mm(xn, wg_ref[...])
    a_o[...] = a
    row = _iota((tm, 1), 0)
    r1 = pltpu.roll(a, 1, 0)
    r2 = pltpu.roll(a, 2, 0)
    if sample:
        tpos = row % SAMPLE_T
        prev1 = jnp.where(tpos == 0, p1_ref[...], r1)
        prev2 = jnp.where(tpos < 2, p2_ref[...], r2)
    else:
        ah = _mm(halo_ref[...], wu_ref[...])
        ah = jnp.where((i * tm) % seq == 0, 0.0, ah)
        prev1 = jnp.where(row == 0, ah[7:8, :], r1)
        prev2 = jnp.where(row == 0, ah[6:7, :], jnp.where(row == 1, ah[7:8, :], r2))
    c = cb_ref[...] + cw_ref[2:3, :] * a
    c = c + cw_ref[0:1, :] * prev2
    c = c + cw_ref[1:2, :] * prev1
    u = (c * jax.nn.sigmoid(c)) * b
    y = _mm(u, wd_ref[...])

    @pl.when(j == 0)
    def _():
        acc_s[...] = y

    @pl.when(j > 0)
    def _():
        acc_s[...] = acc_s[...] + y

    @pl.when(j == pl.num_programs(1) - 1)
    def _():
        h2 = h1_ref[...] + acc_s[...]
        xn3 = h2 * lax.rsqrt(jnp.mean(h2 * h2, axis=-1, keepdims=True) + EPS) * gp_ref[...]
        gate = jax.nn.sigmoid(_mm(xn3, wpg_ref[...]))
        h_o[...] = h2 + gate * _mm(p_ref[...], wpp_ref[...])


def _ffn(xn, h1, p, wu, wg, wd, cw, cb, gp, wpg, wpp, tm, tf, seq, prefix=None):
    n = xn.shape[0]
    nf = D_FF // tf
    sample = prefix is not None
    row = lambda w: pl.BlockSpec((tm, w), lambda i, j: (i, 0))
    full = lambda a: pl.BlockSpec(a.shape, lambda i, j: (0, 0))
    ff_col = lambda r: pl.BlockSpec((r, tf), lambda i, j: (0, j))
    in_specs = [row(D_MODEL)]
    args = [xn]
    if not sample:
        in_specs.append(pl.BlockSpec((SUB, D_MODEL), lambda i, j: (jnp.maximum(i * (tm // SUB) - 1, 0), 0)))
        args.append(xn)
    in_specs += [row(D_MODEL), row(PLE_DIM), ff_col(D_MODEL), ff_col(D_MODEL),
                 pl.BlockSpec((tf, D_MODEL), lambda i, j: (j, 0)), ff_col(CONV_W), ff_col(1),
                 full(gp), full(wpg), full(wpp)]
    args += [h1, p, wu, wg, wd, cw, cb, gp, wpg, wpp]
    if sample:
        in_specs += [pl.BlockSpec((tm, tf), lambda i, j: (i, j))] * 2
        args += list(prefix)
    kern = functools.partial(_ffn_kernel, tm=tm, seq=seq, sample=sample)
    return pl.pallas_call(
        kern,
        grid=(n // tm, nf),
        in_specs=in_specs,
        out_specs=[row(D_MODEL), pl.BlockSpec((tm, tf), lambda i, j: (i, j))],
        out_shape=[jax.ShapeDtypeStruct((n, D_MODEL), F32), jax.ShapeDtypeStruct((n, D_FF), F32)],
        scratch_shapes=[pltpu.VMEM((tm, D_MODEL), F32)],
        compiler_params=_cparams(("parallel", "arbitrary"), 56),
        name="ffn",
    )(*args)


def _pick(n, prefs):
    for p in prefs:
        if n % p == 0:
            return p
    return n


def kernel(x_prompt, x_sample, cache_nsa, cache_mla, cache_dsa, state_win, state_conv, page_table, p_prompt, p_sample, norm_mix, w_in, nsa_qn, nsa_kn, nsa_cmp_pos, mla_cqn, mla_ckvn, mla_w_uq, mla_w_ukv, mla_qn, mla_kn, dsa_qn, dsa_kn, w_out, norm_ffn, ffn_w_up, ffn_w_gate, ffn_conv_w, ffn_conv_b, ffn_w_down, norm_ple, ple_w_gate, ple_w_proj):
    depth = w_in.shape[0]
    B, T, _ = x_prompt.shape
    DB, TS, _ = x_sample.shape
    n_pages = page_table.shape[1]
    past = n_pages * PAGE
    assert TS <= 4 and T % 128 == 0 and T >= NSA_WINDOW + 128 and past >= NSA_WINDOW
    mm = MXU_DTYPE
    bf = lambda a: jnp.asarray(a, jnp.bfloat16)

    win_cols, wuq_cols = _win_cols(), _wuq_cols()
    g256 = bf(_blockdiag(256, 64))
    g1024 = bf(_blockdiag(1024, 128))
    g8 = bf((np.arange(1024)[:, None] // 128 == np.arange(128)[None, :]).astype(np.float32))
    pk = bf(_placement(128, 1024, [(j, h * 128 + 64 + j) for h in range(8) for j in range(32)]))
    swc = bf(_placement(256, 256, [(j, (j + 32) % 64) for j in range(64)]))
    pq_c = bf(_q_placement(4, 64, 256, 0))
    pq_s = bf(_q_placement(4, 64, 256, 128))
    pq_w = bf(_q_placement(4, 64, 128, 0))
    pq_d = bf(_q_placement(4, 64, 256, 0))
    pq_i = bf(_q_placement(8, 32, 256, 128))
    tkc = _pick(T, (512, 256, 128))
    tri = bf(np.triu(np.ones((tkc, tkc), np.float32)))
    tab_p = _rope_tables(jnp.arange(T))
    pos_s = past + (jnp.arange(DB * SAMPLE_T) % SAMPLE_T)
    tab_s = _rope_tables(pos_s)
    nblk_p = T // NSA_BLK
    ctab_p = _cmp_tables(nblk_p)
    nbp_s = ((past // NSA_BLK + 1 + LANE - 1) // LANE) * LANE
    ctab_s = _cmp_tables(nbp_s)
    exp_rows = np.full((512,), -1, np.int64)
    for h in range(4):
        exp_rows[h * 128 + 64: (h + 1) * 128] = h * 64 + np.arange(64)

    tm_p = _pick(B * T, (256, 128))
    tq = 128
    tq_m = _pick(T, (256, 128))
    tm_o = _pick(B * T, (512, 256, 128))
    tf = D_FF // 2
    ns = DB * SAMPLE_T

    hp = x_prompt.reshape(B * T, D_MODEL)
    hs = jnp.pad(x_sample, ((0, 0), (0, SAMPLE_T - TS), (0, 0))).reshape(ns, D_MODEL)
    outs = {k: [] for k in ("nsa_p", "nsa_s", "mla_p", "mla_s", "dsa_p", "dsa_s", "win_p", "win_s", "conv_p", "conv_s")}
    unpad = lambda a: a.reshape(DB, SAMPLE_T, -1)[:, :TS]

    for i in range(depth):
        wbig = _take_cols(w_in[i], win_cols).astype(mm)
        wuq = _take_cols(mla_w_uq[i], wuq_cols).astype(mm)
        gv = _gain_vector(norm_mix[i], nsa_qn[i], nsa_kn[i], mla_cqn[i], mla_ckvn[i], dsa_qn[i], dsa_kn[i], mla_qn[i])
        ukv = mla_w_ukv[i].reshape(MLA_KV_RANK, MLA_HEADS, MLA_NOPE + MLA_V)
        wk = jnp.pad(ukv[:, :, :MLA_NOPE], ((0, 0), (0, 0), (0, LANE - MLA_NOPE))).reshape(MLA_KV_RANK, 1024).astype(mm)
        wv = ukv[:, :, MLA_NOPE:].reshape(MLA_KV_RANK, 512).astype(mm)
        gk = jnp.tile(jnp.concatenate([mla_kn[i], jnp.zeros((32,), F32)]), 8)[None, :]
        wpos = jnp.concatenate([nsa_cmp_pos[i, 0], nsa_cmp_pos[i, 1], jnp.zeros((NSA_BLK, 128), F32)], axis=1)
        wo = w_out[i]
        expand = lambda w: jnp.where(jnp.asarray(exp_rows >= 0)[:, None],
                                     jnp.take(w, jnp.asarray(np.maximum(exp_rows, 0)), axis=0), 0.0).astype(mm)
        wo_a, wo_m, wo_d = expand(wo[0:256]), wo[256:768].astype(mm), expand(wo[768:1024])
        ffn_w = (ffn_w_up[i].astype(mm), ffn_w_gate[i].astype(mm), ffn_w_down[i].astype(mm), ffn_conv_w[i],
                 ffn_conv_b[i][None, :], norm_ple[i][None, :], ple_w_gate[i].astype(mm), ple_w_proj[i].astype(mm))
        g_ffn = norm_ffn[i][None, :]

        (qa, qm, qc, qi, misc, rn, rm, rd, rw, nk, wkk, dk, kp) = _proj(
            hp, tab_p, T // tm_p, gv, wbig, wuq, g256, g1024, tm_p)
        km, vm = _mla_prep(rm, kp, wk, wv, pk, g8, gk, _pick(B * T, (512, 256, 128)))
        o_m = _mla_flash(qm, km, vm, B, T, tq_m, tq_m)
        cmp = _nsa_cmp(rn, wpos, ctab_p, swc, B, T)
        o_a = _nsa_prompt(qa, misc, cmp, nk, wkk, pq_c, pq_s, pq_w, B, T, tq, tkc)
        o_d = _dsa_prompt(qc, qi, misc, dk, pq_d, pq_i, tri, B, T, tq, tkc)
        h1, xn2 = _out_proj(hp, o_a, o_m, o_d, wo_a, wo_m, wo_d, g_ffn, tm_o)
        hp, a_p = _ffn(xn2, h1, p_prompt[i].reshape(B * T, PLE_DIM), *ffn_w, tm_o, tf, T)
        outs["nsa_p"].append(rn.reshape(B, T, -1))
        outs["mla_p"].append(rm.reshape(B, T, -1))
        outs["dsa_p"].append(rd.reshape(B, T, -1))
        outs["win_p"].append(rw.reshape(B, T, -1)[:, T - min(NSA_WINDOW, T):])
        outs["conv_p"].append(a_p.reshape(B, T, D_FF)[:, T - (CONV_W - 1):])

        (qa, qm, qc, qi, misc, rn, rm, rd, rw, nk, wkk, dk, kp) = _proj(
            hs, tab_s, 1, gv, wbig, wuq, g256, g1024, _pick(ns, (256, 128)))
        r3 = lambda a: a.reshape(DB, SAMPLE_T, -1)
        pps = _pick(n_pages, (8, 4))
        o_m = _mla_sample(page_table, cache_mla, i, r3(qm), r3(rm), r3(kp), wk, wv, pk, g8, gk, pps)
        o_a = _nsa_sample(page_table, cache_nsa, i, r3(qa), r3(misc), r3(rn), state_win, r3(rw), wpos, ctab_s, swc,
                          pq_c, pq_s, pq_w, pps, tkc)
        o_d = _dsa_sample(page_table, cache_dsa, i, r3(qc), r3(qi), r3(misc), r3(dk), pq_d, pq_i, tri, pps, tkc, TS)
        tm_s = _pick(ns, (256, 128))
        h1, xn2 = _out_proj(hs, o_a.reshape(ns, 512), o_m.reshape(ns, 512), o_d.reshape(ns, 512),
                            wo_a, wo_m, wo_d, g_ffn, tm_s)
        sc = state_conv[i]
        zrow = jnp.zeros((DB, SAMPLE_T - 1, D_FF), F32)
        p1 = jnp.concatenate([sc[:, 1:2], zrow], axis=1).reshape(ns, D_FF)
        p2 = jnp.concatenate([sc[:, 0:2], zrow[:, 1:]], axis=1).reshape(ns, D_FF)
        p_s = jnp.pad(p_sample[i], ((0, 0), (0, SAMPLE_T - TS), (0, 0))).reshape(ns, PLE_DIM)
        hs, a_s = _ffn(xn2, h1, p_s, *ffn_w, tm_s, tf, SAMPLE_T, prefix=(p1, p2))
        outs["nsa_s"].append(unpad(rn))
        outs["mla_s"].append(unpad(rm))
        outs["dsa_s"].append(unpad(rd))
        win_all = jnp.concatenate([state_win[i], unpad(rw)], axis=1)
        outs["win_s"].append(win_all[:, TS:])
        a_ext = jnp.concatenate([sc, unpad(a_s)], axis=1)
        outs["conv_s"].append(a_ext[:, TS:])

    st = lambda k: jnp.stack(outs[k])
    return (hp.reshape(B, T, D_MODEL), unpad(hs),
            st("nsa_p"), st("nsa_s"), st("mla_p"), st("mla_s"), st("dsa_p"), st("dsa_s"),
            st("win_p"), st("win_s"), st("conv_p"), st("conv_s"))
```

```python
import functools
import math

import numpy as np
import jax
import jax.numpy as jnp
from jax import lax
from jax.experimental import pallas as pl
from jax.experimental.pallas import tpu as pltpu

F32 = jnp.float32
I32 = jnp.int32
MXU_DTYPE = jnp.bfloat16

D_MODEL = 1024
D_HEAD = 64
ROPE_THETA = 10000.0
EPS = 1e-6
NEG = -1e30
MASK_FLOOR = -1e29
NSA_HEADS = 4
NSA_BLK = 64
NSA_TOPN = 16
NSA_WINDOW = 512
NSA_FORCE = 1e9
MLA_HEADS = 8
MLA_Q_RANK = 256
MLA_KV_RANK = 256
MLA_NOPE = 64
MLA_ROPE = 32
MLA_V = 64
MLA_QK = MLA_NOPE + MLA_ROPE
DSA_HEADS = 4
IDX_HEADS = 8
IDX_DIM = 32
DSA_TOPK_MAX = 256
D_FF = 2816
CONV_W = 3
PLE_DIM = 256
PAGE = 128
LANE = 128
SUB = 8
SAMPLE_T = 8

C_QA, C_KC, C_VC, C_KS, C_VS, C_KW, C_VW, C_GA = 0, 256, 320, 384, 448, 512, 576, 640
C_CQ, C_CKV, C_KPE, C_QC, C_KD, C_VD, C_QI, C_KI, C_WI = 652, 908, 1164, 1196, 1452, 1516, 1580, 1836, 1868

(Z_QA, Z_QAS, Z_NSA, Z_NSAS, Z_WIN, Z_WINS, Z_MISC, Z_CQ, Z_CKV, Z_KPE, Z_KPES, Z_QC, Z_QCS,
 Z_DKV, Z_DKVS, Z_KI, Z_KIS, Z_QI, Z_QIS, Z_TOTAL) = (
    0, 256, 512, 768, 1024, 1152, 1280, 1408, 1664, 1920, 2048, 2176, 2432, 2688, 2816, 2944,
    3072, 3200, 3456, 3712)

(T_64C, T_64S, T_NSAC, T_NSAS, T_KVC, T_KVS, T_QC, T_QS, T_32C, T_32S, T_328C, T_328S, T_TOTAL) = (
    0, 256, 512, 768, 1024, 1152, 1280, 2304, 3328, 3456, 3584, 3840, 4096)

(GV_NMIX, GV_QA_A, GV_QA_B, GV_NSA_A, GV_NSA_B, GV_WIN_A, GV_WIN_B, GV_CQ, GV_CKV, GV_QC_A, GV_QC_B,
 GV_DKV_A, GV_DKV_B, GV_Q, GV_TOTAL) = (
    0, 1024, 1280, 1536, 1792, 2048, 2176, 2304, 2560, 2816, 3072, 3328, 3456, 3584, 4608)


def _swap_idx(base, d, n):
    l = np.arange(n * d)
    return base + (l // d) * d + ((l % d) + d // 2) % d


def _win_cols():
    cols = np.full((Z_TOTAL,), -1, np.int64)

    def put(off, idx, at=0):
        cols[off + at: off + at + len(idx)] = idx

    put(Z_QA, C_QA + np.arange(256))
    put(Z_QAS, _swap_idx(C_QA, 64, 4))
    put(Z_NSA, C_KC + np.arange(256))
    put(Z_NSAS, _swap_idx(C_KS, 64, 1), at=128)
    put(Z_WIN, C_KW + np.arange(128))
    put(Z_WINS, _swap_idx(C_KW, 64, 1))
    put(Z_MISC, C_GA + np.arange(12))
    put(Z_MISC, C_WI + np.arange(8), at=12)
    put(Z_CQ, C_CQ + np.arange(256))
    put(Z_CKV, C_CKV + np.arange(256))
    put(Z_KPE, C_KPE + np.arange(32))
    put(Z_KPES, _swap_idx(C_KPE, 32, 1))
    put(Z_QC, C_QC + np.arange(256))
    put(Z_QCS, _swap_idx(C_QC, 64, 4))
    put(Z_DKV, C_KD + np.arange(128))
    put(Z_DKVS, _swap_idx(C_KD, 64, 1))
    put(Z_KI, C_KI + np.arange(32))
    put(Z_KIS, _swap_idx(C_KI, 32, 1))
    put(Z_QI, C_QI + np.arange(256))
    put(Z_QIS, _swap_idx(C_QI, 32, 8))
    return cols


def _wuq_cols():
    cols = np.full((2 * MLA_HEADS * LANE,), -1, np.int64)
    for h in range(MLA_HEADS):
        cols[h * LANE: h * LANE + MLA_QK] = h * MLA_QK + np.arange(MLA_QK)
        cols[1024 + h * LANE + MLA_NOPE: 1024 + h * LANE + MLA_QK] = _swap_idx(h * MLA_QK + MLA_NOPE, MLA_ROPE, 1)
    return cols


def _take_cols(w, cols):
    g = jnp.take(w, jnp.asarray(np.maximum(cols, 0)), axis=1)
    return jnp.where(jnp.asarray(cols >= 0)[None, :], g, 0.0)


def _blockdiag(n, d):
    i = np.arange(n)
    return (i[:, None] // d == i[None, :] // d).astype(np.float32)


def _placement(n_in, n_out, pairs):
    m = np.zeros((n_in, n_out), np.float32)
    for s, d in pairs:
        m[s, d] = 1.0
    return m


def _q_placement(n_heads, d, width, at):
    pairs = [(h * d + j, h * width + at + j) for h in range(n_heads) for j in range(d)]
    return _placement(n_heads * d, n_heads * width, pairs)


def _rope_tables(pos):
    pos = pos.astype(F32)[:, None]
    P = pos.shape[0]

    def cs(d):
        half = d // 2
        inv = ROPE_THETA ** (-jnp.arange(half, dtype=F32) / half)
        ang = pos * inv[None, :]
        c, s = jnp.cos(ang), jnp.sin(ang)
        return jnp.concatenate([c, c], axis=1), jnp.concatenate([-s, s], axis=1)

    c64, s64 = cs(64)
    c32, s32 = cs(32)
    one = lambda n: jnp.ones((P, n), F32)
    zero = lambda n: jnp.zeros((P, n), F32)
    segs = [
        jnp.tile(c64, (1, 4)), jnp.tile(s64, (1, 4)),
        jnp.concatenate([one(128), c64, one(64)], 1), jnp.concatenate([zero(128), s64, zero(64)], 1),
        jnp.concatenate([c64, one(64)], 1), jnp.concatenate([s64, zero(64)], 1),
        jnp.tile(jnp.concatenate([one(64), c32, one(32)], 1), (1, 8)),
        jnp.tile(jnp.concatenate([zero(64), s32, zero(32)], 1), (1, 8)),
        jnp.concatenate([c32, one(96)], 1), jnp.concatenate([s32, zero(96)], 1),
        jnp.tile(c32, (1, 8)), jnp.tile(s32, (1, 8)),
    ]
    return jnp.concatenate(segs, axis=1)


def _cmp_tables(nbp):
    pos = (jnp.arange(nbp) * NSA_BLK + (NSA_BLK - 1)).astype(F32)[:, None]
    inv = ROPE_THETA ** (-jnp.arange(32, dtype=F32) / 32)
    ang = pos * inv[None, :]
    c, s = jnp.cos(ang), jnp.sin(ang)
    one = jnp.ones((nbp, 192), F32)
    zero = jnp.zeros((nbp, 192), F32)
    return jnp.concatenate([c, c, one, -s, s, zero], axis=1)


def _swap64(g):
    return jnp.concatenate([g[32:], g[:32]])


def _gain_vector(norm_mix, nsa_qn, nsa_kn, mla_cqn, mla_ckvn, dsa_qn, dsa_kn, mla_qn):
    o64, z64 = jnp.ones((64,), F32), jnp.zeros((64,), F32)
    qpad = jnp.concatenate([mla_qn, jnp.zeros((32,), F32)])
    segs = [
        norm_mix,
        jnp.tile(nsa_qn, 4), jnp.tile(_swap64(nsa_qn), 4),
        jnp.concatenate([nsa_kn[0], o64, nsa_kn[1], o64]), jnp.concatenate([z64, z64, _swap64(nsa_kn[1]), z64]),
        jnp.concatenate([nsa_kn[2], o64]), jnp.concatenate([_swap64(nsa_kn[2]), z64]),
        mla_cqn, mla_ckvn,
        jnp.tile(dsa_qn, 4), jnp.tile(_swap64(dsa_qn), 4),
        jnp.concatenate([dsa_kn, o64]), jnp.concatenate([_swap64(dsa_kn), z64]),
        jnp.tile(qpad, 8),
    ]
    return jnp.concatenate(segs)[None, :]


def _mm(a, b):
    return jnp.dot(a.astype(MXU_DTYPE), b.astype(MXU_DTYPE), preferred_element_type=F32)


def _mm_nt(a, b):
    return lax.dot_general(a.astype(MXU_DTYPE), b.astype(MXU_DTYPE), (((1,), (1,)), ((), ())),
                           preferred_element_type=F32)


def _split3(x):
    x1 = x.astype(jnp.bfloat16)
    r1 = x - x1.astype(F32)
    x2 = r1.astype(jnp.bfloat16)
    x3 = (r1 - x2.astype(F32)).astype(jnp.bfloat16)
    return x1, x2, x3


def _place(x, p):
    x1, x2, x3 = _split3(x)
    d = lambda a: jnp.dot(a, p, preferred_element_type=F32)
    return d(x1) + d(x2) + d(x3)


def _gsum(x2, g):
    hi = x2.astype(jnp.bfloat16)
    lo = (x2 - hi.astype(F32)).astype(jnp.bfloat16)
    return jnp.dot(hi, g, preferred_element_type=F32) + jnp.dot(lo, g, preferred_element_type=F32)


def _iota(shape, dim):
    return lax.broadcasted_iota(I32, shape, dim)


def _stack_heads(x, n, w):
    return jnp.concatenate([x[:, h * w:(h + 1) * w] for h in range(n)], axis=0)


def _softmax_rows(s, mask):
    s = jnp.where(mask, s, NEG)
    m = jnp.max(s, axis=-1, keepdims=True)
    e = jnp.where(mask, jnp.exp(s - m), 0.0)
    l = jnp.sum(e, axis=-1, keepdims=True)
    return e / jnp.where(l > 0.0, l, 1.0)


def _online_update(carry, s, mask, v, vt=False):
    m, l, acc = carry
    if mask is not None:
        s = jnp.where(mask, s, NEG)
    m_new = jnp.maximum(m, jnp.max(s, axis=-1, keepdims=True))
    alpha = jnp.exp(m - m_new)
    p = jnp.exp(s - m_new)
    if mask is not None:
        p = jnp.where(mask, p, 0.0)
    l = alpha * l + jnp.sum(p, axis=-1, keepdims=True)
    acc = alpha * acc + (_mm_nt(p, v) if vt else _mm(p, v))
    return m_new, l, acc


def _flash_init(rows, width):
    return (jnp.full((rows, 1), MASK_FLOOR, F32), jnp.zeros((rows, 1), F32), jnp.zeros((rows, width), F32))


def _finish(l, acc):
    return acc / jnp.where(l > 0.0, l, 1.0)


def _sortable(x):
    b = lax.bitcast_convert_type(x, I32)
    return jnp.where(b < 0, b ^ jnp.int32(0x7FFFFFFF), b)


def _cparams(sem, vmem_mb):
    return pltpu.CompilerParams(dimension_semantics=sem, vmem_limit_bytes=vmem_mb * 1024 * 1024)


def _proj_kernel(x_ref, tab_ref, gv_ref, w_ref, wuq_ref, g256_ref, g1024_ref,
                 qa_o, qm_o, qc_o, qi_o, misc_o, rn_o, rm_o, rd_o, rw_o, nk_o, wk_o, dk_o, kp_o):
    gv = lambda off, w: gv_ref[:, off:off + w]
    tab = lambda off, w: tab_ref[:, off:off + w]
    x = x_ref[...]
    xn = x * lax.rsqrt(jnp.mean(x * x, axis=-1, keepdims=True) + EPS) * gv(GV_NMIX, D_MODEL)
    z = _mm(xn, w_ref[...])
    zs = lambda off, w: z[:, off:off + w]
    g256 = g256_ref[...]
    g128 = g256_ref[0:128, 0:128]

    def group(zo, zso, w, gmat, d, ga, gb, tc, ts, normmask=None):
        a, asw = zs(zo, w), zs(zso, w)
        if gmat is None:
            rs = None
        else:
            rs = lax.rsqrt(_gsum(a * a, gmat) / d + EPS)
            if normmask is not None:
                rs = jnp.where(normmask, rs, 1.0)
        ca, cb = tab(tc, w), tab(ts, w)
        if ga is not None:
            ca, cb = gv(ga, w) * ca, gv(gb, w) * cb
        if rs is not None:
            ca, cb = rs * ca, rs * cb
        return a * ca + asw * cb

    qa_o[...] = group(Z_QA, Z_QAS, 256, g256, 64.0, GV_QA_A, GV_QA_B, T_64C, T_64S).astype(qa_o.dtype)
    lane = _iota((1, 256), 1)
    nm = (lane < 64) | ((lane >= 128) & (lane < 192))
    rn = group(Z_NSA, Z_NSAS, 256, g256, 64.0, GV_NSA_A, GV_NSA_B, T_NSAC, T_NSAS, nm)
    rn_o[...] = rn
    nk_o[...] = rn.astype(nk_o.dtype)
    nm128 = _iota((1, 128), 1) < 64
    rw = group(Z_WIN, Z_WINS, 128, g128, 64.0, GV_WIN_A, GV_WIN_B, T_KVC, T_KVS, nm128)
    rw_o[...] = rw
    wk_o[...] = rw.astype(wk_o.dtype)
    zm = zs(Z_MISC, 128)
    misc_o[...] = jnp.where(_iota((1, 128), 1) < 12, jax.nn.sigmoid(zm), zm * (IDX_HEADS ** -0.5))
    cq = zs(Z_CQ, 256)
    cqn = cq * lax.rsqrt(jnp.mean(cq * cq, axis=-1, keepdims=True) + EPS) * gv(GV_CQ, 256)
    q2 = _mm(cqn, wuq_ref[...])
    qr = q2[:, 0:1024] * tab(T_QC, 1024) + q2[:, 1024:2048] * tab(T_QS, 1024)
    rs = lax.rsqrt(_gsum(qr * qr, g1024_ref[...]) / float(MLA_QK) + EPS)
    qm_o[...] = (qr * rs * gv(GV_Q, 1024)).astype(qm_o.dtype)
    ckv = zs(Z_CKV, 256)
    rm_o[:, 0:256] = ckv * lax.rsqrt(jnp.mean(ckv * ckv, axis=-1, keepdims=True) + EPS) * gv(GV_CKV, 256)
    kp = group(Z_KPE, Z_KPES, 128, None, 0.0, None, None, T_32C, T_32S)
    rm_o[:, 256:288] = kp[:, 0:32]
    kp_o[...] = kp
    qc_o[...] = group(Z_QC, Z_QCS, 256, g256, 64.0, GV_QC_A, GV_QC_B, T_64C, T_64S).astype(qc_o.dtype)
    dkv = group(Z_DKV, Z_DKVS, 128, g128, 64.0, GV_DKV_A, GV_DKV_B, T_KVC, T_KVS, nm128)
    ki = group(Z_KI, Z_KIS, 128, None, 0.0, None, None, T_32C, T_32S)
    rd_o[:, 0:128] = dkv
    rd_o[:, 128:160] = ki[:, 0:32]
    dk_o[:, 0:128] = dkv.astype(dk_o.dtype)
    dk_o[:, 128:256] = ki.astype(dk_o.dtype)
    qi_o[...] = group(Z_QI, Z_QIS, 256, None, 0.0, None, None, T_328C, T_328S).astype(qi_o.dtype)


def _proj(x, tab, n_pos_tiles, gv, wbig, wuq, g256, g1024, tm):
    n = x.shape[0]
    row = lambda w: pl.BlockSpec((tm, w), lambda i: (i, 0))
    full = lambda a: pl.BlockSpec(a.shape, lambda i: (0, 0))
    widths = [(256, MXU_DTYPE), (1024, MXU_DTYPE), (256, MXU_DTYPE), (256, MXU_DTYPE), (128, F32),
              (256, F32), (288, F32), (160, F32), (128, F32),
              (256, MXU_DTYPE), (128, MXU_DTYPE), (256, MXU_DTYPE), (128, F32)]
    return pl.pallas_call(
        _proj_kernel,
        grid=(n // tm,),
        in_specs=[row(D_MODEL),
                  pl.BlockSpec((tm, T_TOTAL), lambda i: (i % n_pos_tiles, 0)),
                  full(gv), full(wbig), full(wuq), full(g256), full(g1024)],
        out_specs=[row(w) for w, _ in widths],
        out_shape=[jax.ShapeDtypeStruct((n, w), dt) for w, dt in widths],
        compiler_params=_cparams(("parallel",), 56),
        name="proj",
    )(x, tab, gv, wbig, wuq, g256, g1024)


def _mla_keys(ckv, kpe128, wk, wv, pk, g8, gk):
    kraw = _mm(ckv, wk) + _place(kpe128, pk)
    ss = _gsum(kraw * kraw, g8)
    rs = lax.rsqrt(ss / float(MLA_QK) + EPS)
    parts = [kraw[:, h * LANE:(h + 1) * LANE] * rs[:, h:h + 1] for h in range(MLA_HEADS)]
    km = jnp.concatenate(parts, axis=1) * gk
    return km.astype(MXU_DTYPE), _mm(ckv, wv).astype(MXU_DTYPE)


def _mla_prep_kernel(rm_ref, kp_ref, wk_ref, wv_ref, pk_ref, g8_ref, gk_ref, km_o, vm_o):
    km, vm = _mla_keys(rm_ref[:, 0:256], kp_ref[...], wk_ref[...], wv_ref[...], pk_ref[...],
                       g8_ref[...], gk_ref[...])
    km_o[...] = km
    vm_o[...] = vm


def _mla_prep(rm, kp, wk, wv, pk, g8, gk, tk):
    n = rm.shape[0]
    full = lambda a: pl.BlockSpec(a.shape, lambda i: (0, 0))
    return pl.pallas_call(
        _mla_prep_kernel,
        grid=(n // tk,),
        in_specs=[pl.BlockSpec((tk, 288), lambda i: (i, 0)), pl.BlockSpec((tk, 128), lambda i: (i, 0)),
                  full(wk), full(wv), full(pk), full(g8), full(gk)],
        out_specs=[pl.BlockSpec((tk, 1024), lambda i: (i, 0)), pl.BlockSpec((tk, 512), lambda i: (i, 0))],
        out_shape=[jax.ShapeDtypeStruct((n, 1024), MXU_DTYPE), jax.ShapeDtypeStruct((n, 512), MXU_DTYPE)],
        compiler_params=_cparams(("parallel",), 40),
        name="mla_prep",
    )(rm, kp, wk, wv, pk, g8, gk)


def _mla_flash_kernel(q_ref, k_ref, v_ref, o_ref, *, tq, tk):
    i = pl.program_id(2)
    scale = MLA_QK ** -0.5
    q_pos = i * tq + _iota((tq, 1), 0)
    qs = [q_ref[:, hh * LANE:(hh + 1) * LANE] for hh in range(2)]

    def chunk(c, carry, masked):
        k0 = pl.multiple_of(c * tk, tk)
        v = v_ref[pl.ds(k0, tk), :]
        mask = ((k0 + _iota((1, tk), 1)) <= q_pos) if masked else None
        out = []
        for hh in range(2):
            k = k_ref[pl.ds(k0, tk), hh * LANE:(hh + 1) * LANE]
            out.append(_online_update(carry[hh], _mm_nt(qs[hh], k) * scale, mask, v))
        return tuple(out)

    one = (jnp.full((tq, 1), NEG, F32), jnp.zeros((tq, 1), F32), jnp.zeros((tq, LANE), F32))
    n_full = (i * tq) // tk
    n_all = ((i + 1) * tq + tk - 1) // tk
    carry = lax.fori_loop(0, n_full, lambda c, cr: chunk(c, cr, False), (one, one))
    carry = lax.fori_loop(n_full, n_all, lambda c, cr: chunk(c, cr, True), carry)
    outs = [_finish(cr[1], cr[2]) for cr in carry]
    lane = _iota((1, LANE), 1)
    o_ref[...] = jnp.where(lane < MLA_V, outs[0], outs[1])


def _mla_flash(qm, km, vm, nb, t, tq, tk):
    nq = t // tq
    kern = functools.partial(_mla_flash_kernel, tq=tq, tk=tk)
    return pl.pallas_call(
        kern,
        grid=(nb, MLA_HEADS // 2, nq),
        in_specs=[pl.BlockSpec((tq, 256), lambda b, h, i: (b * nq + i, h)),
                  pl.BlockSpec((t, 256), lambda b, h, i: (b, h)),
                  pl.BlockSpec((t, 128), lambda b, h, i: (b, h))],
        out_specs=pl.BlockSpec((tq, 128), lambda b, h, i: (b * nq + i, h)),
        out_shape=jax.ShapeDtypeStruct((nb * t, 512), F32),
        compiler_params=_cparams(("parallel", "parallel", "arbitrary"), 48),
        name="mla_flash",
    )(qm, km, vm)


def _mla_sample_kernel(pt_ref, *refs, pps, past):
    pages = refs[:pps]
    (q_ref, newm_ref, newkp_ref, wkvt_ref, gkc_ref, wk_ref, wv_ref, pk_ref, g8_ref, gk_ref, o_ref,
     m_s, l_s, acc_s) = refs[pps:]
    j = pl.program_id(1)
    nsteps = pl.num_programs(1)
    rows = SAMPLE_T * MLA_HEADS
    q = q_ref[0].astype(F32)
    head_of_lane = _iota((SUB, 1024), 1) // LANE
    sub = _iota((SUB, 1024), 0)
    qbd = jnp.concatenate(
        [jnp.where(head_of_lane == sub, jnp.broadcast_to(q[t:t + 1, :], (SUB, 1024)), 0.0)
         for t in range(SAMPLE_T)], axis=0).astype(MXU_DTYPE)
    q_pos = past + _iota((rows, 1), 0) // MLA_HEADS
    scale = MLA_QK ** -0.5

    @pl.when(j == 0)
    def _():
        m_s[...] = jnp.full(m_s.shape, NEG, F32)
        l_s[...] = jnp.zeros(l_s.shape, F32)
        acc_s[...] = jnp.zeros(acc_s.shape, F32)

    def update(s, mask, v, vt):
        m, l, acc = _online_update((m_s[...], l_s[...], acc_s[...]), s, mask, v, vt)
        m_s[...] = m
        l_s[...] = l
        acc_s[...] = acc

    @pl.when(j < nsteps - 1)
    def _():
        n = 4 * PAGE
        for g in range(pps // 4):
            xt = jnp.concatenate([pages[g * 4 + p][...] for p in range(4)], axis=1)
            kpet = xt[256:288]
            kv = jnp.dot(wkvt_ref[...], xt[0:256].astype(MXU_DTYPE), preferred_element_type=F32)
            pe2 = jnp.sum(kpet * kpet, axis=0, keepdims=True)
            parts = []
            for h in range(MLA_HEADS):
                kn = kv[h * MLA_NOPE:(h + 1) * MLA_NOPE]
                rs = lax.rsqrt((jnp.sum(kn * kn, axis=0, keepdims=True) + pe2) / float(MLA_QK) + EPS)
                parts += [(kn * rs) * gkc_ref[0:MLA_NOPE, :], (kpet * rs) * gkc_ref[MLA_NOPE:MLA_QK, :],
                          jnp.zeros((LANE - MLA_QK, n), F32)]
            kmt = jnp.concatenate(parts, axis=0).astype(MXU_DTYPE)
            s = jnp.dot(qbd, kmt, preferred_element_type=F32) * scale
            update(s, None, kv[512:1024].astype(MXU_DTYPE), True)

    @pl.when(j == nsteps - 1)
    def _():
        ckv = jnp.concatenate([newm_ref[0][:, 0:256], jnp.zeros((PAGE - SAMPLE_T, 256), F32)], axis=0)
        kpe = jnp.concatenate([newkp_ref[0], jnp.zeros((PAGE - SAMPLE_T, 128), F32)], axis=0)
        km, vm = _mla_keys(ckv, kpe, wk_ref[...], wv_ref[...], pk_ref[...], g8_ref[...], gk_ref[...])
        update(_mm_nt(qbd, km) * scale, (past + _iota((1, PAGE), 1)) <= q_pos, vm, False)
        o = _finish(l_s[...], acc_s[...])
        lane_head = _iota((SUB, 512), 1) // MLA_V
        sub8 = _iota((SUB, 512), 0)
        out = jnp.zeros((SAMPLE_T, 512), F32)
        for t in range(SAMPLE_T):
            blk = jnp.where(lane_head == sub8, o[t * SUB:(t + 1) * SUB, :], 0.0)
            r = jnp.sum(blk, axis=0, keepdims=True)
            out = jnp.where(_iota((SAMPLE_T, 512), 0) == t, jnp.broadcast_to(r, (SAMPLE_T, 512)), out)
        o_ref[0] = out


def _mla_sample(page_table, cache_t, layer, qm, newm, newkp, wkvt, gkc, wk, wv, pk, g8, gk, pps):
    db, n_pages = page_table.shape
    past = n_pages * PAGE
    nsteps = n_pages // pps + 1
    last = n_pages - 1

    def page_spec(p):
        return pl.BlockSpec((None, None, 288, PAGE),
                            lambda b, j, pt: (layer, pt[b, jnp.minimum(j * pps + p, last)], 0, 0))

    full = lambda a: pl.BlockSpec(a.shape, lambda b, j, pt: (0, 0))
    per_b = lambda w: pl.BlockSpec((1, SAMPLE_T, w), lambda b, j, pt: (b, 0, 0))
    rows = SAMPLE_T * MLA_HEADS
    kern = functools.partial(_mla_sample_kernel, pps=pps, past=past)
    return pl.pallas_call(
        kern,
        grid_spec=pltpu.PrefetchScalarGridSpec(
            num_scalar_prefetch=1,
            grid=(db, nsteps),
            in_specs=[page_spec(p) for p in range(pps)] + [per_b(1024), per_b(288), per_b(128), full(wkvt), full(gkc),
                                                          full(wk), full(wv), full(pk), full(g8), full(gk)],
            out_specs=per_b(512),
            scratch_shapes=[pltpu.VMEM((rows, 1), F32), pltpu.VMEM((rows, 1), F32), pltpu.VMEM((rows, 512), F32)],
        ),
        out_shape=jax.ShapeDtypeStruct((db, SAMPLE_T, 512), F32),
        compiler_params=_cparams(("parallel", "arbitrary"), 40),
        name="mla_sample",
    )(page_table, *([cache_t] * pps), qm, newm, newkp, wkvt, gkc, wk, wv, pk, g8, gk)


def _cmp_blocks(rows, wpos):
    n = rows.shape[0] // NSA_BLK
    x = rows.reshape(n, NSA_BLK, 256) * wpos[None]
    return jnp.sum(x, axis=1) / float(NSA_BLK)


def _cmp_rope(craw, ctab, swc):
    return craw * ctab[:, 0:256] + _place(craw, swc) * ctab[:, 256:512]


def _nsa_cmp_kernel(rn_ref, wpos_ref, ctab_ref, swc_ref, o_ref):
    craw = _cmp_blocks(rn_ref[...], wpos_ref[...])
    o_ref[...] = _cmp_rope(craw, ctab_ref[...], swc_ref[...]).astype(o_ref.dtype)


def _nsa_cmp(rn, wpos, ctab, swc, nb, t):
    nblk = t // NSA_BLK
    tb = min(nblk, 16)
    nt = nblk // tb
    return pl.pallas_call(
        _nsa_cmp_kernel,
        grid=(nb, nt),
        in_specs=[pl.BlockSpec((tb * NSA_BLK, 256), lambda b, i: (b * nt + i, 0)),
                  pl.BlockSpec(wpos.shape, lambda b, i: (0, 0)),
                  pl.BlockSpec((tb, 512), lambda b, i: (i, 0)),
                  pl.BlockSpec(swc.shape, lambda b, i: (0, 0))],
        out_specs=pl.BlockSpec((tb, 256), lambda b, i: (b * nt + i, 0)),
        out_shape=jax.ShapeDtypeStruct((nb * nblk, 256), MXU_DTYPE),
        compiler_params=_cparams(("parallel", "parallel"), 32),
        name="nsa_cmp",
    )(rn, wpos, ctab, swc)


def _top_n_mask(x, n_sel):
    t, w = x.shape
    lane = _iota((t, w), 1).astype(F32)

    def body(_, carry):
        x, sel = carry
        m = jnp.max(x, axis=-1, keepdims=True)
        first = jnp.min(jnp.where(x == m, lane, float(w)), axis=-1, keepdims=True)
        hit = lane == first
        return jnp.where(hit, -jnp.inf, x), jnp.where(hit, 1.0, sel)

    _, sel = lax.fori_loop(0, n_sel, body, (x, jnp.zeros((t, w), F32)))
    return sel


def _nsa_attend(qa, misc, q_pos, cmp, nblk, key_ref, n_chunks, tkc, win, w_pos, pqc, pqs, pqw):
    tq = qa.shape[0]
    h = NSA_HEADS
    scale = D_HEAD ** -0.5
    nbp = cmp.shape[0]
    qp4 = jnp.concatenate([q_pos] * h, axis=0)
    qc_st = _stack_heads(_mm(qa, pqc), h, 256)
    blk = _iota((1, nbp), 1)
    s = _mm_nt(qc_st, cmp) * scale
    p_c = _softmax_rows(s, ((blk * NSA_BLK + (NSA_BLK - 1)) <= qp4) & (blk < nblk))
    o_c = _mm(p_c, cmp)
    imp = p_c[0:tq]
    for hh in range(1, h):
        imp = imp + p_c[hh * tq:(hh + 1) * tq]
    cur = q_pos // NSA_BLK
    imp = jnp.where(blk == cur, NSA_FORCE, jnp.where(blk < cur, imp, -1.0))
    imp = jnp.where(blk < nblk, imp, -jnp.inf)
    sel = _top_n_mask(imp, min(NSA_TOPN, nblk))
    qs_st = _stack_heads(_mm(qa, pqs), h, 256) * scale
    blk_col = _iota((nbp, tkc), 0)
    tok_col = _iota((nbp, tkc), 1)

    def chunk(c, carry):
        k0 = pl.multiple_of(c * tkc, tkc)
        keys = key_ref[pl.ds(k0, tkc), :]
        expand = jnp.where(((k0 + tok_col) // NSA_BLK) == blk_col, 1.0, 0.0)
        tok = _mm(sel, expand)
        bias1 = jnp.where((tok > 0.5) & ((k0 + _iota((1, tkc), 1)) <= q_pos), 0.0, NEG)
        s = _mm_nt(qs_st, keys) + jnp.concatenate([bias1] * h, axis=0)
        return _online_update(carry, s, None, keys)

    _, l, acc = lax.fori_loop(0, n_chunks, chunk, _flash_init(h * tq, 256))
    o_s = _finish(l, acc)
    qw_st = _stack_heads(_mm(qa, pqw), h, 128)
    rel = qp4 - w_pos
    s = _mm_nt(qw_st, win) * scale
    p_w = _softmax_rows(s, (rel >= 0) & (rel < NSA_WINDOW) & (w_pos >= 0))
    o_w = _mm(p_w, win)
    gate = lambda jj: jnp.concatenate([misc[:, 3 * hh + jj:3 * hh + jj + 1] for hh in range(h)], axis=0)
    tsum = gate(0) * o_c[:, 0:128] + gate(1) * o_s[:, 128:256] + gate(2) * o_w
    tsum = jnp.where(_iota((1, 128), 1) >= D_HEAD, tsum, 0.0)
    return jnp.concatenate([tsum[hh * tq:(hh + 1) * tq] for hh in range(h)], axis=1)


def _nsa_prompt_kernel(qa_ref, misc_ref, cmp_ref, nk_ref, wk_ref, pqc_ref, pqs_ref, pqw_ref, o_ref,
                       *, tq, tkc, t):
    i = pl.program_id(1)
    s0 = i * tq
    q_pos = s0 + _iota((tq, 1), 0)
    w = NSA_WINDOW + tq
    kstart = pl.multiple_of(jnp.clip(s0 - NSA_WINDOW, 0, t - w), SUB)
    win = wk_ref[pl.ds(kstart, w), :]
    w_pos = kstart + _iota((1, w), 1)
    n_chunks = (s0 + tq + tkc - 1) // tkc
    o_ref[...] = _nsa_attend(qa_ref[...], misc_ref[...], q_pos, cmp_ref[...], t // NSA_BLK, nk_ref, n_chunks,
                             tkc, win, w_pos, pqc_ref[...], pqs_ref[...], pqw_ref[...])


def _nsa_prompt(qa, misc, cmp, nk, wk, pqc, pqs, pqw, nb, t, tq, tkc):
    nq = t // tq
    nblk = t // NSA_BLK
    kern = functools.partial(_nsa_prompt_kernel, tq=tq, tkc=tkc, t=t)
    full = lambda a: pl.BlockSpec(a.shape, lambda b, i: (0, 0))
    return pl.pallas_call(
        kern,
        grid=(nb, nq),
        in_specs=[pl.BlockSpec((tq, 256), lambda b, i: (b * nq + i, 0)),
                  pl.BlockSpec((tq, 128), lambda b, i: (b * nq + i, 0)),
                  pl.BlockSpec((nblk, 256), lambda b, i: (b, 0)),
                  pl.BlockSpec((t, 256), lambda b, i: (b, 0)),
                  pl.BlockSpec((t, 128), lambda b, i: (b, 0)),
                  full(pqc), full(pqs), full(pqw)],
        out_specs=pl.BlockSpec((tq, 512), lambda b, i: (b * nq + i, 0)),
        out_shape=jax.ShapeDtypeStruct((nb * t, 512), F32),
        compiler_params=_cparams(("parallel", "arbitrary"), 48),
        name="nsa_prompt",
    )(qa, misc, cmp, nk, wk, pqc, pqs, pqw)


def _nsa_sample_kernel(pt_ref, *refs, pps, past, tkc):
    pages = refs[:pps]
    (qa_ref, misc_ref, newn_ref, sw_ref, neww_ref, wpos_ref, ctab_ref, swc_ref, pqc_ref, pqs_ref, pqw_ref,
     o_ref, key_s, cmp_s, win_s) = refs[pps:]
    j = pl.program_id(1)
    nsteps = pl.num_programs(1)
    nblk = past // NSA_BLK + 1
    bps = pps * PAGE // NSA_BLK

    @pl.when(j == 0)
    def _():
        key_s[...] = jnp.zeros(key_s.shape, key_s.dtype)
        cmp_s[...] = jnp.zeros(cmp_s.shape, F32)

    @pl.when(j < nsteps - 1)
    def _():
        rows = jnp.concatenate([pages[p][...] for p in range(pps)], axis=0)
        r0 = pl.multiple_of(j * pps * PAGE, pps * PAGE)
        key_s[pl.ds(r0, pps * PAGE), :] = rows.astype(key_s.dtype)
        b0 = pl.multiple_of(j * bps, bps)
        cmp_s[pl.ds(b0, bps), :] = _cmp_blocks(rows, wpos_ref[...])

    @pl.when(j == nsteps - 1)
    def _():
        newn = newn_ref[0]
        tail = jnp.concatenate([newn, jnp.zeros((NSA_BLK - SAMPLE_T, 256), F32)], axis=0)
        key_s[past:past + NSA_BLK, :] = tail.astype(key_s.dtype)
        cmp_s[past // NSA_BLK: past // NSA_BLK + 1, :] = _cmp_blocks(tail, wpos_ref[...])
        cmp = _cmp_rope(cmp_s[...], ctab_ref[...], swc_ref[...]).astype(MXU_DTYPE)
        win_s[0:NSA_WINDOW, :] = sw_ref[0].astype(win_s.dtype)
        win_s[NSA_WINDOW:NSA_WINDOW + 2 * SAMPLE_T, :] = jnp.concatenate(
            [neww_ref[0], jnp.zeros((SAMPLE_T, 128), F32)], axis=0).astype(win_s.dtype)
        w = NSA_WINDOW + 2 * SAMPLE_T
        w_pos = past - NSA_WINDOW + _iota((1, w), 1)
        q_pos = past + _iota((SAMPLE_T, 1), 0)
        n_chunks = (past + SAMPLE_T + tkc - 1) // tkc
        o_ref[0] = _nsa_attend(qa_ref[0], misc_ref[0], q_pos, cmp, nblk, key_s, n_chunks, tkc, win_s[...], w_pos,
                               pqc_ref[...], pqs_ref[...], pqw_ref[...])


def _nsa_sample(page_table, cache, layer, qa, misc, newn, state_win, neww, wpos, ctab, swc, pqc, pqs, pqw,
                pps, tkc):
    db, n_pages = page_table.shape
    past = n_pages * PAGE
    nsteps = n_pages // pps + 1
    last = n_pages - 1
    nbp = ctab.shape[0]
    lpad = ((past + SAMPLE_T + tkc - 1) // tkc) * tkc

    def page_spec(p):
        return pl.BlockSpec((None, None, PAGE, 256),
                            lambda b, j, pt: (layer, pt[b, jnp.minimum(j * pps + p, last)], 0, 0))

    full = lambda a: pl.BlockSpec(a.shape, lambda b, j, pt: (0, 0))
    per_b = lambda w: pl.BlockSpec((1, SAMPLE_T, w), lambda b, j, pt: (b, 0, 0))
    kern = functools.partial(_nsa_sample_kernel, pps=pps, past=past, tkc=tkc)
    return pl.pallas_call(
        kern,
        grid_spec=pltpu.PrefetchScalarGridSpec(
            num_scalar_prefetch=1,
            grid=(db, nsteps),
            in_specs=[page_spec(p) for p in range(pps)] + [
                per_b(256), per_b(128), per_b(256),
                pl.BlockSpec((None, 1, NSA_WINDOW, 128), lambda b, j, pt: (layer, b, 0, 0)),
                per_b(128), full(wpos), full(ctab), full(swc), full(pqc), full(pqs), full(pqw)],
            out_specs=per_b(512),
            scratch_shapes=[pltpu.VMEM((lpad, 256), MXU_DTYPE), pltpu.VMEM((nbp, 256), F32),
                            pltpu.VMEM((NSA_WINDOW + 2 * SAMPLE_T, 128), MXU_DTYPE)],
        ),
        out_shape=jax.ShapeDtypeStruct((db, SAMPLE_T, 512), F32),
        compiler_params=_cparams(("parallel", "arbitrary"), 48),
        name="nsa_sample",
    )(page_table, *([cache] * pps), qa, misc, newn, state_win, neww, wpos, ctab, swc, pqc, pqs, pqw)


def _loop(n, body, init):
    if isinstance(n, int):
        for c in range(n):
            init = body(c, init)
        return init
    return lax.fori_loop(0, n, body, init)


def _dsa_attend(qc, qi, misc, q_pos, key_ref, n_chunks, tkc, n_top, pqd, pqi, tri, key_s, kt=False):
    tq = qc.shape[0]
    h = DSA_HEADS
    qi_st = _stack_heads(_mm(qi, pqi), IDX_HEADS, 256)
    wcol = [misc[:, 12 + g:13 + g] for g in range(IDX_HEADS)]
    qk = _mm if kt else _mm_nt

    def get_keys(c):
        if kt:
            return key_ref[c]
        return key_ref[pl.ds(c * tkc if isinstance(c, int) else pl.multiple_of(c * tkc, tkc), tkc), :]

    def score_chunk(c, _):
        k0 = c * tkc
        rel = jnp.maximum(qk(qi_st, get_keys(c)), 0.0)
        sc = wcol[0] * rel[0:tq]
        for g in range(1, IDX_HEADS):
            sc = sc + wcol[g] * rel[g * tq:(g + 1) * tq]
        sc = jnp.where(sc == 0.0, 0.0, sc)
        sc = jnp.where((k0 + _iota((1, tkc), 1)) <= q_pos, sc, NEG)
        key_s[c] = _sortable(sc)
        return 0

    _loop(n_chunks, score_chunk, 0)

    def count(pred_fn):
        def body(c, acc):
            hit = jnp.where(pred_fn(key_s[c]), 1.0, 0.0)
            for u in range(tkc // LANE):
                acc = acc + hit[:, u * LANE:(u + 1) * LANE]
            return acc
        acc = _loop(n_chunks, body, jnp.zeros((tq, LANE), F32))
        return jnp.sum(acc, axis=-1, keepdims=True)

    def bit_step(it, thr):
        cand = thr + jnp.left_shift(jnp.int32(1), jnp.int32(31) - it)
        cnt = count(lambda k: k >= cand)
        return jnp.where(cnt >= float(n_top), cand, thr)

    thr = lax.fori_loop(0, 32, bit_step, jnp.full((tq, 1), -2 ** 31, I32))
    need = float(n_top) - count(lambda k: k > thr)

    qd_st = _stack_heads(_mm(qc, pqd), h, 256) * (D_HEAD ** -0.5)

    def chunk(c, carry):
        eq_seen, fl = carry
        keys = get_keys(c)
        kk = key_s[c]
        eq = jnp.where(kk == thr, 1.0, 0.0)
        rank = eq_seen + _mm(eq, tri)
        sel = (kk > thr) | ((kk == thr) & (rank <= need))
        bias1 = jnp.where(sel & ((c * tkc + _iota((1, tkc), 1)) <= q_pos), 0.0, NEG)
        s = qk(qd_st, keys) + jnp.concatenate([bias1] * h, axis=0)
        return eq_seen + jnp.sum(eq, axis=-1, keepdims=True), _online_update(fl, s, None, keys, kt)

    init = (jnp.zeros((tq, 1), F32), _flash_init(h * tq, 256))
    _, (_, l, acc) = _loop(n_chunks, chunk, init)
    o = _finish(l, acc)[:, 0:128]
    o = jnp.where(_iota((1, 128), 1) >= D_HEAD, o, 0.0)
    return jnp.concatenate([o[hh * tq:(hh + 1) * tq] for hh in range(h)], axis=1)


def _dsa_prompt_kernel(qc_ref, qi_ref, misc_ref, dk_ref, pqd_ref, pqi_ref, tri_ref, o_ref, key_s,
                       *, tq, tkc, n_top):
    i = pl.program_id(1)
    s0 = i * tq
    q_pos = s0 + _iota((tq, 1), 0)
    n_chunks = (s0 + tq + tkc - 1) // tkc
    o_ref[...] = _dsa_attend(qc_ref[...], qi_ref[...], misc_ref[...], q_pos, dk_ref, n_chunks, tkc, n_top,
                             pqd_ref[...], pqi_ref[...], tri_ref[...], key_s)


def _dsa_prompt(qc, qi, misc, dk, pqd, pqi, tri, nb, t, tq, tkc):
    nq = t // tq
    n_top = min(DSA_TOPK_MAX, t // 4)
    kern = functools.partial(_dsa_prompt_kernel, tq=tq, tkc=tkc, n_top=n_top)
    full = lambda a: pl.BlockSpec(a.shape, lambda b, i: (0, 0))
    return pl.pallas_call(
        kern,
        grid=(nb, nq),
        in_specs=[pl.BlockSpec((tq, 256), lambda b, i: (b * nq + i, 0)),
                  pl.BlockSpec((tq, 256), lambda b, i: (b * nq + i, 0)),
                  pl.BlockSpec((tq, 128), lambda b, i: (b * nq + i, 0)),
                  pl.BlockSpec((t, 256), lambda b, i: (b, 0)),
                  full(pqd), full(pqi), full(tri)],
        out_specs=pl.BlockSpec((tq, 512), lambda b, i: (b * nq + i, 0)),
        out_shape=jax.ShapeDtypeStruct((nb * t, 512), F32),
        scratch_shapes=[pltpu.VMEM((t // tkc, tq, tkc), I32)],
        compiler_params=_cparams(("parallel", "arbitrary"), 48),
        name="dsa_prompt",
    )(qc, qi, misc, dk, pqd, pqi, tri)


def _dsa_sample_kernel(pt_ref, *refs, pps, past, tkc, n_top):
    pages = refs[:pps]
    (qc_ref, qi_ref, misc_ref, newd_ref, pqd_ref, pqi_ref, tri_ref, o_ref, key_t, key_s) = refs[pps:]
    j = pl.program_id(1)
    nsteps = pl.num_programs(1)
    ppc = tkc // PAGE
    n_chunks = past // tkc + 1

    @pl.when(j == 0)
    def _():
        key_t[:, 160:256, :] = jnp.zeros((n_chunks, 96, tkc), key_t.dtype)
        key_t[n_chunks - 1] = jnp.zeros((256, tkc), key_t.dtype)

    @pl.when(j < nsteps - 1)
    def _():
        for p in range(pps):
            c = j * (pps // ppc) + p // ppc
            key_t[c, 0:160, (p % ppc) * PAGE:(p % ppc + 1) * PAGE] = pages[p][...].astype(key_t.dtype)

    @pl.when(j == nsteps - 1)
    def _():
        new_rows = jnp.concatenate([newd_ref[0].astype(F32), jnp.zeros((PAGE - SAMPLE_T, 256), F32)], axis=0)
        key_t[n_chunks - 1, :, 0:PAGE] = new_rows.T.astype(key_t.dtype)
        q_pos = past + _iota((SAMPLE_T, 1), 0)
        o_ref[0] = _dsa_attend(qc_ref[0], qi_ref[0], misc_ref[0], q_pos, key_t, n_chunks, tkc, n_top,
                               pqd_ref[...], pqi_ref[...], tri_ref[...], key_s, kt=True)


def _dsa_sample(page_table, cache_t, layer, qc, qi, misc, newd, pqd, pqi, tri, pps, tkc, ts):
    db, n_pages = page_table.shape
    past = n_pages * PAGE
    assert past % tkc == 0 and pps % (tkc // PAGE) == 0
    nsteps = n_pages // pps + 1
    last = n_pages - 1
    n_chunks = past // tkc + 1
    n_top = min(DSA_TOPK_MAX, (past + ts) // 4)

    def page_spec(p):
        return pl.BlockSpec((None, None, 160, PAGE),
                            lambda b, j, pt: (layer, pt[b, jnp.minimum(j * pps + p, last)], 0, 0))

    full = lambda a: pl.BlockSpec(a.shape, lambda b, j, pt: (0, 0))
    per_b = lambda w: pl.BlockSpec((1, SAMPLE_T, w), lambda b, j, pt: (b, 0, 0))
    kern = functools.partial(_dsa_sample_kernel, pps=pps, past=past, tkc=tkc, n_top=n_top)
    return pl.pallas_call(
        kern,
        grid_spec=pltpu.PrefetchScalarGridSpec(
            num_scalar_prefetch=1,
            grid=(db, nsteps),
            in_specs=[page_spec(p) for p in range(pps)] + [
                per_b(256), per_b(256), per_b(128), per_b(256), full(pqd), full(pqi), full(tri)],
            out_specs=per_b(512),
            scratch_shapes=[pltpu.VMEM((n_chunks, 256, tkc), MXU_DTYPE), pltpu.VMEM((n_chunks, SAMPLE_T, tkc), I32)],
        ),
        out_shape=jax.ShapeDtypeStruct((db, SAMPLE_T, 512), F32),
        compiler_params=_cparams(("parallel", "arbitrary"), 48),
        name="dsa_sample",
    )(page_table, *([cache_t] * pps), qc, qi, misc, newd, pqd, pqi, tri)


def _out_kernel(h_ref, oa_ref, om_ref, od_ref, wa_ref, wm_ref, wd_ref, g_ref, h1_o, xn_o):
    h1 = h_ref[...] + (_mm(oa_ref[...], wa_ref[...]) + _mm(om_ref[...], wm_ref[...]) + _mm(od_ref[...], wd_ref[...]))
    h1_o[...] = h1
    xn = h1 * lax.rsqrt(jnp.mean(h1 * h1, axis=-1, keepdims=True) + EPS) * g_ref[...]
    xn_o[...] = xn.astype(xn_o.dtype)


def _out_proj(h, oa, om, od, wa, wm, wd, g, tm):
    n = h.shape[0]
    row = lambda w: pl.BlockSpec((tm, w), lambda i: (i, 0))
    full = lambda a: pl.BlockSpec(a.shape, lambda i: (0, 0))
    return pl.pallas_call(
        _out_kernel,
        grid=(n // tm,),
        in_specs=[row(D_MODEL), row(512), row(512), row(512), full(wa), full(wm), full(wd), full(g)],
        out_specs=[row(D_MODEL), row(D_MODEL)],
        out_shape=[jax.ShapeDtypeStruct((n, D_MODEL), F32), jax.ShapeDtypeStruct((n, D_MODEL), MXU_DTYPE)],
        compiler_params=_cparams(("parallel",), 40),
        name="out_proj",
    )(h, oa, om, od, wa, wm, wd, g)


def _ffn_kernel(*refs, tm, seq, sample):
    if sample:
        (xn_ref, h1_ref, p_ref, wu_ref, wg_ref, wd_ref, cw_ref, cb_ref, gp_ref, wpg_ref, wpp_ref,
         p1_ref, p2_ref, h_o, a_o, acc_s) = refs
    else:
        (xn_ref, halo_ref, h1_ref, p_ref, wu_ref, wg_ref, wd_ref, cw_ref, cb_ref, gp_ref, wpg_ref, wpp_ref,
         h_o, a_o, acc_s) = refs
    i = pl.program_id(0)
    j = pl.program_id(1)
    xn = xn_ref[...]
    a = _mm(xn, wu_ref[...])
    b = _mm(xn, wg_ref[...])
    a_o[...] = a
    row = _iota((tm, 1), 0)
    r1 = pltpu.roll(a, 1, 0)
    r2 = pltpu.roll(a, 2, 0)
    if sample:
        tpos = row % SAMPLE_T
        prev1 = jnp.where(tpos == 0, p1_ref[...], r1)
        prev2 = jnp.where(tpos < 2, p2_ref[...], r2)
    else:
        ah = _mm(halo_ref[...], wu_ref[...])
        ah = jnp.where((i * tm) % seq == 0, 0.0, ah)
        prev1 = jnp.where(row == 0, ah[7:8, :], r1)
        prev2 = jnp.where(row == 0, ah[6:7, :], jnp.where(row == 1, ah[7:8, :], r2))
    c = cb_ref[...] + cw_ref[2:3, :] * a
    c = c + cw_ref[0:1, :] * prev2
    c = c + cw_ref[1:2, :] * prev1
    u = (c * jax.nn.sigmoid(c)) * b
    y = _mm(u, wd_ref[...])

    @pl.when(j == 0)
    def _():
        acc_s[...] = y

    @pl.when(j > 0)
    def _():
        acc_s[...] = acc_s[...] + y

    @pl.when(j == pl.num_programs(1) - 1)
    def _():
        h2 = h1_ref[...] + acc_s[...]
        xn3 = h2 * lax.rsqrt(jnp.mean(h2 * h2, axis=-1, keepdims=True) + EPS) * gp_ref[...]
        gate = jax.nn.sigmoid(_mm(xn3, wpg_ref[...]))
        h_o[...] = h2 + gate * _mm(p_ref[...], wpp_ref[...])


def _ffn(xn, h1, p, wu, wg, wd, cw, cb, gp, wpg, wpp, tm, tf, seq, prefix=None):
    n = xn.shape[0]
    nf = D_FF // tf
    sample = prefix is not None
    row = lambda w: pl.BlockSpec((tm, w), lambda i, j: (i, 0))
    full = lambda a: pl.BlockSpec(a.shape, lambda i, j: (0, 0))
    ff_col = lambda r: pl.BlockSpec((r, tf), lambda i, j: (0, j))
    in_specs = [row(D_MODEL)]
    args = [xn]
    if not sample:
        in_specs.append(pl.BlockSpec((SUB, D_MODEL), lambda i, j: (jnp.maximum(i * (tm // SUB) - 1, 0), 0)))
        args.append(xn)
    in_specs += [row(D_MODEL), row(PLE_DIM), ff_col(D_MODEL), ff_col(D_MODEL),
                 pl.BlockSpec((tf, D_MODEL), lambda i, j: (j, 0)), ff_col(CONV_W), ff_col(1),
                 full(gp), full(wpg), full(wpp)]
    args += [h1, p, wu, wg, wd, cw, cb, gp, wpg, wpp]
    if sample:
        in_specs += [pl.BlockSpec((tm, tf), lambda i, j: (i, j))] * 2
        args += list(prefix)
    kern = functools.partial(_ffn_kernel, tm=tm, seq=seq, sample=sample)
    return pl.pallas_call(
        kern,
        grid=(n // tm, nf),
        in_specs=in_specs,
        out_specs=[row(D_MODEL), pl.BlockSpec((tm, tf), lambda i, j: (i, j))],
        out_shape=[jax.ShapeDtypeStruct((n, D_MODEL), F32), jax.ShapeDtypeStruct((n, D_FF), F32)],
        scratch_shapes=[pltpu.VMEM((tm, D_MODEL), F32)],
        compiler_params=_cparams(("parallel", "arbitrary"), 56),
        name="ffn",
    )(*args)


def _pick(n, prefs):
    for p in prefs:
        if n % p == 0:
            return p
    return n


def kernel(x_prompt, x_sample, cache_nsa, cache_mla, cache_dsa, state_win, state_conv, page_table, p_prompt, p_sample, norm_mix, w_in, nsa_qn, nsa_kn, nsa_cmp_pos, mla_cqn, mla_ckvn, mla_w_uq, mla_w_ukv, mla_qn, mla_kn, dsa_qn, dsa_kn, w_out, norm_ffn, ffn_w_up, ffn_w_gate, ffn_conv_w, ffn_conv_b, ffn_w_down, norm_ple, ple_w_gate, ple_w_proj):
    depth = w_in.shape[0]
    B, T, _ = x_prompt.shape
    DB, TS, _ = x_sample.shape
    n_pages = page_table.shape[1]
    past = n_pages * PAGE
    assert TS <= 4 and T % 128 == 0 and T >= NSA_WINDOW + 128 and past >= NSA_WINDOW
    mm = MXU_DTYPE
    bf = lambda a: jnp.asarray(a, jnp.bfloat16)

    win_cols, wuq_cols = _win_cols(), _wuq_cols()
    g256 = bf(_blockdiag(256, 64))
    g1024 = bf(_blockdiag(1024, 128))
    g8 = bf((np.arange(1024)[:, None] // 128 == np.arange(128)[None, :]).astype(np.float32))
    pk = bf(_placement(128, 1024, [(j, h * 128 + 64 + j) for h in range(8) for j in range(32)]))
    swc = bf(_placement(256, 256, [(j, (j + 32) % 64) for j in range(64)]))
    pq_c = bf(_q_placement(4, 64, 256, 0))
    pq_s = bf(_q_placement(4, 64, 256, 128))
    pq_w = bf(_q_placement(4, 64, 128, 0))
    pq_d = bf(_q_placement(4, 64, 256, 0))
    pq_i = bf(_q_placement(8, 32, 256, 128))
    tkc = _pick(T, (512, 256, 128))
    tri = bf(np.triu(np.ones((tkc, tkc), np.float32)))
    tab_p = _rope_tables(jnp.arange(T))
    pos_s = past + (jnp.arange(DB * SAMPLE_T) % SAMPLE_T)
    tab_s = _rope_tables(pos_s)
    nblk_p = T // NSA_BLK
    ctab_p = _cmp_tables(nblk_p)
    nbp_s = ((past // NSA_BLK + 1 + LANE - 1) // LANE) * LANE
    ctab_s = _cmp_tables(nbp_s)
    exp_rows = np.full((512,), -1, np.int64)
    for h in range(4):
        exp_rows[h * 128 + 64: (h + 1) * 128] = h * 64 + np.arange(64)

    tm_p = _pick(B * T, (256, 128))
    tq = 128
    tq_m = _pick(T, (256, 128))
    tm_o = _pick(B * T, (512, 256, 128))
    tf = D_FF // 2
    ns = DB * SAMPLE_T

    cache_mla_t = jnp.swapaxes(cache_mla, 2, 3)
    cache_dsa_t = jnp.swapaxes(cache_dsa, 2, 3)
    hp = x_prompt.reshape(B * T, D_MODEL)
    hs = jnp.pad(x_sample, ((0, 0), (0, SAMPLE_T - TS), (0, 0))).reshape(ns, D_MODEL)
    outs = {k: [] for k in ("nsa_p", "nsa_s", "mla_p", "mla_s", "dsa_p", "dsa_s", "win_p", "win_s", "conv_p", "conv_s")}
    unpad = lambda a: a.reshape(DB, SAMPLE_T, -1)[:, :TS]

    for i in range(depth):
        wbig = _take_cols(w_in[i], win_cols).astype(mm)
        wuq = _take_cols(mla_w_uq[i], wuq_cols).astype(mm)
        gv = _gain_vector(norm_mix[i], nsa_qn[i], nsa_kn[i], mla_cqn[i], mla_ckvn[i], dsa_qn[i], dsa_kn[i], mla_qn[i])
        ukv = mla_w_ukv[i].reshape(MLA_KV_RANK, MLA_HEADS, MLA_NOPE + MLA_V)
        wk = jnp.pad(ukv[:, :, :MLA_NOPE], ((0, 0), (0, 0), (0, LANE - MLA_NOPE))).reshape(MLA_KV_RANK, 1024).astype(mm)
        wv = ukv[:, :, MLA_NOPE:].reshape(MLA_KV_RANK, 512).astype(mm)
        gk = jnp.tile(jnp.concatenate([mla_kn[i], jnp.zeros((32,), F32)]), 8)[None, :]
        wkvt = jnp.concatenate([ukv[:, :, :MLA_NOPE].reshape(MLA_KV_RANK, 512),
                                ukv[:, :, MLA_NOPE:].reshape(MLA_KV_RANK, 512)], axis=1).T.astype(mm)
        gkc = mla_kn[i][:, None]
        wpos = jnp.concatenate([nsa_cmp_pos[i, 0], nsa_cmp_pos[i, 1], jnp.zeros((NSA_BLK, 128), F32)], axis=1)
        wo = w_out[i]
        expand = lambda w: jnp.where(jnp.asarray(exp_rows >= 0)[:, None],
                                     jnp.take(w, jnp.asarray(np.maximum(exp_rows, 0)), axis=0), 0.0).astype(mm)
        wo_a, wo_m, wo_d = expand(wo[0:256]), wo[256:768].astype(mm), expand(wo[768:1024])
        ffn_w = (ffn_w_up[i].astype(mm), ffn_w_gate[i].astype(mm), ffn_w_down[i].astype(mm), ffn_conv_w[i],
                 ffn_conv_b[i][None, :], norm_ple[i][None, :], ple_w_gate[i].astype(mm), ple_w_proj[i].astype(mm))
        g_ffn = norm_ffn[i][None, :]

        (qa, qm, qc, qi, misc, rn, rm, rd, rw, nk, wkk, dk, kp) = _proj(
            hp, tab_p, T // tm_p, gv, wbig, wuq, g256, g1024, tm_p)
        km, vm = _mla_prep(rm, kp, wk, wv, pk, g8, gk, _pick(B * T, (512, 256, 128)))
        o_m = _mla_flash(qm, km, vm, B, T, tq_m, _pick(T, (512, 256, 128)))
        cmp = _nsa_cmp(rn, wpos, ctab_p, swc, B, T)
        o_a = _nsa_prompt(qa, misc, cmp, nk, wkk, pq_c, pq_s, pq_w, B, T, tq, tkc)
        o_d = _dsa_prompt(qc, qi, misc, dk, pq_d, pq_i, tri, B, T, tq, tkc)
        h1, xn2 = _out_proj(hp, o_a, o_m, o_d, wo_a, wo_m, wo_d, g_ffn, tm_o)
        hp, a_p = _ffn(xn2, h1, p_prompt[i].reshape(B * T, PLE_DIM), *ffn_w, tm_o, tf, T)
        outs["nsa_p"].append(rn.reshape(B, T, -1))
        outs["mla_p"].append(rm.reshape(B, T, -1))
        outs["dsa_p"].append(rd.reshape(B, T, -1))
        outs["win_p"].append(rw.reshape(B, T, -1)[:, T - min(NSA_WINDOW, T):])
        outs["conv_p"].append(a_p.reshape(B, T, D_FF)[:, T - (CONV_W - 1):])

        (qa, qm, qc, qi, misc, rn, rm, rd, rw, nk, wkk, dk, kp) = _proj(
            hs, tab_s, 1, gv, wbig, wuq, g256, g1024, _pick(ns, (256, 128)))
        r3 = lambda a: a.reshape(DB, SAMPLE_T, -1)
        pps = _pick(n_pages, (8, 4))
        o_m = _mla_sample(page_table, cache_mla_t, i, r3(qm), r3(rm), r3(kp), wkvt, gkc, wk, wv, pk, g8, gk, pps)
        o_a = _nsa_sample(page_table, cache_nsa, i, r3(qa), r3(misc), r3(rn), state_win, r3(rw), wpos, ctab_s, swc,
                          pq_c, pq_s, pq_w, pps, tkc)
        o_d = _dsa_sample(page_table, cache_dsa_t, i, r3(qc), r3(qi), r3(misc), r3(dk), pq_d, pq_i, tri, pps, tkc, TS)
        tm_s = _pick(ns, (256, 128))
        h1, xn2 = _out_proj(hs, o_a.reshape(ns, 512), o_m.reshape(ns, 512), o_d.reshape(ns, 512),
                            wo_a, wo_m, wo_d, g_ffn, tm_s)
        sc = state_conv[i]
        zrow = jnp.zeros((DB, SAMPLE_T - 1, D_FF), F32)
        p1 = jnp.concatenate([sc[:, 1:2], zrow], axis=1).reshape(ns, D_FF)
        p2 = jnp.concatenate([sc[:, 0:2], zrow[:, 1:]], axis=1).reshape(ns, D_FF)
        p_s = jnp.pad(p_sample[i], ((0, 0), (0, SAMPLE_T - TS), (0, 0))).reshape(ns, PLE_DIM)
        hs, a_s = _ffn(xn2, h1, p_s, *ffn_w, tm_s, tf, SAMPLE_T, prefix=(p1, p2))
        outs["nsa_s"].append(unpad(rn))
        outs["mla_s"].append(unpad(rm))
        outs["dsa_s"].append(unpad(rd))
        win_all = jnp.concatenate([state_win[i], unpad(rw)], axis=1)
        outs["win_s"].append(win_all[:, TS:])
        a_ext = jnp.concatenate([sc, unpad(a_s)], axis=1)
        outs["conv_s"].append(a_ext[:, TS:])

    st = lambda k: jnp.stack(outs[k])
    return (hp.reshape(B, T, D_MODEL), unpad(hs),
            st("nsa_p"), st("nsa_s"), st("mla_p"), st("mla_s"), st("dsa_p"), st("dsa_s"),
            st("win_p"), st("win_s"), st("conv_p"), st("conv_s"))
```

```python
import functools
import math

import numpy as np
import jax
import jax.numpy as jnp
from jax import lax
from jax.experimental import pallas as pl
from jax.experimental.pallas import tpu as pltpu

F32 = jnp.float32
I32 = jnp.int32
MXU_DTYPE = jnp.bfloat16

D_MODEL = 1024
D_HEAD = 64
ROPE_THETA = 10000.0
EPS = 1e-6
NEG = -1e30
MASK_FLOOR = -1e29
NSA_HEADS = 4
NSA_BLK = 64
NSA_TOPN = 16
NSA_WINDOW = 512
NSA_FORCE = 1e9
MLA_HEADS = 8
MLA_Q_RANK = 256
MLA_KV_RANK = 256
MLA_NOPE = 64
MLA_ROPE = 32
MLA_V = 64
MLA_QK = MLA_NOPE + MLA_ROPE
DSA_HEADS = 4
IDX_HEADS = 8
IDX_DIM = 32
DSA_TOPK_MAX = 256
D_FF = 2816
CONV_W = 3
PLE_DIM = 256
PAGE = 128
LANE = 128
SUB = 8
SAMPLE_T = 8

C_QA, C_KC, C_VC, C_KS, C_VS, C_KW, C_VW, C_GA = 0, 256, 320, 384, 448, 512, 576, 640
C_CQ, C_CKV, C_KPE, C_QC, C_KD, C_VD, C_QI, C_KI, C_WI = 652, 908, 1164, 1196, 1452, 1516, 1580, 1836, 1868

(Z_QA, Z_QAS, Z_NSA, Z_NSAS, Z_WIN, Z_WINS, Z_MISC, Z_CQ, Z_CKV, Z_KPE, Z_KPES, Z_QC, Z_QCS,
 Z_DKV, Z_DKVS, Z_KI, Z_KIS, Z_QI, Z_QIS, Z_TOTAL) = (
    0, 256, 512, 768, 1024, 1152, 1280, 1408, 1664, 1920, 2048, 2176, 2432, 2688, 2816, 2944,
    3072, 3200, 3456, 3712)

(T_64C, T_64S, T_NSAC, T_NSAS, T_KVC, T_KVS, T_QC, T_QS, T_32C, T_32S, T_328C, T_328S, T_TOTAL) = (
    0, 256, 512, 768, 1024, 1152, 1280, 2304, 3328, 3456, 3584, 3840, 4096)

(GV_NMIX, GV_QA_A, GV_QA_B, GV_NSA_A, GV_NSA_B, GV_WIN_A, GV_WIN_B, GV_CQ, GV_CKV, GV_QC_A, GV_QC_B,
 GV_DKV_A, GV_DKV_B, GV_Q, GV_TOTAL) = (
    0, 1024, 1280, 1536, 1792, 2048, 2176, 2304, 2560, 2816, 3072, 3328, 3456, 3584, 4608)


def _swap_idx(base, d, n):
    l = np.arange(n * d)
    return base + (l // d) * d + ((l % d) + d // 2) % d


def _win_cols():
    cols = np.full((Z_TOTAL,), -1, np.int64)

    def put(off, idx, at=0):
        cols[off + at: off + at + len(idx)] = idx

    put(Z_QA, C_QA + np.arange(256))
    put(Z_QAS, _swap_idx(C_QA, 64, 4))
    put(Z_NSA, C_KC + np.arange(256))
    put(Z_NSAS, _swap_idx(C_KS, 64, 1), at=128)
    put(Z_WIN, C_KW + np.arange(128))
    put(Z_WINS, _swap_idx(C_KW, 64, 1))
    put(Z_MISC, C_GA + np.arange(12))
    put(Z_MISC, C_WI + np.arange(8), at=12)
    put(Z_CQ, C_CQ + np.arange(256))
    put(Z_CKV, C_CKV + np.arange(256))
    put(Z_KPE, C_KPE + np.arange(32))
    put(Z_KPES, _swap_idx(C_KPE, 32, 1))
    put(Z_QC, C_QC + np.arange(256))
    put(Z_QCS, _swap_idx(C_QC, 64, 4))
    put(Z_DKV, C_KD + np.arange(128))
    put(Z_DKVS, _swap_idx(C_KD, 64, 1))
    put(Z_KI, C_KI + np.arange(32))
    put(Z_KIS, _swap_idx(C_KI, 32, 1))
    put(Z_QI, C_QI + np.arange(256))
    put(Z_QIS, _swap_idx(C_QI, 32, 8))
    return cols


def _wuq_cols():
    cols = np.full((2 * MLA_HEADS * LANE,), -1, np.int64)
    for h in range(MLA_HEADS):
        cols[h * LANE: h * LANE + MLA_QK] = h * MLA_QK + np.arange(MLA_QK)
        cols[1024 + h * LANE + MLA_NOPE: 1024 + h * LANE + MLA_QK] = _swap_idx(h * MLA_QK + MLA_NOPE, MLA_ROPE, 1)
    return cols


def _take_cols(w, cols):
    g = jnp.take(w, jnp.asarray(np.maximum(cols, 0)), axis=1)
    return jnp.where(jnp.asarray(cols >= 0)[None, :], g, 0.0)


def _blockdiag(n, d):
    i = np.arange(n)
    return (i[:, None] // d == i[None, :] // d).astype(np.float32)


def _placement(n_in, n_out, pairs):
    m = np.zeros((n_in, n_out), np.float32)
    for s, d in pairs:
        m[s, d] = 1.0
    return m


def _q_placement(n_heads, d, width, at):
    pairs = [(h * d + j, h * width + at + j) for h in range(n_heads) for j in range(d)]
    return _placement(n_heads * d, n_heads * width, pairs)


def _rope_tables(pos):
    pos = pos.astype(F32)[:, None]
    P = pos.shape[0]

    def cs(d):
        half = d // 2
        inv = ROPE_THETA ** (-jnp.arange(half, dtype=F32) / half)
        ang = pos * inv[None, :]
        c, s = jnp.cos(ang), jnp.sin(ang)
        return jnp.concatenate([c, c], axis=1), jnp.concatenate([-s, s], axis=1)

    c64, s64 = cs(64)
    c32, s32 = cs(32)
    one = lambda n: jnp.ones((P, n), F32)
    zero = lambda n: jnp.zeros((P, n), F32)
    segs = [
        jnp.tile(c64, (1, 4)), jnp.tile(s64, (1, 4)),
        jnp.concatenate([one(128), c64, one(64)], 1), jnp.concatenate([zero(128), s64, zero(64)], 1),
        jnp.concatenate([c64, one(64)], 1), jnp.concatenate([s64, zero(64)], 1),
        jnp.tile(jnp.concatenate([one(64), c32, one(32)], 1), (1, 8)),
        jnp.tile(jnp.concatenate([zero(64), s32, zero(32)], 1), (1, 8)),
        jnp.concatenate([c32, one(96)], 1), jnp.concatenate([s32, zero(96)], 1),
        jnp.tile(c32, (1, 8)), jnp.tile(s32, (1, 8)),
    ]
    return jnp.concatenate(segs, axis=1)


def _cmp_tables(nbp):
    pos = (jnp.arange(nbp) * NSA_BLK + (NSA_BLK - 1)).astype(F32)[:, None]
    inv = ROPE_THETA ** (-jnp.arange(32, dtype=F32) / 32)
    ang = pos * inv[None, :]
    c, s = jnp.cos(ang), jnp.sin(ang)
    one = jnp.ones((nbp, 192), F32)
    zero = jnp.zeros((nbp, 192), F32)
    return jnp.concatenate([c, c, one, -s, s, zero], axis=1)


def _swap64(g):
    return jnp.concatenate([g[32:], g[:32]])


def _gain_vector(norm_mix, nsa_qn, nsa_kn, mla_cqn, mla_ckvn, dsa_qn, dsa_kn, mla_qn):
    o64, z64 = jnp.ones((64,), F32), jnp.zeros((64,), F32)
    qpad = jnp.concatenate([mla_qn, jnp.zeros((32,), F32)])
    segs = [
        norm_mix,
        jnp.tile(nsa_qn, 4), jnp.tile(_swap64(nsa_qn), 4),
        jnp.concatenate([nsa_kn[0], o64, nsa_kn[1], o64]), jnp.concatenate([z64, z64, _swap64(nsa_kn[1]), z64]),
        jnp.concatenate([nsa_kn[2], o64]), jnp.concatenate([_swap64(nsa_kn[2]), z64]),
        mla_cqn, mla_ckvn,
        jnp.tile(dsa_qn, 4), jnp.tile(_swap64(dsa_qn), 4),
        jnp.concatenate([dsa_kn, o64]), jnp.concatenate([_swap64(dsa_kn), z64]),
        jnp.tile(qpad, 8),
    ]
    return jnp.concatenate(segs)[None, :]


def _mm(a, b):
    return jnp.dot(a.astype(MXU_DTYPE), b.astype(MXU_DTYPE), preferred_element_type=F32)


def _mm_nt(a, b):
    return lax.dot_general(a.astype(MXU_DTYPE), b.astype(MXU_DTYPE), (((1,), (1,)), ((), ())),
                           preferred_element_type=F32)


def _split3(x):
    x1 = x.astype(jnp.bfloat16)
    r1 = x - x1.astype(F32)
    x2 = r1.astype(jnp.bfloat16)
    x3 = (r1 - x2.astype(F32)).astype(jnp.bfloat16)
    return x1, x2, x3


def _place(x, p):
    x1, x2, x3 = _split3(x)
    d = lambda a: jnp.dot(a, p, preferred_element_type=F32)
    return d(x1) + d(x2) + d(x3)


def _gsum(x2, g):
    hi = x2.astype(jnp.bfloat16)
    lo = (x2 - hi.astype(F32)).astype(jnp.bfloat16)
    return jnp.dot(hi, g, preferred_element_type=F32) + jnp.dot(lo, g, preferred_element_type=F32)


def _iota(shape, dim):
    return lax.broadcasted_iota(I32, shape, dim)


def _stack_heads(x, n, w):
    return jnp.concatenate([x[:, h * w:(h + 1) * w] for h in range(n)], axis=0)


def _softmax_rows(s, mask):
    s = jnp.where(mask, s, NEG)
    m = jnp.max(s, axis=-1, keepdims=True)
    e = jnp.where(mask, jnp.exp(s - m), 0.0)
    l = jnp.sum(e, axis=-1, keepdims=True)
    return e / jnp.where(l > 0.0, l, 1.0)


def _online_update(carry, s, mask, v, vt=False):
    m, l, acc = carry
    if mask is not None:
        s = jnp.where(mask, s, NEG)
    m_new = jnp.maximum(m, jnp.max(s, axis=-1, keepdims=True))
    alpha = jnp.exp(m - m_new)
    p = jnp.exp(s - m_new)
    if mask is not None:
        p = jnp.where(mask, p, 0.0)
    l = alpha * l + jnp.sum(p, axis=-1, keepdims=True)
    acc = alpha * acc + (_mm_nt(p, v) if vt else _mm(p, v))
    return m_new, l, acc


def _flash_init(rows, width):
    return (jnp.full((rows, 1), MASK_FLOOR, F32), jnp.zeros((rows, 1), F32), jnp.zeros((rows, width), F32))


def _finish(l, acc):
    return acc / jnp.where(l > 0.0, l, 1.0)


def _sortable(x):
    b = lax.bitcast_convert_type(x, I32)
    return jnp.where(b < 0, b ^ jnp.int32(0x7FFFFFFF), b)


def _cparams(sem, vmem_mb):
    return pltpu.CompilerParams(dimension_semantics=sem, vmem_limit_bytes=vmem_mb * 1024 * 1024)


def _proj_kernel(x_ref, tab_ref, gv_ref, w_ref, wuq_ref, g256_ref, g1024_ref,
                 qa_o, qm_o, qc_o, qi_o, misc_o, rn_o, rm_o, rd_o, rw_o, nk_o, wk_o, dk_o, kp_o):
    gv = lambda off, w: gv_ref[:, off:off + w]
    tab = lambda off, w: tab_ref[:, off:off + w]
    x = x_ref[...]
    xn = x * lax.rsqrt(jnp.mean(x * x, axis=-1, keepdims=True) + EPS) * gv(GV_NMIX, D_MODEL)
    z = _mm(xn, w_ref[...])
    zs = lambda off, w: z[:, off:off + w]
    g256 = g256_ref[...]
    g128 = g256_ref[0:128, 0:128]

    def group(zo, zso, w, gmat, d, ga, gb, tc, ts, normmask=None):
        a, asw = zs(zo, w), zs(zso, w)
        if gmat is None:
            rs = None
        else:
            rs = lax.rsqrt(_gsum(a * a, gmat) / d + EPS)
            if normmask is not None:
                rs = jnp.where(normmask, rs, 1.0)
        ca, cb = tab(tc, w), tab(ts, w)
        if ga is not None:
            ca, cb = gv(ga, w) * ca, gv(gb, w) * cb
        if rs is not None:
            ca, cb = rs * ca, rs * cb
        return a * ca + asw * cb

    qa_o[...] = group(Z_QA, Z_QAS, 256, g256, 64.0, GV_QA_A, GV_QA_B, T_64C, T_64S).astype(qa_o.dtype)
    lane = _iota((1, 256), 1)
    nm = (lane < 64) | ((lane >= 128) & (lane < 192))
    rn = group(Z_NSA, Z_NSAS, 256, g256, 64.0, GV_NSA_A, GV_NSA_B, T_NSAC, T_NSAS, nm)
    rn_o[...] = rn
    nk_o[...] = rn.astype(nk_o.dtype)
    nm128 = _iota((1, 128), 1) < 64
    rw = group(Z_WIN, Z_WINS, 128, g128, 64.0, GV_WIN_A, GV_WIN_B, T_KVC, T_KVS, nm128)
    rw_o[...] = rw
    wk_o[...] = rw.astype(wk_o.dtype)
    zm = zs(Z_MISC, 128)
    misc_o[...] = jnp.where(_iota((1, 128), 1) < 12, jax.nn.sigmoid(zm), zm * (IDX_HEADS ** -0.5))
    cq = zs(Z_CQ, 256)
    cqn = cq * lax.rsqrt(jnp.mean(cq * cq, axis=-1, keepdims=True) + EPS) * gv(GV_CQ, 256)
    q2 = _mm(cqn, wuq_ref[...])
    qr = q2[:, 0:1024] * tab(T_QC, 1024) + q2[:, 1024:2048] * tab(T_QS, 1024)
    rs = lax.rsqrt(_gsum(qr * qr, g1024_ref[...]) / float(MLA_QK) + EPS)
    qm_o[...] = (qr * rs * gv(GV_Q, 1024)).astype(qm_o.dtype)
    ckv = zs(Z_CKV, 256)
    rm_o[:, 0:256] = ckv * lax.rsqrt(jnp.mean(ckv * ckv, axis=-1, keepdims=True) + EPS) * gv(GV_CKV, 256)
    kp = group(Z_KPE, Z_KPES, 128, None, 0.0, None, None, T_32C, T_32S)
    rm_o[:, 256:288] = kp[:, 0:32]
    kp_o[...] = kp
    qc_o[...] = group(Z_QC, Z_QCS, 256, g256, 64.0, GV_QC_A, GV_QC_B, T_64C, T_64S).astype(qc_o.dtype)
    dkv = group(Z_DKV, Z_DKVS, 128, g128, 64.0, GV_DKV_A, GV_DKV_B, T_KVC, T_KVS, nm128)
    ki = group(Z_KI, Z_KIS, 128, None, 0.0, None, None, T_32C, T_32S)
    rd_o[:, 0:128] = dkv
    rd_o[:, 128:160] = ki[:, 0:32]
    dk_o[:, 0:128] = dkv.astype(dk_o.dtype)
    dk_o[:, 128:256] = ki.astype(dk_o.dtype)
    qi_o[...] = group(Z_QI, Z_QIS, 256, None, 0.0, None, None, T_328C, T_328S).astype(qi_o.dtype)


def _proj(x, tab, n_pos_tiles, gv, wbig, wuq, g256, g1024, tm):
    n = x.shape[0]
    row = lambda w: pl.BlockSpec((tm, w), lambda i: (i, 0))
    full = lambda a: pl.BlockSpec(a.shape, lambda i: (0, 0))
    widths = [(256, MXU_DTYPE), (1024, MXU_DTYPE), (256, MXU_DTYPE), (256, MXU_DTYPE), (128, F32),
              (256, F32), (288, F32), (160, F32), (128, F32),
              (256, MXU_DTYPE), (128, MXU_DTYPE), (256, MXU_DTYPE), (128, F32)]
    return pl.pallas_call(
        _proj_kernel,
        grid=(n // tm,),
        in_specs=[row(D_MODEL),
                  pl.BlockSpec((tm, T_TOTAL), lambda i: (i % n_pos_tiles, 0)),
                  full(gv), full(wbig), full(wuq), full(g256), full(g1024)],
        out_specs=[row(w) for w, _ in widths],
        out_shape=[jax.ShapeDtypeStruct((n, w), dt) for w, dt in widths],
        compiler_params=_cparams(("parallel",), 56),
        name="proj",
    )(x, tab, gv, wbig, wuq, g256, g1024)


def _mla_keys(ckv, kpe128, wk, wv, pk, g8, gk):
    kraw = _mm(ckv, wk) + _place(kpe128, pk)
    ss = _gsum(kraw * kraw, g8)
    rs = lax.rsqrt(ss / float(MLA_QK) + EPS)
    parts = [kraw[:, h * LANE:(h + 1) * LANE] * rs[:, h:h + 1] for h in range(MLA_HEADS)]
    km = jnp.concatenate(parts, axis=1) * gk
    return km.astype(MXU_DTYPE), _mm(ckv, wv).astype(MXU_DTYPE)


def _mla_prep_kernel(rm_ref, kp_ref, wk_ref, wv_ref, pk_ref, g8_ref, gk_ref, km_o, vm_o):
    km, vm = _mla_keys(rm_ref[:, 0:256], kp_ref[...], wk_ref[...], wv_ref[...], pk_ref[...],
                       g8_ref[...], gk_ref[...])
    km_o[...] = km
    vm_o[...] = vm


def _mla_prep(rm, kp, wk, wv, pk, g8, gk, tk):
    n = rm.shape[0]
    full = lambda a: pl.BlockSpec(a.shape, lambda i: (0, 0))
    return pl.pallas_call(
        _mla_prep_kernel,
        grid=(n // tk,),
        in_specs=[pl.BlockSpec((tk, 288), lambda i: (i, 0)), pl.BlockSpec((tk, 128), lambda i: (i, 0)),
                  full(wk), full(wv), full(pk), full(g8), full(gk)],
        out_specs=[pl.BlockSpec((tk, 1024), lambda i: (i, 0)), pl.BlockSpec((tk, 512), lambda i: (i, 0))],
        out_shape=[jax.ShapeDtypeStruct((n, 1024), MXU_DTYPE), jax.ShapeDtypeStruct((n, 512), MXU_DTYPE)],
        compiler_params=_cparams(("parallel",), 40),
        name="mla_prep",
    )(rm, kp, wk, wv, pk, g8, gk)


def _mla_flash_kernel(q_ref, k_ref, v_ref, o_ref, *, tq, tk):
    i = pl.program_id(2)
    scale = MLA_QK ** -0.5
    q_pos = i * tq + _iota((tq, 1), 0)
    qs = [q_ref[:, hh * LANE:(hh + 1) * LANE] for hh in range(2)]

    def chunk(c, carry, masked):
        k0 = pl.multiple_of(c * tk, tk)
        v = v_ref[pl.ds(k0, tk), :]
        mask = ((k0 + _iota((1, tk), 1)) <= q_pos) if masked else None
        out = []
        for hh in range(2):
            k = k_ref[pl.ds(k0, tk), hh * LANE:(hh + 1) * LANE]
            out.append(_online_update(carry[hh], _mm_nt(qs[hh], k) * scale, mask, v))
        return tuple(out)

    one = (jnp.full((tq, 1), NEG, F32), jnp.zeros((tq, 1), F32), jnp.zeros((tq, LANE), F32))
    n_full = (i * tq) // tk
    n_all = ((i + 1) * tq + tk - 1) // tk
    carry = lax.fori_loop(0, n_full, lambda c, cr: chunk(c, cr, False), (one, one))
    carry = lax.fori_loop(n_full, n_all, lambda c, cr: chunk(c, cr, True), carry)
    outs = [_finish(cr[1], cr[2]) for cr in carry]
    lane = _iota((1, LANE), 1)
    o_ref[...] = jnp.where(lane < MLA_V, outs[0], outs[1])


def _mla_flash(qm, km, vm, nb, t, tq, tk):
    nq = t // tq
    kern = functools.partial(_mla_flash_kernel, tq=tq, tk=tk)
    return pl.pallas_call(
        kern,
        grid=(nb, MLA_HEADS // 2, nq),
        in_specs=[pl.BlockSpec((tq, 256), lambda b, h, i: (b * nq + i, h)),
                  pl.BlockSpec((t, 256), lambda b, h, i: (b, h)),
                  pl.BlockSpec((t, 128), lambda b, h, i: (b, h))],
        out_specs=pl.BlockSpec((tq, 128), lambda b, h, i: (b * nq + i, h)),
        out_shape=jax.ShapeDtypeStruct((nb * t, 512), F32),
        compiler_params=_cparams(("parallel", "parallel", "arbitrary"), 48),
        name="mla_flash",
    )(qm, km, vm)


def _mla_sample_kernel(pt_ref, *refs, pps, past):
    pages = refs[:pps]
    (q_ref, newm_ref, newkp_ref, wkvt_ref, gkc_ref, wk_ref, wv_ref, pk_ref, g8_ref, gk_ref, o_ref,
     m_s, l_s, acc_s) = refs[pps:]
    j = pl.program_id(1)
    nsteps = pl.num_programs(1)
    rows = SAMPLE_T * MLA_HEADS
    q = q_ref[0].astype(F32)
    head_of_lane = _iota((SUB, 1024), 1) // LANE
    sub = _iota((SUB, 1024), 0)
    qbd = jnp.concatenate(
        [jnp.where(head_of_lane == sub, jnp.broadcast_to(q[t:t + 1, :], (SUB, 1024)), 0.0)
         for t in range(SAMPLE_T)], axis=0).astype(MXU_DTYPE)
    q_pos = past + _iota((rows, 1), 0) // MLA_HEADS
    scale = MLA_QK ** -0.5

    @pl.when(j == 0)
    def _():
        m_s[...] = jnp.full(m_s.shape, NEG, F32)
        l_s[...] = jnp.zeros(l_s.shape, F32)
        acc_s[...] = jnp.zeros(acc_s.shape, F32)

    def update(s, mask, v, vt):
        m, l, acc = _online_update((m_s[...], l_s[...], acc_s[...]), s, mask, v, vt)
        m_s[...] = m
        l_s[...] = l
        acc_s[...] = acc

    @pl.when(j < nsteps - 1)
    def _():
        n = pps * PAGE
        xt = jnp.concatenate([pages[p][...] for p in range(pps)], axis=1)
        kpet = xt[256:288]
        kv = jnp.dot(wkvt_ref[...], xt[0:256].astype(MXU_DTYPE), preferred_element_type=F32)
        pe2 = jnp.sum(kpet * kpet, axis=0, keepdims=True)
        parts = []
        for h in range(MLA_HEADS):
            kn = kv[h * MLA_NOPE:(h + 1) * MLA_NOPE]
            rs = lax.rsqrt((jnp.sum(kn * kn, axis=0, keepdims=True) + pe2) / float(MLA_QK) + EPS)
            parts += [(kn * rs) * gkc_ref[0:MLA_NOPE, :], (kpet * rs) * gkc_ref[MLA_NOPE:MLA_QK, :],
                      jnp.zeros((LANE - MLA_QK, n), F32)]
        kmt = jnp.concatenate(parts, axis=0).astype(MXU_DTYPE)
        s = jnp.dot(qbd, kmt, preferred_element_type=F32) * scale
        update(s, None, kv[512:1024].astype(MXU_DTYPE), True)

    @pl.when(j == nsteps - 1)
    def _():
        ckv = jnp.concatenate([newm_ref[0][:, 0:256], jnp.zeros((PAGE - SAMPLE_T, 256), F32)], axis=0)
        kpe = jnp.concatenate([newkp_ref[0], jnp.zeros((PAGE - SAMPLE_T, 128), F32)], axis=0)
        km, vm = _mla_keys(ckv, kpe, wk_ref[...], wv_ref[...], pk_ref[...], g8_ref[...], gk_ref[...])
        update(_mm_nt(qbd, km) * scale, (past + _iota((1, PAGE), 1)) <= q_pos, vm, False)
        o = _finish(l_s[...], acc_s[...])
        lane_head = _iota((SUB, 512), 1) // MLA_V
        sub8 = _iota((SUB, 512), 0)
        out = jnp.zeros((SAMPLE_T, 512), F32)
        for t in range(SAMPLE_T):
            blk = jnp.where(lane_head == sub8, o[t * SUB:(t + 1) * SUB, :], 0.0)
            r = jnp.sum(blk, axis=0, keepdims=True)
            out = jnp.where(_iota((SAMPLE_T, 512), 0) == t, jnp.broadcast_to(r, (SAMPLE_T, 512)), out)
        o_ref[0] = out


def _mla_sample(page_table, cache_t, layer, qm, newm, newkp, wkvt, gkc, wk, wv, pk, g8, gk, pps):
    db, n_pages = page_table.shape
    past = n_pages * PAGE
    nsteps = n_pages // pps + 1
    last = n_pages - 1

    def page_spec(p):
        return pl.BlockSpec((None, None, 288, PAGE),
                            lambda b, j, pt: (layer, pt[b, jnp.minimum(j * pps + p, last)], 0, 0))

    full = lambda a: pl.BlockSpec(a.shape, lambda b, j, pt: (0, 0))
    per_b = lambda w: pl.BlockSpec((1, SAMPLE_T, w), lambda b, j, pt: (b, 0, 0))
    rows = SAMPLE_T * MLA_HEADS
    kern = functools.partial(_mla_sample_kernel, pps=pps, past=past)
    return pl.pallas_call(
        kern,
        grid_spec=pltpu.PrefetchScalarGridSpec(
            num_scalar_prefetch=1,
            grid=(db, nsteps),
            in_specs=[page_spec(p) for p in range(pps)] + [per_b(1024), per_b(288), per_b(128), full(wkvt), full(gkc),
                                                          full(wk), full(wv), full(pk), full(g8), full(gk)],
            out_specs=per_b(512),
            scratch_shapes=[pltpu.VMEM((rows, 1), F32), pltpu.VMEM((rows, 1), F32), pltpu.VMEM((rows, 512), F32)],
        ),
        out_shape=jax.ShapeDtypeStruct((db, SAMPLE_T, 512), F32),
        compiler_params=_cparams(("parallel", "arbitrary"), 40),
        name="mla_sample",
    )(page_table, *([cache_t] * pps), qm, newm, newkp, wkvt, gkc, wk, wv, pk, g8, gk)


def _cmp_blocks(rows, wpos):
    n = rows.shape[0] // NSA_BLK
    x = rows.reshape(n, NSA_BLK, 256) * wpos[None]
    return jnp.sum(x, axis=1) / float(NSA_BLK)


def _cmp_rope(craw, ctab, swc):
    return craw * ctab[:, 0:256] + _place(craw, swc) * ctab[:, 256:512]


def _nsa_cmp_kernel(rn_ref, wpos_ref, ctab_ref, swc_ref, o_ref):
    craw = _cmp_blocks(rn_ref[...], wpos_ref[...])
    o_ref[...] = _cmp_rope(craw, ctab_ref[...], swc_ref[...]).astype(o_ref.dtype)


def _nsa_cmp(rn, wpos, ctab, swc, nb, t):
    nblk = t // NSA_BLK
    tb = min(nblk, 16)
    nt = nblk // tb
    return pl.pallas_call(
        _nsa_cmp_kernel,
        grid=(nb, nt),
        in_specs=[pl.BlockSpec((tb * NSA_BLK, 256), lambda b, i: (b * nt + i, 0)),
                  pl.BlockSpec(wpos.shape, lambda b, i: (0, 0)),
                  pl.BlockSpec((tb, 512), lambda b, i: (i, 0)),
                  pl.BlockSpec(swc.shape, lambda b, i: (0, 0))],
        out_specs=pl.BlockSpec((tb, 256), lambda b, i: (b * nt + i, 0)),
        out_shape=jax.ShapeDtypeStruct((nb * nblk, 256), MXU_DTYPE),
        compiler_params=_cparams(("parallel", "parallel"), 32),
        name="nsa_cmp",
    )(rn, wpos, ctab, swc)


def _top_n_mask(x, n_sel):
    t, w = x.shape
    lane = _iota((t, w), 1).astype(F32)

    def body(_, carry):
        x, sel = carry
        m = jnp.max(x, axis=-1, keepdims=True)
        first = jnp.min(jnp.where(x == m, lane, float(w)), axis=-1, keepdims=True)
        hit = lane == first
        return jnp.where(hit, -jnp.inf, x), jnp.where(hit, 1.0, sel)

    _, sel = lax.fori_loop(0, n_sel, body, (x, jnp.zeros((t, w), F32)))
    return sel


def _nsa_attend(qa, misc, q_pos, cmp, nblk, key_ref, n_chunks, tkc, win, w_pos, pqc, pqs, pqw, expand_ref):
    tq = qa.shape[0]
    h = NSA_HEADS
    scale = D_HEAD ** -0.5
    nbp = cmp.shape[0]
    qp4 = jnp.concatenate([q_pos] * h, axis=0)
    qc_st = _stack_heads(_mm(qa, pqc), h, 256)
    blk = _iota((1, nbp), 1)
    s = _mm_nt(qc_st, cmp) * scale
    p_c = _softmax_rows(s, ((blk * NSA_BLK + (NSA_BLK - 1)) <= qp4) & (blk < nblk))
    o_c = _mm(p_c, cmp)
    imp = p_c[0:tq]
    for hh in range(1, h):
        imp = imp + p_c[hh * tq:(hh + 1) * tq]
    cur = q_pos // NSA_BLK
    imp = jnp.where(blk == cur, NSA_FORCE, jnp.where(blk < cur, imp, -1.0))
    imp = jnp.where(blk < nblk, imp, -jnp.inf)
    sel = _top_n_mask(imp, min(NSA_TOPN, nblk))
    qs_st = _stack_heads(_mm(qa, pqs), h, 256) * scale
    def chunk(c, carry):
        k0 = c * tkc if isinstance(c, int) else pl.multiple_of(c * tkc, tkc)
        keys = key_ref[pl.ds(k0, tkc), :]
        tok = _mm(sel, expand_ref[c])
        bias1 = jnp.where((tok > 0.5) & ((k0 + _iota((1, tkc), 1)) <= q_pos), 0.0, NEG)
        s = _mm_nt(qs_st, keys) + jnp.concatenate([bias1] * h, axis=0)
        return _online_update(carry, s, None, keys)

    _, l, acc = _loop(n_chunks, chunk, _flash_init(h * tq, 256))
    o_s = _finish(l, acc)
    qw_st = _stack_heads(_mm(qa, pqw), h, 128)
    rel = qp4 - w_pos
    s = _mm_nt(qw_st, win) * scale
    p_w = _softmax_rows(s, (rel >= 0) & (rel < NSA_WINDOW) & (w_pos >= 0))
    o_w = _mm(p_w, win)
    gate = lambda jj: jnp.concatenate([misc[:, 3 * hh + jj:3 * hh + jj + 1] for hh in range(h)], axis=0)
    tsum = gate(0) * o_c[:, 0:128] + gate(1) * o_s[:, 128:256] + gate(2) * o_w
    tsum = jnp.where(_iota((1, 128), 1) >= D_HEAD, tsum, 0.0)
    return jnp.concatenate([tsum[hh * tq:(hh + 1) * tq] for hh in range(h)], axis=1)


def _nsa_prompt_kernel(qa_ref, misc_ref, cmp_ref, nk_ref, wk_ref, pqc_ref, pqs_ref, pqw_ref, exp_ref, o_ref,
                       *, tq, tkc, t):
    i = pl.program_id(1)
    s0 = i * tq
    q_pos = s0 + _iota((tq, 1), 0)
    w = NSA_WINDOW + tq
    kstart = pl.multiple_of(jnp.clip(s0 - NSA_WINDOW, 0, t - w), SUB)
    win = wk_ref[pl.ds(kstart, w), :]
    w_pos = kstart + _iota((1, w), 1)
    n_chunks = (s0 + tq + tkc - 1) // tkc
    o_ref[...] = _nsa_attend(qa_ref[...], misc_ref[...], q_pos, cmp_ref[...], t // NSA_BLK, nk_ref, n_chunks,
                             tkc, win, w_pos, pqc_ref[...], pqs_ref[...], pqw_ref[...], exp_ref)


def _nsa_prompt(qa, misc, cmp, nk, wk, pqc, pqs, pqw, expand, nb, t, tq, tkc):
    nq = t // tq
    nblk = t // NSA_BLK
    kern = functools.partial(_nsa_prompt_kernel, tq=tq, tkc=tkc, t=t)
    full = lambda a: pl.BlockSpec(a.shape, lambda b, i: (0, 0))
    return pl.pallas_call(
        kern,
        grid=(nb, nq),
        in_specs=[pl.BlockSpec((tq, 256), lambda b, i: (b * nq + i, 0)),
                  pl.BlockSpec((tq, 128), lambda b, i: (b * nq + i, 0)),
                  pl.BlockSpec((nblk, 256), lambda b, i: (b, 0)),
                  pl.BlockSpec((t, 256), lambda b, i: (b, 0)),
                  pl.BlockSpec((t, 128), lambda b, i: (b, 0)),
                  full(pqc), full(pqs), full(pqw), pl.BlockSpec(expand.shape, lambda b, i: (0, 0, 0))],
        out_specs=pl.BlockSpec((tq, 512), lambda b, i: (b * nq + i, 0)),
        out_shape=jax.ShapeDtypeStruct((nb * t, 512), F32),
        compiler_params=_cparams(("parallel", "arbitrary"), 48),
        name="nsa_prompt",
    )(qa, misc, cmp, nk, wk, pqc, pqs, pqw, expand)


def _nsa_sample_kernel(pt_ref, *refs, pps, past):
    pages = refs[:pps]
    (qa_ref, misc_ref, newn_ref, sw_ref, neww_ref, wpos_ref, ctab_ref, swc_ref, pqc_ref, pqs_ref, pqw_ref, exp_ref,
     o_ref, key_s, cmp_s, win_s) = refs[pps:]
    j = pl.program_id(1)
    nsteps = pl.num_programs(1)
    nblk = past // NSA_BLK + 1
    bps = pps * PAGE // NSA_BLK
    lpad = key_s.shape[0]

    @pl.when(j == 0)
    def _():
        key_s[past:lpad, :] = jnp.zeros((lpad - past, 256), key_s.dtype)
        cmp_s[...] = jnp.zeros(cmp_s.shape, F32)

    @pl.when(j < nsteps - 1)
    def _():
        rows = jnp.concatenate([pages[p][...] for p in range(pps)], axis=0)
        r0 = pl.multiple_of(j * pps * PAGE, pps * PAGE)
        key_s[pl.ds(r0, pps * PAGE), :] = rows.astype(key_s.dtype)
        b0 = pl.multiple_of(j * bps, bps)
        cmp_s[pl.ds(b0, bps), :] = _cmp_blocks(rows, wpos_ref[...])

    @pl.when(j == nsteps - 1)
    def _():
        newn = newn_ref[0]
        tail = jnp.concatenate([newn, jnp.zeros((NSA_BLK - SAMPLE_T, 256), F32)], axis=0)
        key_s[past:past + NSA_BLK, :] = tail.astype(key_s.dtype)
        cmp_s[past // NSA_BLK: past // NSA_BLK + 1, :] = _cmp_blocks(tail, wpos_ref[...])
        cmp = _cmp_rope(cmp_s[...], ctab_ref[...], swc_ref[...]).astype(MXU_DTYPE)
        win_s[0:NSA_WINDOW, :] = sw_ref[0].astype(win_s.dtype)
        win_s[NSA_WINDOW:NSA_WINDOW + 2 * SAMPLE_T, :] = jnp.concatenate(
            [neww_ref[0], jnp.zeros((SAMPLE_T, 128), F32)], axis=0).astype(win_s.dtype)
        w = NSA_WINDOW + 2 * SAMPLE_T
        w_pos = past - NSA_WINDOW + _iota((1, w), 1)
        q_pos = past + _iota((SAMPLE_T, 1), 0)
        o_ref[0] = _nsa_attend(qa_ref[0], misc_ref[0], q_pos, cmp, nblk, key_s, 1, lpad, win_s[...], w_pos,
                               pqc_ref[...], pqs_ref[...], pqw_ref[...], exp_ref)


def _nsa_sample(page_table, cache, layer, qa, misc, newn, state_win, neww, wpos, ctab, swc, pqc, pqs, pqw,
                expand, pps):
    db, n_pages = page_table.shape
    past = n_pages * PAGE
    nsteps = n_pages // pps + 1
    last = n_pages - 1
    nbp = ctab.shape[0]
    lpad = expand.shape[2]

    def page_spec(p):
        return pl.BlockSpec((None, None, PAGE, 256),
                            lambda b, j, pt: (layer, pt[b, jnp.minimum(j * pps + p, last)], 0, 0))

    full = lambda a: pl.BlockSpec(a.shape, lambda b, j, pt: (0, 0))
    per_b = lambda w: pl.BlockSpec((1, SAMPLE_T, w), lambda b, j, pt: (b, 0, 0))
    kern = functools.partial(_nsa_sample_kernel, pps=pps, past=past)
    return pl.pallas_call(
        kern,
        grid_spec=pltpu.PrefetchScalarGridSpec(
            num_scalar_prefetch=1,
            grid=(db, nsteps),
            in_specs=[page_spec(p) for p in range(pps)] + [
                per_b(256), per_b(128), per_b(256),
                pl.BlockSpec((None, 1, NSA_WINDOW, 128), lambda b, j, pt: (layer, b, 0, 0)),
                per_b(128), full(wpos), full(ctab), full(swc), full(pqc), full(pqs), full(pqw),
                pl.BlockSpec(expand.shape, lambda b, j, pt: (0, 0, 0))],
            out_specs=per_b(512),
            scratch_shapes=[pltpu.VMEM((lpad, 256), MXU_DTYPE), pltpu.VMEM((nbp, 256), F32),
                            pltpu.VMEM((NSA_WINDOW + 2 * SAMPLE_T, 128), MXU_DTYPE)],
        ),
        out_shape=jax.ShapeDtypeStruct((db, SAMPLE_T, 512), F32),
        compiler_params=_cparams(("parallel", "arbitrary"), 48),
        name="nsa_sample",
    )(page_table, *([cache] * pps), qa, misc, newn, state_win, neww, wpos, ctab, swc, pqc, pqs, pqw, expand)


def _loop(n, body, init):
    if isinstance(n, int):
        for c in range(n):
            init = body(c, init)
        return init
    return lax.fori_loop(0, n, body, init)


def _dsa_attend(qc, qi, misc, q_pos, key_ref, n_chunks, tkc, n_top, pqd, pqi, tri, key_s, kt=False):
    tq = qc.shape[0]
    h = DSA_HEADS
    qi_st = _stack_heads(_mm(qi, pqi), IDX_HEADS, 256)
    wcol = [misc[:, 12 + g:13 + g] for g in range(IDX_HEADS)]
    qk = _mm if kt else _mm_nt

    def get_keys(c):
        if kt:
            return key_ref[c]
        return key_ref[pl.ds(c * tkc if isinstance(c, int) else pl.multiple_of(c * tkc, tkc), tkc), :]

    def score_chunk(c, _):
        k0 = c * tkc
        rel = jnp.maximum(qk(qi_st, get_keys(c)), 0.0)
        sc = wcol[0] * rel[0:tq]
        for g in range(1, IDX_HEADS):
            sc = sc + wcol[g] * rel[g * tq:(g + 1) * tq]
        sc = jnp.where(sc == 0.0, 0.0, sc)
        sc = jnp.where((k0 + _iota((1, tkc), 1)) <= q_pos, sc, NEG)
        key_s[c] = _sortable(sc)
        return 0

    _loop(n_chunks, score_chunk, 0)

    def count(pred_fn):
        def body(c, acc):
            hit = jnp.where(pred_fn(key_s[c]), 1.0, 0.0)
            for u in range(tkc // LANE):
                acc = acc + hit[:, u * LANE:(u + 1) * LANE]
            return acc
        acc = _loop(n_chunks, body, jnp.zeros((tq, LANE), F32))
        return jnp.sum(acc, axis=-1, keepdims=True)

    static = isinstance(n_chunks, int)
    if static:
        def digit_step(it, thr):
            d = jnp.left_shift(jnp.int32(1), jnp.int32(30) - 2 * it)
            for mult in (1, 2, 3):
                cnt = count(lambda k: k >= thr + mult * d)
                step = jnp.where(cnt >= float(n_top), d, 0)
                new = thr + step if mult == 1 else new + step
            return new

        thr = lax.fori_loop(0, 16, digit_step, jnp.full((tq, 1), -2 ** 31, I32))
    else:
        def bit_step(it, thr):
            cand = thr + jnp.left_shift(jnp.int32(1), jnp.int32(31) - it)
            cnt = count(lambda k: k >= cand)
            return jnp.where(cnt >= float(n_top), cand, thr)

        thr = lax.fori_loop(0, 32, bit_step, jnp.full((tq, 1), -2 ** 31, I32))
    need = float(n_top) - count(lambda k: k > thr)

    qd_st = _stack_heads(_mm(qc, pqd), h, 256) * (D_HEAD ** -0.5)

    def scores(c, eq_seen):
        kk = key_s[c]
        eq = jnp.where(kk == thr, 1.0, 0.0)
        rank = eq_seen + _mm(eq, tri)
        sel = (kk > thr) | ((kk == thr) & (rank <= need))
        bias1 = jnp.where(sel & ((c * tkc + _iota((1, tkc), 1)) <= q_pos), 0.0, NEG)
        s = qk(qd_st, get_keys(c)) + jnp.concatenate([bias1] * h, axis=0)
        return eq_seen + jnp.sum(eq, axis=-1, keepdims=True), s

    if static:
        eq_seen, parts = jnp.zeros((tq, 1), F32), []
        for c in range(n_chunks):
            eq_seen, s = scores(c, eq_seen)
            parts.append(s)
        s = jnp.concatenate(parts, axis=1)
        m = jnp.maximum(jnp.max(s, axis=-1, keepdims=True), MASK_FLOOR)
        p = jnp.exp(s - m)
        l = jnp.sum(p, axis=-1, keepdims=True)
        pv = _mm_nt if kt else _mm
        acc = pv(p[:, 0:tkc], get_keys(0))
        for c in range(1, n_chunks):
            acc = acc + pv(p[:, c * tkc:(c + 1) * tkc], get_keys(c))
    else:
        def chunk(c, carry):
            eq_seen, s = scores(c, carry[0])
            return eq_seen, _online_update(carry[1], s, None, get_keys(c), kt)

        _, (_, l, acc) = lax.fori_loop(0, n_chunks, chunk, (jnp.zeros((tq, 1), F32), _flash_init(h * tq, 256)))
    o = _finish(l, acc)[:, 0:128]
    o = jnp.where(_iota((1, 128), 1) >= D_HEAD, o, 0.0)
    return jnp.concatenate([o[hh * tq:(hh + 1) * tq] for hh in range(h)], axis=1)


def _dsa_prompt_kernel(qc_ref, qi_ref, misc_ref, dk_ref, pqd_ref, pqi_ref, tri_ref, o_ref, key_s,
                       *, tq, tkc, n_top):
    i = pl.program_id(1)
    s0 = i * tq
    q_pos = s0 + _iota((tq, 1), 0)
    n_chunks = (s0 + tq + tkc - 1) // tkc
    o_ref[...] = _dsa_attend(qc_ref[...], qi_ref[...], misc_ref[...], q_pos, dk_ref, n_chunks, tkc, n_top,
                             pqd_ref[...], pqi_ref[...], tri_ref[...], key_s)


def _dsa_prompt(qc, qi, misc, dk, pqd, pqi, tri, nb, t, tq, tkc):
    nq = t // tq
    n_top = min(DSA_TOPK_MAX, t // 4)
    kern = functools.partial(_dsa_prompt_kernel, tq=tq, tkc=tkc, n_top=n_top)
    full = lambda a: pl.BlockSpec(a.shape, lambda b, i: (0, 0))
    return pl.pallas_call(
        kern,
        grid=(nb, nq),
        in_specs=[pl.BlockSpec((tq, 256), lambda b, i: (b * nq + i, 0)),
                  pl.BlockSpec((tq, 256), lambda b, i: (b * nq + i, 0)),
                  pl.BlockSpec((tq, 128), lambda b, i: (b * nq + i, 0)),
                  pl.BlockSpec((t, 256), lambda b, i: (b, 0)),
                  full(pqd), full(pqi), full(tri)],
        out_specs=pl.BlockSpec((tq, 512), lambda b, i: (b * nq + i, 0)),
        out_shape=jax.ShapeDtypeStruct((nb * t, 512), F32),
        scratch_shapes=[pltpu.VMEM((t // tkc, tq, tkc), I32)],
        compiler_params=_cparams(("parallel", "arbitrary"), 48),
        name="dsa_prompt",
    )(qc, qi, misc, dk, pqd, pqi, tri)


def _dsa_sample_kernel(pt_ref, *refs, pps, past, tkc, n_top):
    pages = refs[:pps]
    (qc_ref, qi_ref, misc_ref, newd_ref, pqd_ref, pqi_ref, tri_ref, o_ref, key_t, key_s) = refs[pps:]
    j = pl.program_id(1)
    nsteps = pl.num_programs(1)
    ppc = tkc // PAGE
    n_chunks = past // tkc + 1

    @pl.when(j == 0)
    def _():
        key_t[:, 160:256, :] = jnp.zeros((n_chunks, 96, tkc), key_t.dtype)
        key_t[n_chunks - 1] = jnp.zeros((256, tkc), key_t.dtype)

    @pl.when(j < nsteps - 1)
    def _():
        for p in range(pps):
            c = j * (pps // ppc) + p // ppc
            key_t[c, 0:160, (p % ppc) * PAGE:(p % ppc + 1) * PAGE] = pages[p][...].astype(key_t.dtype)

    @pl.when(j == nsteps - 1)
    def _():
        new_rows = jnp.concatenate([newd_ref[0].astype(F32), jnp.zeros((PAGE - SAMPLE_T, 256), F32)], axis=0)
        key_t[n_chunks - 1, :, 0:PAGE] = new_rows.T.astype(key_t.dtype)
        q_pos = past + _iota((SAMPLE_T, 1), 0)
        o_ref[0] = _dsa_attend(qc_ref[0], qi_ref[0], misc_ref[0], q_pos, key_t, n_chunks, tkc, n_top,
                               pqd_ref[...], pqi_ref[...], tri_ref[...], key_s, kt=True)


def _dsa_sample(page_table, cache_t, layer, qc, qi, misc, newd, pqd, pqi, tri, pps, tkc, ts):
    db, n_pages = page_table.shape
    past = n_pages * PAGE
    assert past % tkc == 0 and pps % (tkc // PAGE) == 0
    nsteps = n_pages // pps + 1
    last = n_pages - 1
    n_chunks = past // tkc + 1
    n_top = min(DSA_TOPK_MAX, (past + ts) // 4)

    def page_spec(p):
        return pl.BlockSpec((None, None, 160, PAGE),
                            lambda b, j, pt: (layer, pt[b, jnp.minimum(j * pps + p, last)], 0, 0))

    full = lambda a: pl.BlockSpec(a.shape, lambda b, j, pt: (0, 0))
    per_b = lambda w: pl.BlockSpec((1, SAMPLE_T, w), lambda b, j, pt: (b, 0, 0))
    kern = functools.partial(_dsa_sample_kernel, pps=pps, past=past, tkc=tkc, n_top=n_top)
    return pl.pallas_call(
        kern,
        grid_spec=pltpu.PrefetchScalarGridSpec(
            num_scalar_prefetch=1,
            grid=(db, nsteps),
            in_specs=[page_spec(p) for p in range(pps)] + [
                per_b(256), per_b(256), per_b(128), per_b(256), full(pqd), full(pqi), full(tri)],
            out_specs=per_b(512),
            scratch_shapes=[pltpu.VMEM((n_chunks, 256, tkc), MXU_DTYPE), pltpu.VMEM((n_chunks, SAMPLE_T, tkc), I32)],
        ),
        out_shape=jax.ShapeDtypeStruct((db, SAMPLE_T, 512), F32),
        compiler_params=_cparams(("parallel", "arbitrary"), 48),
        name="dsa_sample",
    )(page_table, *([cache_t] * pps), qc, qi, misc, newd, pqd, pqi, tri)


def _out_kernel(h_ref, oa_ref, om_ref, od_ref, wa_ref, wm_ref, wd_ref, g_ref, h1_o, xn_o):
    h1 = h_ref[...] + (_mm(oa_ref[...], wa_ref[...]) + _mm(om_ref[...], wm_ref[...]) + _mm(od_ref[...], wd_ref[...]))
    h1_o[...] = h1
    xn = h1 * lax.rsqrt(jnp.mean(h1 * h1, axis=-1, keepdims=True) + EPS) * g_ref[...]
    xn_o[...] = xn.astype(xn_o.dtype)


def _out_proj(h, oa, om, od, wa, wm, wd, g, tm):
    n = h.shape[0]
    row = lambda w: pl.BlockSpec((tm, w), lambda i: (i, 0))
    full = lambda a: pl.BlockSpec(a.shape, lambda i: (0, 0))
    return pl.pallas_call(
        _out_kernel,
        grid=(n // tm,),
        in_specs=[row(D_MODEL), row(512), row(512), row(512), full(wa), full(wm), full(wd), full(g)],
        out_specs=[row(D_MODEL), row(D_MODEL)],
        out_shape=[jax.ShapeDtypeStruct((n, D_MODEL), F32), jax.ShapeDtypeStruct((n, D_MODEL), MXU_DTYPE)],
        compiler_params=_cparams(("parallel",), 40),
        name="out_proj",
    )(h, oa, om, od, wa, wm, wd, g)


def _ffn_kernel(*refs, tm, seq, sample):
    if sample:
        (xn_ref, h1_ref, p_ref, wu_ref, wg_ref, wd_ref, cw_ref, cb_ref, gp_ref, wpg_ref, wpp_ref,
         p1_ref, p2_ref, h_o, a_o, acc_s) = refs
    else:
        (xn_ref, halo_ref, h1_ref, p_ref, wu_ref, wg_ref, wd_ref, cw_ref, cb_ref, gp_ref, wpg_ref, wpp_ref,
         h_o, a_o, acc_s) = refs
    i = pl.program_id(0)
    j = pl.program_id(1)
    xn = xn_ref[...]
    a = _mm(xn, wu_ref[...])
    b = _mm(xn, wg_ref[...])
    a_o[...] = a
    row = _iota((tm, 1), 0)
    r1 = pltpu.roll(a, 1, 0)
    r2 = pltpu.roll(a, 2, 0)
    if sample:
        tpos = row % SAMPLE_T
        prev1 = jnp.where(tpos == 0, p1_ref[...], r1)
        prev2 = jnp.where(tpos < 2, p2_ref[...], r2)
    else:
        ah = _mm(halo_ref[...], wu_ref[...])
        ah = jnp.where((i * tm) % seq == 0, 0.0, ah)
        prev1 = jnp.where(row == 0, ah[7:8, :], r1)
        prev2 = jnp.where(row == 0, ah[6:7, :], jnp.where(row == 1, ah[7:8, :], r2))
    c = cb_ref[...] + cw_ref[2:3, :] * a
    c = c + cw_ref[0:1, :] * prev2
    c = c + cw_ref[1:2, :] * prev1
    u = (c * jax.nn.sigmoid(c)) * b
    y = _mm(u, wd_ref[...])

    @pl.when(j == 0)
    def _():
        acc_s[...] = y

    @pl.when(j > 0)
    def _():
        acc_s[...] = acc_s[...] + y

    @pl.when(j == pl.num_programs(1) - 1)
    def _():
        h2 = h1_ref[...] + acc_s[...]
        xn3 = h2 * lax.rsqrt(jnp.mean(h2 * h2, axis=-1, keepdims=True) + EPS) * gp_ref[...]
        gate = jax.nn.sigmoid(_mm(xn3, wpg_ref[...]))
        h_o[...] = h2 + gate * _mm(p_ref[...], wpp_ref[...])


def _ffn(xn, h1, p, wu, wg, wd, cw, cb, gp, wpg, wpp, tm, tf, seq, prefix=None):
    n = xn.shape[0]
    nf = D_FF // tf
    sample = prefix is not None
    row = lambda w: pl.BlockSpec((tm, w), lambda i, j: (i, 0))
    full = lambda a: pl.BlockSpec(a.shape, lambda i, j: (0, 0))
    ff_col = lambda r: pl.BlockSpec((r, tf), lambda i, j: (0, j))
    in_specs = [row(D_MODEL)]
    args = [xn]
    if not sample:
        in_specs.append(pl.BlockSpec((SUB, D_MODEL), lambda i, j: (jnp.maximum(i * (tm // SUB) - 1, 0), 0)))
        args.append(xn)
    in_specs += [row(D_MODEL), row(PLE_DIM), ff_col(D_MODEL), ff_col(D_MODEL),
                 pl.BlockSpec((tf, D_MODEL), lambda i, j: (j, 0)), ff_col(CONV_W), ff_col(1),
                 full(gp), full(wpg), full(wpp)]
    args += [h1, p, wu, wg, wd, cw, cb, gp, wpg, wpp]
    if sample:
        in_specs += [pl.BlockSpec((tm, tf), lambda i, j: (i, j))] * 2
        args += list(prefix)
    kern = functools.partial(_ffn_kernel, tm=tm, seq=seq, sample=sample)
    return pl.pallas_call(
        kern,
        grid=(n // tm, nf),
        in_specs=in_specs,
        out_specs=[row(D_MODEL), pl.BlockSpec((tm, tf), lambda i, j: (i, j))],
        out_shape=[jax.ShapeDtypeStruct((n, D_MODEL), F32), jax.ShapeDtypeStruct((n, D_FF), F32)],
        scratch_shapes=[pltpu.VMEM((tm, D_MODEL), F32)],
        compiler_params=_cparams(("parallel", "arbitrary"), 56),
        name="ffn",
    )(*args)


def _pick(n, prefs):
    for p in prefs:
        if n % p == 0:
            return p
    return n


def kernel(x_prompt, x_sample, cache_nsa, cache_mla, cache_dsa, state_win, state_conv, page_table, p_prompt, p_sample, norm_mix, w_in, nsa_qn, nsa_kn, nsa_cmp_pos, mla_cqn, mla_ckvn, mla_w_uq, mla_w_ukv, mla_qn, mla_kn, dsa_qn, dsa_kn, w_out, norm_ffn, ffn_w_up, ffn_w_gate, ffn_conv_w, ffn_conv_b, ffn_w_down, norm_ple, ple_w_gate, ple_w_proj):
    depth = w_in.shape[0]
    B, T, _ = x_prompt.shape
    DB, TS, _ = x_sample.shape
    n_pages = page_table.shape[1]
    past = n_pages * PAGE
    assert TS <= 4 and T % 128 == 0 and T >= NSA_WINDOW + 128 and past >= NSA_WINDOW
    mm = MXU_DTYPE
    bf = lambda a: jnp.asarray(a, jnp.bfloat16)

    win_cols, wuq_cols = _win_cols(), _wuq_cols()
    g256 = bf(_blockdiag(256, 64))
    g1024 = bf(_blockdiag(1024, 128))
    g8 = bf((np.arange(1024)[:, None] // 128 == np.arange(128)[None, :]).astype(np.float32))
    pk = bf(_placement(128, 1024, [(j, h * 128 + 64 + j) for h in range(8) for j in range(32)]))
    swc = bf(_placement(256, 256, [(j, (j + 32) % 64) for j in range(64)]))
    pq_c = bf(_q_placement(4, 64, 256, 0))
    pq_s = bf(_q_placement(4, 64, 256, 128))
    pq_w = bf(_q_placement(4, 64, 128, 0))
    pq_d = bf(_q_placement(4, 64, 256, 0))
    pq_i = bf(_q_placement(8, 32, 256, 128))
    tkc = _pick(T, (512, 256, 128))
    tri = bf(np.triu(np.ones((tkc, tkc), np.float32)))
    tab_p = _rope_tables(jnp.arange(T))
    pos_s = past + (jnp.arange(DB * SAMPLE_T) % SAMPLE_T)
    tab_s = _rope_tables(pos_s)
    nblk_p = T // NSA_BLK
    ctab_p = _cmp_tables(nblk_p)
    nbp_s = ((past // NSA_BLK + 1 + LANE - 1) // LANE) * LANE
    ctab_s = _cmp_tables(nbp_s)
    lpad_s = ((past + SAMPLE_T + tkc - 1) // tkc) * tkc
    blk_of = lambda n_keys: np.arange(n_keys) // NSA_BLK
    expand_p = bf((blk_of(T).reshape(T // tkc, 1, tkc) == np.arange(nblk_p)[None, :, None]).astype(np.float32))
    expand_s = bf((blk_of(lpad_s)[None, None, :] == np.arange(nbp_s)[None, :, None]).astype(np.float32))
    exp_rows = np.full((512,), -1, np.int64)
    for h in range(4):
        exp_rows[h * 128 + 64: (h + 1) * 128] = h * 64 + np.arange(64)

    tm_p = _pick(B * T, (256, 128))
    tq = 256
    tq_m = _pick(T, (512, 256, 128))
    tm_o = _pick(B * T, (512, 256, 128))
    tf = D_FF // 2
    ns = DB * SAMPLE_T

    cache_mla_t = jnp.swapaxes(cache_mla, 2, 3)
    cache_dsa_t = jnp.swapaxes(cache_dsa, 2, 3)
    hp = x_prompt.reshape(B * T, D_MODEL)
    hs = jnp.pad(x_sample, ((0, 0), (0, SAMPLE_T - TS), (0, 0))).reshape(ns, D_MODEL)
    outs = {k: [] for k in ("nsa_p", "nsa_s", "mla_p", "mla_s", "dsa_p", "dsa_s", "win_p", "win_s", "conv_p", "conv_s")}
    unpad = lambda a: a.reshape(DB, SAMPLE_T, -1)[:, :TS]

    for i in range(depth):
        wbig = _take_cols(w_in[i], win_cols).astype(mm)
        wuq = _take_cols(mla_w_uq[i], wuq_cols).astype(mm)
        gv = _gain_vector(norm_mix[i], nsa_qn[i], nsa_kn[i], mla_cqn[i], mla_ckvn[i], dsa_qn[i], dsa_kn[i], mla_qn[i])
        ukv = mla_w_ukv[i].reshape(MLA_KV_RANK, MLA_HEADS, MLA_NOPE + MLA_V)
        wk = jnp.pad(ukv[:, :, :MLA_NOPE], ((0, 0), (0, 0), (0, LANE - MLA_NOPE))).reshape(MLA_KV_RANK, 1024).astype(mm)
        wv = ukv[:, :, MLA_NOPE:].reshape(MLA_KV_RANK, 512).astype(mm)
        gk = jnp.tile(jnp.concatenate([mla_kn[i], jnp.zeros((32,), F32)]), 8)[None, :]
        wkvt = jnp.concatenate([ukv[:, :, :MLA_NOPE].reshape(MLA_KV_RANK, 512),
                                ukv[:, :, MLA_NOPE:].reshape(MLA_KV_RANK, 512)], axis=1).T.astype(mm)
        gkc = mla_kn[i][:, None]
        wpos = jnp.concatenate([nsa_cmp_pos[i, 0], nsa_cmp_pos[i, 1], jnp.zeros((NSA_BLK, 128), F32)], axis=1)
        wo = w_out[i]
        expand = lambda w: jnp.where(jnp.asarray(exp_rows >= 0)[:, None],
                                     jnp.take(w, jnp.asarray(np.maximum(exp_rows, 0)), axis=0), 0.0).astype(mm)
        wo_a, wo_m, wo_d = expand(wo[0:256]), wo[256:768].astype(mm), expand(wo[768:1024])
        ffn_w = (ffn_w_up[i].astype(mm), ffn_w_gate[i].astype(mm), ffn_w_down[i].astype(mm), ffn_conv_w[i],
                 ffn_conv_b[i][None, :], norm_ple[i][None, :], ple_w_gate[i].astype(mm), ple_w_proj[i].astype(mm))
        g_ffn = norm_ffn[i][None, :]

        (qa, qm, qc, qi, misc, rn, rm, rd, rw, nk, wkk, dk, kp) = _proj(
            hp, tab_p, T // tm_p, gv, wbig, wuq, g256, g1024, tm_p)
        km, vm = _mla_prep(rm, kp, wk, wv, pk, g8, gk, _pick(B * T, (512, 256, 128)))
        o_m = _mla_flash(qm, km, vm, B, T, tq_m, _pick(T, (512, 256, 128)))
        cmp = _nsa_cmp(rn, wpos, ctab_p, swc, B, T)
        o_a = _nsa_prompt(qa, misc, cmp, nk, wkk, pq_c, pq_s, pq_w, expand_p, B, T, tq, tkc)
        o_d = _dsa_prompt(qc, qi, misc, dk, pq_d, pq_i, tri, B, T, tq, tkc)
        h1, xn2 = _out_proj(hp, o_a, o_m, o_d, wo_a, wo_m, wo_d, g_ffn, tm_o)
        hp, a_p = _ffn(xn2, h1, p_prompt[i].reshape(B * T, PLE_DIM), *ffn_w, tm_o, tf, T)
        outs["nsa_p"].append(rn.reshape(B, T, -1))
        outs["mla_p"].append(rm.reshape(B, T, -1))
        outs["dsa_p"].append(rd.reshape(B, T, -1))
        outs["win_p"].append(rw.reshape(B, T, -1)[:, T - min(NSA_WINDOW, T):])
        outs["conv_p"].append(a_p.reshape(B, T, D_FF)[:, T - (CONV_W - 1):])

        (qa, qm, qc, qi, misc, rn, rm, rd, rw, nk, wkk, dk, kp) = _proj(
            hs, tab_s, 1, gv, wbig, wuq, g256, g1024, _pick(ns, (256, 128)))
        r3 = lambda a: a.reshape(DB, SAMPLE_T, -1)
        pps = _pick(n_pages, (8, 4))
        o_m = _mla_sample(page_table, cache_mla_t, i, r3(qm), r3(rm), r3(kp), wkvt, gkc, wk, wv, pk, g8, gk, pps)
        o_a = _nsa_sample(page_table, cache_nsa, i, r3(qa), r3(misc), r3(rn), state_win, r3(rw), wpos, ctab_s, swc,
                          pq_c, pq_s, pq_w, expand_s, pps)
        o_d = _dsa_sample(page_table, cache_dsa_t, i, r3(qc), r3(qi), r3(misc), r3(dk), pq_d, pq_i, tri, pps, tkc, TS)
        tm_s = _pick(ns, (256, 128))
        h1, xn2 = _out_proj(hs, o_a.reshape(ns, 512), o_m.reshape(ns, 512), o_d.reshape(ns, 512),
                            wo_a, wo_m, wo_d, g_ffn, tm_s)
        sc = state_conv[i]
        zrow = jnp.zeros((DB, SAMPLE_T - 1, D_FF), F32)
        p1 = jnp.concatenate([sc[:, 1:2], zrow], axis=1).reshape(ns, D_FF)
        p2 = jnp.concatenate([sc[:, 0:2], zrow[:, 1:]], axis=1).reshape(ns, D_FF)
        p_s = jnp.pad(p_sample[i], ((0, 0), (0, SAMPLE_T - TS), (0, 0))).reshape(ns, PLE_DIM)
        hs, a_s = _ffn(xn2, h1, p_s, *ffn_w, tm_s, tf, SAMPLE_T, prefix=(p1, p2))
        outs["nsa_s"].append(unpad(rn))
        outs["mla_s"].append(unpad(rm))
        outs["dsa_s"].append(unpad(rd))
        win_all = jnp.concatenate([state_win[i], unpad(rw)], axis=1)
        outs["win_s"].append(win_all[:, TS:])
        a_ext = jnp.concatenate([sc, unpad(a_s)], axis=1)
        outs["conv_s"].append(a_ext[:, TS:])

    st = lambda k: jnp.stack(outs[k])
    return (hp.reshape(B, T, D_MODEL), unpad(hs),
            st("nsa_p"), st("nsa_s"), st("mla_p"), st("mla_s"), st("dsa_p"), st("dsa_s"),
            st("win_p"), st("win_s"), st("conv_p"), st("conv_s"))
```

```python
import functools
import math

import numpy as np
import jax
import jax.numpy as jnp
from jax import lax
from jax.experimental import pallas as pl
from jax.experimental.pallas import tpu as pltpu

F32 = jnp.float32
I32 = jnp.int32
MXU_DTYPE = jnp.bfloat16

D_MODEL = 1024
D_HEAD = 64
ROPE_THETA = 10000.0
EPS = 1e-6
NEG = -1e30
MASK_FLOOR = -1e29
NSA_HEADS = 4
NSA_BLK = 64
NSA_TOPN = 16
NSA_WINDOW = 512
NSA_FORCE = 1e9
MLA_HEADS = 8
MLA_Q_RANK = 256
MLA_KV_RANK = 256
MLA_NOPE = 64
MLA_ROPE = 32
MLA_V = 64
MLA_QK = MLA_NOPE + MLA_ROPE
DSA_HEADS = 4
IDX_HEADS = 8
IDX_DIM = 32
DSA_TOPK_MAX = 256
D_FF = 2816
CONV_W = 3
PLE_DIM = 256
PAGE = 128
LANE = 128
SUB = 8
SAMPLE_T = 8

C_QA, C_KC, C_VC, C_KS, C_VS, C_KW, C_VW, C_GA = 0, 256, 320, 384, 448, 512, 576, 640
C_CQ, C_CKV, C_KPE, C_QC, C_KD, C_VD, C_QI, C_KI, C_WI = 652, 908, 1164, 1196, 1452, 1516, 1580, 1836, 1868

(Z_QA, Z_QAS, Z_NSA, Z_NSAS, Z_WIN, Z_WINS, Z_MISC, Z_CQ, Z_CKV, Z_KPE, Z_KPES, Z_QC, Z_QCS,
 Z_DKV, Z_DKVS, Z_KI, Z_KIS, Z_QI, Z_QIS, Z_TOTAL) = (
    0, 256, 512, 768, 1024, 1152, 1280, 1408, 1664, 1920, 2048, 2176, 2432, 2688, 2816, 2944,
    3072, 3200, 3456, 3712)

(T_64C, T_64S, T_NSAC, T_NSAS, T_KVC, T_KVS, T_QC, T_QS, T_32C, T_32S, T_328C, T_328S, T_TOTAL) = (
    0, 256, 512, 768, 1024, 1152, 1280, 2304, 3328, 3456, 3584, 3840, 4096)

(GV_NMIX, GV_QA_A, GV_QA_B, GV_NSA_A, GV_NSA_B, GV_WIN_A, GV_WIN_B, GV_CQ, GV_CKV, GV_QC_A, GV_QC_B,
 GV_DKV_A, GV_DKV_B, GV_Q, GV_TOTAL) = (
    0, 1024, 1280, 1536, 1792, 2048, 2176, 2304, 2560, 2816, 3072, 3328, 3456, 3584, 4608)


def _swap_idx(base, d, n):
    l = np.arange(n * d)
    return base + (l // d) * d + ((l % d) + d // 2) % d


def _win_cols():
    cols = np.full((Z_TOTAL,), -1, np.int64)

    def put(off, idx, at=0):
        cols[off + at: off + at + len(idx)] = idx

    put(Z_QA, C_QA + np.arange(256))
    put(Z_QAS, _swap_idx(C_QA, 64, 4))
    put(Z_NSA, C_KC + np.arange(256))
    put(Z_NSAS, _swap_idx(C_KS, 64, 1), at=128)
    put(Z_WIN, C_KW + np.arange(128))
    put(Z_WINS, _swap_idx(C_KW, 64, 1))
    put(Z_MISC, C_GA + np.arange(12))
    put(Z_MISC, C_WI + np.arange(8), at=12)
    put(Z_CQ, C_CQ + np.arange(256))
    put(Z_CKV, C_CKV + np.arange(256))
    put(Z_KPE, C_KPE + np.arange(32))
    put(Z_KPES, _swap_idx(C_KPE, 32, 1))
    put(Z_QC, C_QC + np.arange(256))
    put(Z_QCS, _swap_idx(C_QC, 64, 4))
    put(Z_DKV, C_KD + np.arange(128))
    put(Z_DKVS, _swap_idx(C_KD, 64, 1))
    put(Z_KI, C_KI + np.arange(32))
    put(Z_KIS, _swap_idx(C_KI, 32, 1))
    put(Z_QI, C_QI + np.arange(256))
    put(Z_QIS, _swap_idx(C_QI, 32, 8))
    return cols


def _wuq_cols():
    cols = np.full((2 * MLA_HEADS * LANE,), -1, np.int64)
    for h in range(MLA_HEADS):
        cols[h * LANE: h * LANE + MLA_QK] = h * MLA_QK + np.arange(MLA_QK)
        cols[1024 + h * LANE + MLA_NOPE: 1024 + h * LANE + MLA_QK] = _swap_idx(h * MLA_QK + MLA_NOPE, MLA_ROPE, 1)
    return cols


def _take_cols(w, cols):
    g = jnp.take(w, jnp.asarray(np.maximum(cols, 0)), axis=1)
    return jnp.where(jnp.asarray(cols >= 0)[None, :], g, 0.0)


def _blockdiag(n, d):
    i = np.arange(n)
    return (i[:, None] // d == i[None, :] // d).astype(np.float32)


def _placement(n_in, n_out, pairs):
    m = np.zeros((n_in, n_out), np.float32)
    for s, d in pairs:
        m[s, d] = 1.0
    return m


def _q_placement(n_heads, d, width, at):
    pairs = [(h * d + j, h * width + at + j) for h in range(n_heads) for j in range(d)]
    return _placement(n_heads * d, n_heads * width, pairs)


def _rope_tables(pos):
    pos = pos.astype(F32)[:, None]
    P = pos.shape[0]

    def cs(d):
        half = d // 2
        inv = ROPE_THETA ** (-jnp.arange(half, dtype=F32) / half)
        ang = pos * inv[None, :]
        c, s = jnp.cos(ang), jnp.sin(ang)
        return jnp.concatenate([c, c], axis=1), jnp.concatenate([-s, s], axis=1)

    c64, s64 = cs(64)
    c32, s32 = cs(32)
    one = lambda n: jnp.ones((P, n), F32)
    zero = lambda n: jnp.zeros((P, n), F32)
    segs = [
        jnp.tile(c64, (1, 4)), jnp.tile(s64, (1, 4)),
        jnp.concatenate([one(128), c64, one(64)], 1), jnp.concatenate([zero(128), s64, zero(64)], 1),
        jnp.concatenate([c64, one(64)], 1), jnp.concatenate([s64, zero(64)], 1),
        jnp.tile(jnp.concatenate([one(64), c32, one(32)], 1), (1, 8)),
        jnp.tile(jnp.concatenate([zero(64), s32, zero(32)], 1), (1, 8)),
        jnp.concatenate([c32, one(96)], 1), jnp.concatenate([s32, zero(96)], 1),
        jnp.tile(c32, (1, 8)), jnp.tile(s32, (1, 8)),
    ]
    return jnp.concatenate(segs, axis=1)


def _cmp_tables(nbp):
    pos = (jnp.arange(nbp) * NSA_BLK + (NSA_BLK - 1)).astype(F32)[:, None]
    inv = ROPE_THETA ** (-jnp.arange(32, dtype=F32) / 32)
    ang = pos * inv[None, :]
    c, s = jnp.cos(ang), jnp.sin(ang)
    one = jnp.ones((nbp, 192), F32)
    zero = jnp.zeros((nbp, 192), F32)
    return jnp.concatenate([c, c, one, -s, s, zero], axis=1)


def _swap64(g):
    return jnp.concatenate([g[32:], g[:32]])


def _gain_vector(norm_mix, nsa_qn, nsa_kn, mla_cqn, mla_ckvn, dsa_qn, dsa_kn, mla_qn):
    o64, z64 = jnp.ones((64,), F32), jnp.zeros((64,), F32)
    qpad = jnp.concatenate([mla_qn, jnp.zeros((32,), F32)])
    segs = [
        norm_mix,
        jnp.tile(nsa_qn, 4), jnp.tile(_swap64(nsa_qn), 4),
        jnp.concatenate([nsa_kn[0], o64, nsa_kn[1], o64]), jnp.concatenate([z64, z64, _swap64(nsa_kn[1]), z64]),
        jnp.concatenate([nsa_kn[2], o64]), jnp.concatenate([_swap64(nsa_kn[2]), z64]),
        mla_cqn, mla_ckvn,
        jnp.tile(dsa_qn, 4), jnp.tile(_swap64(dsa_qn), 4),
        jnp.concatenate([dsa_kn, o64]), jnp.concatenate([_swap64(dsa_kn), z64]),
        jnp.tile(qpad, 8),
    ]
    return jnp.concatenate(segs)[None, :]


def _mm(a, b):
    return jnp.dot(a.astype(MXU_DTYPE), b.astype(MXU_DTYPE), preferred_element_type=F32)


def _mm_nt(a, b):
    return lax.dot_general(a.astype(MXU_DTYPE), b.astype(MXU_DTYPE), (((1,), (1,)), ((), ())),
                           preferred_element_type=F32)


def _split3(x):
    x1 = x.astype(jnp.bfloat16)
    r1 = x - x1.astype(F32)
    x2 = r1.astype(jnp.bfloat16)
    x3 = (r1 - x2.astype(F32)).astype(jnp.bfloat16)
    return x1, x2, x3


def _place(x, p):
    x1, x2, x3 = _split3(x)
    d = lambda a: jnp.dot(a, p, preferred_element_type=F32)
    return d(x1) + d(x2) + d(x3)


def _gsum(x2, g):
    hi = x2.astype(jnp.bfloat16)
    lo = (x2 - hi.astype(F32)).astype(jnp.bfloat16)
    return jnp.dot(hi, g, preferred_element_type=F32) + jnp.dot(lo, g, preferred_element_type=F32)


def _iota(shape, dim):
    return lax.broadcasted_iota(I32, shape, dim)


def _stack_heads(x, n, w):
    return jnp.concatenate([x[:, h * w:(h + 1) * w] for h in range(n)], axis=0)


def _softmax_rows(s, mask):
    s = jnp.where(mask, s, NEG)
    m = jnp.max(s, axis=-1, keepdims=True)
    e = jnp.where(mask, jnp.exp(s - m), 0.0)
    l = jnp.sum(e, axis=-1, keepdims=True)
    return e / jnp.where(l > 0.0, l, 1.0)


def _online_update(carry, s, mask, v, vt=False):
    m, l, acc = carry
    if mask is not None:
        s = jnp.where(mask, s, NEG)
    m_new = jnp.maximum(m, jnp.max(s, axis=-1, keepdims=True))
    alpha = jnp.exp(m - m_new)
    p = jnp.exp(s - m_new)
    if mask is not None:
        p = jnp.where(mask, p, 0.0)
    l = alpha * l + jnp.sum(p, axis=-1, keepdims=True)
    acc = alpha * acc + (_mm_nt(p, v) if vt else _mm(p, v))
    return m_new, l, acc


def _flash_init(rows, width):
    return (jnp.full((rows, 1), MASK_FLOOR, F32), jnp.zeros((rows, 1), F32), jnp.zeros((rows, width), F32))


def _finish(l, acc):
    return acc / jnp.where(l > 0.0, l, 1.0)


def _sortable(x):
    b = lax.bitcast_convert_type(x, I32)
    return jnp.where(b < 0, b ^ jnp.int32(0x7FFFFFFF), b)


def _cparams(sem, vmem_mb):
    return pltpu.CompilerParams(dimension_semantics=sem, vmem_limit_bytes=vmem_mb * 1024 * 1024)


def _proj_kernel(x_ref, tab_ref, gv_ref, w_ref, wuq_ref, g256_ref, g1024_ref,
                 qa_o, qm_o, qc_o, qi_o, misc_o, rn_o, rm_o, rd_o, rw_o, nk_o, wk_o, dk_o, kp_o):
    gv = lambda off, w: gv_ref[:, off:off + w]
    tab = lambda off, w: tab_ref[:, off:off + w]
    x = x_ref[...]
    xn = x * lax.rsqrt(jnp.mean(x * x, axis=-1, keepdims=True) + EPS) * gv(GV_NMIX, D_MODEL)
    z = _mm(xn, w_ref[...])
    zs = lambda off, w: z[:, off:off + w]
    g256 = g256_ref[...]
    g128 = g256_ref[0:128, 0:128]

    def group(zo, zso, w, gmat, d, ga, gb, tc, ts, normmask=None):
        a, asw = zs(zo, w), zs(zso, w)
        if gmat is None:
            rs = None
        else:
            rs = lax.rsqrt(_gsum(a * a, gmat) / d + EPS)
            if normmask is not None:
                rs = jnp.where(normmask, rs, 1.0)
        ca, cb = tab(tc, w), tab(ts, w)
        if ga is not None:
            ca, cb = gv(ga, w) * ca, gv(gb, w) * cb
        if rs is not None:
            ca, cb = rs * ca, rs * cb
        return a * ca + asw * cb

    qa_o[...] = group(Z_QA, Z_QAS, 256, g256, 64.0, GV_QA_A, GV_QA_B, T_64C, T_64S).astype(qa_o.dtype)
    lane = _iota((1, 256), 1)
    nm = (lane < 64) | ((lane >= 128) & (lane < 192))
    rn = group(Z_NSA, Z_NSAS, 256, g256, 64.0, GV_NSA_A, GV_NSA_B, T_NSAC, T_NSAS, nm)
    rn_o[...] = rn
    nk_o[...] = rn.astype(nk_o.dtype)
    nm128 = _iota((1, 128), 1) < 64
    rw = group(Z_WIN, Z_WINS, 128, g128, 64.0, GV_WIN_A, GV_WIN_B, T_KVC, T_KVS, nm128)
    rw_o[...] = rw
    wk_o[...] = rw.astype(wk_o.dtype)
    zm = zs(Z_MISC, 128)
    misc_o[...] = jnp.where(_iota((1, 128), 1) < 12, jax.nn.sigmoid(zm), zm * (IDX_HEADS ** -0.5))
    cq = zs(Z_CQ, 256)
    cqn = cq * lax.rsqrt(jnp.mean(cq * cq, axis=-1, keepdims=True) + EPS) * gv(GV_CQ, 256)
    q2 = _mm(cqn, wuq_ref[...])
    qr = q2[:, 0:1024] * tab(T_QC, 1024) + q2[:, 1024:2048] * tab(T_QS, 1024)
    rs = lax.rsqrt(_gsum(qr * qr, g1024_ref[...]) / float(MLA_QK) + EPS)
    qm_o[...] = (qr * rs * gv(GV_Q, 1024)).astype(qm_o.dtype)
    ckv = zs(Z_CKV, 256)
    rm_o[:, 0:256] = ckv * lax.rsqrt(jnp.mean(ckv * ckv, axis=-1, keepdims=True) + EPS) * gv(GV_CKV, 256)
    kp = group(Z_KPE, Z_KPES, 128, None, 0.0, None, None, T_32C, T_32S)
    rm_o[:, 256:288] = kp[:, 0:32]
    kp_o[...] = kp
    qc_o[...] = group(Z_QC, Z_QCS, 256, g256, 64.0, GV_QC_A, GV_QC_B, T_64C, T_64S).astype(qc_o.dtype)
    dkv = group(Z_DKV, Z_DKVS, 128, g128, 64.0, GV_DKV_A, GV_DKV_B, T_KVC, T_KVS, nm128)
    ki = group(Z_KI, Z_KIS, 128, None, 0.0, None, None, T_32C, T_32S)
    rd_o[:, 0:128] = dkv
    rd_o[:, 128:160] = ki[:, 0:32]
    dk_o[:, 0:128] = dkv.astype(dk_o.dtype)
    dk_o[:, 128:256] = ki.astype(dk_o.dtype)
    qi_o[...] = group(Z_QI, Z_QIS, 256, None, 0.0, None, None, T_328C, T_328S).astype(qi_o.dtype)


def _proj(x, tab, n_pos_tiles, gv, wbig, wuq, g256, g1024, tm):
    n = x.shape[0]
    row = lambda w: pl.BlockSpec((tm, w), lambda i: (i, 0))
    full = lambda a: pl.BlockSpec(a.shape, lambda i: (0, 0))
    widths = [(256, MXU_DTYPE), (1024, MXU_DTYPE), (256, MXU_DTYPE), (256, MXU_DTYPE), (128, F32),
              (256, F32), (288, F32), (160, F32), (128, F32),
              (256, MXU_DTYPE), (128, MXU_DTYPE), (256, MXU_DTYPE), (128, F32)]
    return pl.pallas_call(
        _proj_kernel,
        grid=(n // tm,),
        in_specs=[row(D_MODEL),
                  pl.BlockSpec((tm, T_TOTAL), lambda i: (i % n_pos_tiles, 0)),
                  full(gv), full(wbig), full(wuq), full(g256), full(g1024)],
        out_specs=[row(w) for w, _ in widths],
        out_shape=[jax.ShapeDtypeStruct((n, w), dt) for w, dt in widths],
        compiler_params=_cparams(("parallel",), 56),
        name="proj",
    )(x, tab, gv, wbig, wuq, g256, g1024)


def _mla_keys(ckv, kpe128, wk, wv, pk, g8, gk):
    kraw = _mm(ckv, wk) + _place(kpe128, pk)
    ss = _gsum(kraw * kraw, g8)
    rs = lax.rsqrt(ss / float(MLA_QK) + EPS)
    parts = [kraw[:, h * LANE:(h + 1) * LANE] * rs[:, h:h + 1] for h in range(MLA_HEADS)]
    km = jnp.concatenate(parts, axis=1) * gk
    return km.astype(MXU_DTYPE), _mm(ckv, wv).astype(MXU_DTYPE)


def _mla_prep_kernel(rm_ref, kp_ref, wk_ref, wv_ref, pk_ref, g8_ref, gk_ref, km_o, vm_o):
    km, vm = _mla_keys(rm_ref[:, 0:256], kp_ref[...], wk_ref[...], wv_ref[...], pk_ref[...],
                       g8_ref[...], gk_ref[...])
    km_o[...] = km
    vm_o[...] = vm


def _mla_prep(rm, kp, wk, wv, pk, g8, gk, tk):
    n = rm.shape[0]
    full = lambda a: pl.BlockSpec(a.shape, lambda i: (0, 0))
    return pl.pallas_call(
        _mla_prep_kernel,
        grid=(n // tk,),
        in_specs=[pl.BlockSpec((tk, 288), lambda i: (i, 0)), pl.BlockSpec((tk, 128), lambda i: (i, 0)),
                  full(wk), full(wv), full(pk), full(g8), full(gk)],
        out_specs=[pl.BlockSpec((tk, 1024), lambda i: (i, 0)), pl.BlockSpec((tk, 512), lambda i: (i, 0))],
        out_shape=[jax.ShapeDtypeStruct((n, 1024), MXU_DTYPE), jax.ShapeDtypeStruct((n, 512), MXU_DTYPE)],
        compiler_params=_cparams(("parallel",), 40),
        name="mla_prep",
    )(rm, kp, wk, wv, pk, g8, gk)


def _mla_flash_kernel(q_ref, k_ref, v_ref, o_ref, *, tq, tk):
    i = pl.program_id(2)
    scale = MLA_QK ** -0.5
    q_pos = i * tq + _iota((tq, 1), 0)
    qs = [q_ref[:, hh * LANE:(hh + 1) * LANE] for hh in range(2)]

    def chunk(c, carry, masked):
        k0 = pl.multiple_of(c * tk, tk)
        v = v_ref[pl.ds(k0, tk), :]
        mask = ((k0 + _iota((1, tk), 1)) <= q_pos) if masked else None
        out = []
        for hh in range(2):
            k = k_ref[pl.ds(k0, tk), hh * LANE:(hh + 1) * LANE]
            out.append(_online_update(carry[hh], _mm_nt(qs[hh], k) * scale, mask, v))
        return tuple(out)

    one = (jnp.full((tq, 1), NEG, F32), jnp.zeros((tq, 1), F32), jnp.zeros((tq, LANE), F32))
    n_full = (i * tq) // tk
    n_all = ((i + 1) * tq + tk - 1) // tk
    carry = lax.fori_loop(0, n_full, lambda c, cr: chunk(c, cr, False), (one, one))
    carry = lax.fori_loop(n_full, n_all, lambda c, cr: chunk(c, cr, True), carry)
    outs = [_finish(cr[1], cr[2]) for cr in carry]
    lane = _iota((1, LANE), 1)
    o_ref[...] = jnp.where(lane < MLA_V, outs[0], outs[1])


def _mla_flash(qm, km, vm, nb, t, tq, tk):
    nq = t // tq
    kern = functools.partial(_mla_flash_kernel, tq=tq, tk=tk)
    return pl.pallas_call(
        kern,
        grid=(nb, MLA_HEADS // 2, nq),
        in_specs=[pl.BlockSpec((tq, 256), lambda b, h, i: (b * nq + i, h)),
                  pl.BlockSpec((t, 256), lambda b, h, i: (b, h)),
                  pl.BlockSpec((t, 128), lambda b, h, i: (b, h))],
        out_specs=pl.BlockSpec((tq, 128), lambda b, h, i: (b * nq + i, h)),
        out_shape=jax.ShapeDtypeStruct((nb * t, 512), F32),
        compiler_params=_cparams(("parallel", "parallel", "arbitrary"), 48),
        name="mla_flash",
    )(qm, km, vm)


def _mla_sample_kernel(pt_ref, *refs, pps, past):
    pages = refs[:pps]
    (q_ref, newm_ref, newkp_ref, wkvt_ref, gkc_ref, wk_ref, wv_ref, pk_ref, g8_ref, gk_ref, o_ref,
     m_s, l_s, acc_s) = refs[pps:]
    j = pl.program_id(1)
    nsteps = pl.num_programs(1)
    rows = SAMPLE_T * MLA_HEADS
    q = q_ref[0].astype(F32)
    head_of_lane = _iota((SUB, 1024), 1) // LANE
    sub = _iota((SUB, 1024), 0)
    qbd = jnp.concatenate(
        [jnp.where(head_of_lane == sub, jnp.broadcast_to(q[t:t + 1, :], (SUB, 1024)), 0.0)
         for t in range(SAMPLE_T)], axis=0).astype(MXU_DTYPE)
    q_pos = past + _iota((rows, 1), 0) // MLA_HEADS
    scale = MLA_QK ** -0.5

    @pl.when(j == 0)
    def _():
        m_s[...] = jnp.full(m_s.shape, NEG, F32)
        l_s[...] = jnp.zeros(l_s.shape, F32)
        acc_s[...] = jnp.zeros(acc_s.shape, F32)

    def update(s, mask, v, vt):
        m, l, acc = _online_update((m_s[...], l_s[...], acc_s[...]), s, mask, v, vt)
        m_s[...] = m
        l_s[...] = l
        acc_s[...] = acc

    @pl.when(j < nsteps - 1)
    def _():
        ppg = min(pps, 8)
        n = ppg * PAGE
        s_parts, v_parts = [], []
        for g in range(pps // ppg):
            xt = jnp.concatenate([pages[g * ppg + p][...] for p in range(ppg)], axis=1)
            kpet = xt[256:288]
            kv = jnp.dot(wkvt_ref[...], xt[0:256].astype(MXU_DTYPE), preferred_element_type=F32)
            pe2 = jnp.sum(kpet * kpet, axis=0, keepdims=True)
            parts = []
            for h in range(MLA_HEADS):
                kn = kv[h * MLA_NOPE:(h + 1) * MLA_NOPE]
                rs = lax.rsqrt((jnp.sum(kn * kn, axis=0, keepdims=True) + pe2) / float(MLA_QK) + EPS)
                parts += [(kn * rs) * gkc_ref[0:MLA_NOPE, :], (kpet * rs) * gkc_ref[MLA_NOPE:MLA_QK, :],
                          jnp.zeros((LANE - MLA_QK, n), F32)]
            kmt = jnp.concatenate(parts, axis=0).astype(MXU_DTYPE)
            s_parts.append(jnp.dot(qbd, kmt, preferred_element_type=F32) * scale)
            v_parts.append(kv[512:1024].astype(MXU_DTYPE))
        s = jnp.concatenate(s_parts, axis=1)
        m_new = jnp.maximum(m_s[...], jnp.max(s, axis=-1, keepdims=True))
        alpha = jnp.exp(m_s[...] - m_new)
        p = jnp.exp(s - m_new)
        acc = alpha * acc_s[...]
        for g, vmt in enumerate(v_parts):
            acc = acc + _mm_nt(p[:, g * n:(g + 1) * n], vmt)
        m_s[...] = m_new
        l_s[...] = alpha * l_s[...] + jnp.sum(p, axis=-1, keepdims=True)
        acc_s[...] = acc

    @pl.when(j == nsteps - 1)
    def _():
        npad = 2 * SAMPLE_T
        ckv = jnp.concatenate([newm_ref[0][:, 0:256], jnp.zeros((npad - SAMPLE_T, 256), F32)], axis=0)
        kpe = jnp.concatenate([newkp_ref[0], jnp.zeros((npad - SAMPLE_T, 128), F32)], axis=0)
        km, vm = _mla_keys(ckv, kpe, wk_ref[...], wv_ref[...], pk_ref[...], g8_ref[...], gk_ref[...])
        update(_mm_nt(qbd, km) * scale, (past + _iota((1, npad), 1)) <= q_pos, vm, False)
        o = _finish(l_s[...], acc_s[...])
        lane_head = _iota((SUB, 512), 1) // MLA_V
        sub8 = _iota((SUB, 512), 0)
        out = jnp.zeros((SAMPLE_T, 512), F32)
        for t in range(SAMPLE_T):
            blk = jnp.where(lane_head == sub8, o[t * SUB:(t + 1) * SUB, :], 0.0)
            r = jnp.sum(blk, axis=0, keepdims=True)
            out = jnp.where(_iota((SAMPLE_T, 512), 0) == t, jnp.broadcast_to(r, (SAMPLE_T, 512)), out)
        o_ref[0] = out


def _mla_sample(page_table, cache_t, layer, qm, newm, newkp, wkvt, gkc, wk, wv, pk, g8, gk, pps):
    db, n_pages = page_table.shape
    past = n_pages * PAGE
    nsteps = n_pages // pps + 1
    last = n_pages - 1

    def page_spec(p):
        return pl.BlockSpec((None, None, 288, PAGE),
                            lambda b, j, pt: (layer, pt[b, jnp.minimum(j * pps + p, last)], 0, 0))

    full = lambda a: pl.BlockSpec(a.shape, lambda b, j, pt: (0, 0))
    per_b = lambda w: pl.BlockSpec((1, SAMPLE_T, w), lambda b, j, pt: (b, 0, 0))
    rows = SAMPLE_T * MLA_HEADS
    kern = functools.partial(_mla_sample_kernel, pps=pps, past=past)
    return pl.pallas_call(
        kern,
        grid_spec=pltpu.PrefetchScalarGridSpec(
            num_scalar_prefetch=1,
            grid=(db, nsteps),
            in_specs=[page_spec(p) for p in range(pps)] + [per_b(1024), per_b(288), per_b(128), full(wkvt), full(gkc),
                                                          full(wk), full(wv), full(pk), full(g8), full(gk)],
            out_specs=per_b(512),
            scratch_shapes=[pltpu.VMEM((rows, 1), F32), pltpu.VMEM((rows, 1), F32), pltpu.VMEM((rows, 512), F32)],
        ),
        out_shape=jax.ShapeDtypeStruct((db, SAMPLE_T, 512), F32),
        compiler_params=_cparams(("parallel", "arbitrary"), 40),
        name="mla_sample",
    )(page_table, *([cache_t] * pps), qm, newm, newkp, wkvt, gkc, wk, wv, pk, g8, gk)


def _cmp_blocks(rows, wpos):
    n = rows.shape[0] // NSA_BLK
    x = rows.reshape(n, NSA_BLK, 256) * wpos[None]
    return jnp.sum(x, axis=1) / float(NSA_BLK)


def _cmp_rope(craw, ctab, swc):
    return craw * ctab[:, 0:256] + _place(craw, swc) * ctab[:, 256:512]


def _nsa_cmp_kernel(rn_ref, wpos_ref, ctab_ref, swc_ref, o_ref):
    craw = _cmp_blocks(rn_ref[...], wpos_ref[...])
    o_ref[...] = _cmp_rope(craw, ctab_ref[...], swc_ref[...]).astype(o_ref.dtype)


def _nsa_cmp(rn, wpos, ctab, swc, nb, t):
    nblk = t // NSA_BLK
    tb = min(nblk, 16)
    nt = nblk // tb
    return pl.pallas_call(
        _nsa_cmp_kernel,
        grid=(nb, nt),
        in_specs=[pl.BlockSpec((tb * NSA_BLK, 256), lambda b, i: (b * nt + i, 0)),
                  pl.BlockSpec(wpos.shape, lambda b, i: (0, 0)),
                  pl.BlockSpec((tb, 512), lambda b, i: (i, 0)),
                  pl.BlockSpec(swc.shape, lambda b, i: (0, 0))],
        out_specs=pl.BlockSpec((tb, 256), lambda b, i: (b * nt + i, 0)),
        out_shape=jax.ShapeDtypeStruct((nb * nblk, 256), MXU_DTYPE),
        compiler_params=_cparams(("parallel", "parallel"), 32),
        name="nsa_cmp",
    )(rn, wpos, ctab, swc)


def _top_n_mask(x, n_sel):
    t, w = x.shape
    lane = _iota((t, w), 1).astype(F32)

    def body(_, carry):
        x, sel = carry
        m = jnp.max(x, axis=-1, keepdims=True)
        first = jnp.min(jnp.where(x == m, lane, float(w)), axis=-1, keepdims=True)
        hit = lane == first
        return jnp.where(hit, -jnp.inf, x), jnp.where(hit, 1.0, sel)

    _, sel = lax.fori_loop(0, n_sel, body, (x, jnp.zeros((t, w), F32)))
    return sel


def _nsa_attend(qa, misc, q_pos, cmp, nblk, key_ref, n_chunks, tkc, win, w_pos, pqc, pqs, pqw, expand_ref):
    tq = qa.shape[0]
    h = NSA_HEADS
    scale = D_HEAD ** -0.5
    nbp = cmp.shape[0]
    qp4 = jnp.concatenate([q_pos] * h, axis=0)
    qc_st = _stack_heads(_mm(qa, pqc), h, 256)
    blk = _iota((1, nbp), 1)
    s = _mm_nt(qc_st, cmp) * scale
    p_c = _softmax_rows(s, ((blk * NSA_BLK + (NSA_BLK - 1)) <= qp4) & (blk < nblk))
    o_c = _mm(p_c, cmp)
    imp = p_c[0:tq]
    for hh in range(1, h):
        imp = imp + p_c[hh * tq:(hh + 1) * tq]
    cur = q_pos // NSA_BLK
    imp = jnp.where(blk == cur, NSA_FORCE, jnp.where(blk < cur, imp, -1.0))
    imp = jnp.where(blk < nblk, imp, -jnp.inf)
    sel = _top_n_mask(imp, min(NSA_TOPN, nblk))
    qs_st = _stack_heads(_mm(qa, pqs), h, 256) * scale
    def scores(c):
        k0 = c * tkc if isinstance(c, int) else pl.multiple_of(c * tkc, tkc)
        keys = key_ref[pl.ds(k0, tkc), :]
        tok = _mm(sel, expand_ref[c])
        bias1 = jnp.where((tok > 0.5) & ((k0 + _iota((1, tkc), 1)) <= q_pos), 0.0, NEG)
        return _mm_nt(qs_st, keys), bias1, keys

    if isinstance(n_chunks, int):
        def chunk(c, carry):
            s, bias1, keys = scores(c)
            return _online_update(carry, s + jnp.concatenate([bias1] * h, axis=0), None, keys)

        _, l, acc = _loop(n_chunks, chunk, _flash_init(h * tq, 256))
    else:
        def chunk(c, carry):
            s, bias1, keys = scores(c)
            return tuple(_online_update(carry[hh], s[hh * tq:(hh + 1) * tq] + bias1, None, keys) for hh in range(h))

        heads = lax.fori_loop(0, n_chunks, chunk, tuple(_flash_init(tq, 256) for _ in range(h)))
        l = jnp.concatenate([hd[1] for hd in heads], axis=0)
        acc = jnp.concatenate([hd[2] for hd in heads], axis=0)
    o_s = _finish(l, acc)
    qw_st = _stack_heads(_mm(qa, pqw), h, 128)
    rel = qp4 - w_pos
    s = _mm_nt(qw_st, win) * scale
    p_w = _softmax_rows(s, (rel >= 0) & (rel < NSA_WINDOW) & (w_pos >= 0))
    o_w = _mm(p_w, win)
    gate = lambda jj: jnp.concatenate([misc[:, 3 * hh + jj:3 * hh + jj + 1] for hh in range(h)], axis=0)
    tsum = gate(0) * o_c[:, 0:128] + gate(1) * o_s[:, 128:256] + gate(2) * o_w
    tsum = jnp.where(_iota((1, 128), 1) >= D_HEAD, tsum, 0.0)
    return jnp.concatenate([tsum[hh * tq:(hh + 1) * tq] for hh in range(h)], axis=1)


def _nsa_prompt_kernel(qa_ref, misc_ref, cmp_ref, nk_ref, wk_ref, pqc_ref, pqs_ref, pqw_ref, exp_ref, o_ref,
                       *, tq, tkc, t):
    i = pl.program_id(1)
    s0 = i * tq
    q_pos = s0 + _iota((tq, 1), 0)
    w = NSA_WINDOW + tq
    kstart = pl.multiple_of(jnp.clip(s0 - NSA_WINDOW, 0, t - w), SUB)
    win = wk_ref[pl.ds(kstart, w), :]
    w_pos = kstart + _iota((1, w), 1)
    n_chunks = (s0 + tq + tkc - 1) // tkc
    o_ref[...] = _nsa_attend(qa_ref[...], misc_ref[...], q_pos, cmp_ref[...], t // NSA_BLK, nk_ref, n_chunks,
                             tkc, win, w_pos, pqc_ref[...], pqs_ref[...], pqw_ref[...], exp_ref)


def _nsa_prompt(qa, misc, cmp, nk, wk, pqc, pqs, pqw, expand, nb, t, tq, tkc):
    nq = t // tq
    nblk = t // NSA_BLK
    kern = functools.partial(_nsa_prompt_kernel, tq=tq, tkc=tkc, t=t)
    full = lambda a: pl.BlockSpec(a.shape, lambda b, i: (0, 0))
    return pl.pallas_call(
        kern,
        grid=(nb, nq),
        in_specs=[pl.BlockSpec((tq, 256), lambda b, i: (b * nq + i, 0)),
                  pl.BlockSpec((tq, 128), lambda b, i: (b * nq + i, 0)),
                  pl.BlockSpec((nblk, 256), lambda b, i: (b, 0)),
                  pl.BlockSpec((t, 256), lambda b, i: (b, 0)),
                  pl.BlockSpec((t, 128), lambda b, i: (b, 0)),
                  full(pqc), full(pqs), full(pqw), pl.BlockSpec(expand.shape, lambda b, i: (0, 0, 0))],
        out_specs=pl.BlockSpec((tq, 512), lambda b, i: (b * nq + i, 0)),
        out_shape=jax.ShapeDtypeStruct((nb * t, 512), F32),
        compiler_params=_cparams(("parallel", "arbitrary"), 48),
        name="nsa_prompt",
    )(qa, misc, cmp, nk, wk, pqc, pqs, pqw, expand)


def _nsa_sample_kernel(pt_ref, *refs, pps, past):
    pages = refs[:pps]
    (qa_ref, misc_ref, newn_ref, sw_ref, neww_ref, wpos_ref, ctab_ref, swc_ref, pqc_ref, pqs_ref, pqw_ref, exp_ref,
     o_ref, key_s, cmp_s, win_s) = refs[pps:]
    j = pl.program_id(1)
    nsteps = pl.num_programs(1)
    nblk = past // NSA_BLK + 1
    bps = pps * PAGE // NSA_BLK
    lpad = key_s.shape[0]

    @pl.when(j == 0)
    def _():
        key_s[past:lpad, :] = jnp.zeros((lpad - past, 256), key_s.dtype)
        cmp_s[...] = jnp.zeros(cmp_s.shape, F32)

    @pl.when(j < nsteps - 1)
    def _():
        rows = jnp.concatenate([pages[p][...] for p in range(pps)], axis=0)
        r0 = pl.multiple_of(j * pps * PAGE, pps * PAGE)
        key_s[pl.ds(r0, pps * PAGE), :] = rows.astype(key_s.dtype)
        b0 = pl.multiple_of(j * bps, bps)
        cmp_s[pl.ds(b0, bps), :] = _cmp_blocks(rows, wpos_ref[...])

    @pl.when(j == nsteps - 1)
    def _():
        newn = newn_ref[0]
        tail = jnp.concatenate([newn, jnp.zeros((NSA_BLK - SAMPLE_T, 256), F32)], axis=0)
        key_s[past:past + NSA_BLK, :] = tail.astype(key_s.dtype)
        cmp_s[past // NSA_BLK: past // NSA_BLK + 1, :] = _cmp_blocks(tail, wpos_ref[...])
        cmp = _cmp_rope(cmp_s[...], ctab_ref[...], swc_ref[...]).astype(MXU_DTYPE)
        win_s[0:NSA_WINDOW, :] = sw_ref[0].astype(win_s.dtype)
        win_s[NSA_WINDOW:NSA_WINDOW + 2 * SAMPLE_T, :] = jnp.concatenate(
            [neww_ref[0], jnp.zeros((SAMPLE_T, 128), F32)], axis=0).astype(win_s.dtype)
        w = NSA_WINDOW + 2 * SAMPLE_T
        w_pos = past - NSA_WINDOW + _iota((1, w), 1)
        q_pos = past + _iota((SAMPLE_T, 1), 0)
        o_ref[0] = _nsa_attend(qa_ref[0], misc_ref[0], q_pos, cmp, nblk, key_s, 1, lpad, win_s[...], w_pos,
                               pqc_ref[...], pqs_ref[...], pqw_ref[...], exp_ref)


def _nsa_sample(page_table, cache, layer, qa, misc, newn, state_win, neww, wpos, ctab, swc, pqc, pqs, pqw,
                expand, pps):
    db, n_pages = page_table.shape
    past = n_pages * PAGE
    nsteps = n_pages // pps + 1
    last = n_pages - 1
    nbp = ctab.shape[0]
    lpad = expand.shape[2]

    def page_spec(p):
        return pl.BlockSpec((None, None, PAGE, 256),
                            lambda b, j, pt: (layer, pt[b, jnp.minimum(j * pps + p, last)], 0, 0))

    full = lambda a: pl.BlockSpec(a.shape, lambda b, j, pt: (0, 0))
    per_b = lambda w: pl.BlockSpec((1, SAMPLE_T, w), lambda b, j, pt: (b, 0, 0))
    kern = functools.partial(_nsa_sample_kernel, pps=pps, past=past)
    return pl.pallas_call(
        kern,
        grid_spec=pltpu.PrefetchScalarGridSpec(
            num_scalar_prefetch=1,
            grid=(db, nsteps),
            in_specs=[page_spec(p) for p in range(pps)] + [
                per_b(256), per_b(128), per_b(256),
                pl.BlockSpec((None, 1, NSA_WINDOW, 128), lambda b, j, pt: (layer, b, 0, 0)),
                per_b(128), full(wpos), full(ctab), full(swc), full(pqc), full(pqs), full(pqw),
                pl.BlockSpec(expand.shape, lambda b, j, pt: (0, 0, 0))],
            out_specs=per_b(512),
            scratch_shapes=[pltpu.VMEM((lpad, 256), MXU_DTYPE), pltpu.VMEM((nbp, 256), F32),
                            pltpu.VMEM((NSA_WINDOW + 2 * SAMPLE_T, 128), MXU_DTYPE)],
        ),
        out_shape=jax.ShapeDtypeStruct((db, SAMPLE_T, 512), F32),
        compiler_params=_cparams(("parallel", "arbitrary"), 48),
        name="nsa_sample",
    )(page_table, *([cache] * pps), qa, misc, newn, state_win, neww, wpos, ctab, swc, pqc, pqs, pqw, expand)


def _loop(n, body, init):
    if isinstance(n, int):
        for c in range(n):
            init = body(c, init)
        return init
    return lax.fori_loop(0, n, body, init)


def _dsa_attend(qc, qi, misc, q_pos, key_ref, n_chunks, tkc, n_top, pqd, pqi, tri, key_s, kt=False):
    tq = qc.shape[0]
    h = DSA_HEADS
    qi_st = _stack_heads(_mm(qi, pqi), IDX_HEADS, 256)
    wcol = [misc[:, 12 + g:13 + g] for g in range(IDX_HEADS)]
    qk = _mm if kt else _mm_nt

    def get_keys(c):
        if kt:
            return key_ref[c]
        return key_ref[pl.ds(c * tkc if isinstance(c, int) else pl.multiple_of(c * tkc, tkc), tkc), :]

    def score_chunk(c, _):
        k0 = c * tkc
        rel = jnp.maximum(qk(qi_st, get_keys(c)), 0.0)
        sc = wcol[0] * rel[0:tq]
        for g in range(1, IDX_HEADS):
            sc = sc + wcol[g] * rel[g * tq:(g + 1) * tq]
        sc = jnp.where(sc == 0.0, 0.0, sc)
        sc = jnp.where((k0 + _iota((1, tkc), 1)) <= q_pos, sc, NEG)
        key_s[c] = _sortable(sc)
        return 0

    _loop(n_chunks, score_chunk, 0)

    def count(pred_fn):
        def body(c, acc):
            hit = jnp.where(pred_fn(key_s[c]), 1.0, 0.0)
            for u in range(tkc // LANE):
                acc = acc + hit[:, u * LANE:(u + 1) * LANE]
            return acc
        acc = _loop(n_chunks, body, jnp.zeros((tq, LANE), F32))
        return jnp.sum(acc, axis=-1, keepdims=True)

    static = isinstance(n_chunks, int)
    if static:
        def digit_step(it, thr):
            d = jnp.left_shift(jnp.int32(1), jnp.int32(30) - 2 * it)
            for mult in (1, 2, 3):
                cnt = count(lambda k: k >= thr + mult * d)
                step = jnp.where(cnt >= float(n_top), d, 0)
                new = thr + step if mult == 1 else new + step
            return new

        thr = lax.fori_loop(0, 16, digit_step, jnp.full((tq, 1), -2 ** 31, I32))
    else:
        def bit_step(it, thr):
            cand = thr + jnp.left_shift(jnp.int32(1), jnp.int32(31) - it)
            cnt = count(lambda k: k >= cand)
            return jnp.where(cnt >= float(n_top), cand, thr)

        thr = lax.fori_loop(0, 32, bit_step, jnp.full((tq, 1), -2 ** 31, I32))
    need = float(n_top) - count(lambda k: k > thr)

    qd_st = _stack_heads(_mm(qc, pqd), h, 256) * (D_HEAD ** -0.5)

    def scores(c, eq_seen):
        kk = key_s[c]
        eq = jnp.where(kk == thr, 1.0, 0.0)
        rank = eq_seen + _mm(eq, tri)
        sel = (kk > thr) | ((kk == thr) & (rank <= need))
        bias1 = jnp.where(sel & ((c * tkc + _iota((1, tkc), 1)) <= q_pos), 0.0, NEG)
        return eq_seen + jnp.sum(eq, axis=-1, keepdims=True), qk(qd_st, get_keys(c)), bias1

    if static:
        eq_seen, parts = jnp.zeros((tq, 1), F32), []
        for c in range(n_chunks):
            eq_seen, s, bias1 = scores(c, eq_seen)
            parts.append(s + jnp.concatenate([bias1] * h, axis=0))
        s = jnp.concatenate(parts, axis=1)
        m = jnp.maximum(jnp.max(s, axis=-1, keepdims=True), MASK_FLOOR)
        p = jnp.exp(s - m)
        l = jnp.sum(p, axis=-1, keepdims=True)
        pv = _mm_nt if kt else _mm
        acc = pv(p[:, 0:tkc], get_keys(0))
        for c in range(1, n_chunks):
            acc = acc + pv(p[:, c * tkc:(c + 1) * tkc], get_keys(c))
    else:
        def chunk(c, carry):
            eq_seen, s, bias1 = scores(c, carry[0])
            keys = get_keys(c)
            heads = tuple(_online_update(carry[1][hh], s[hh * tq:(hh + 1) * tq] + bias1, None, keys, kt)
                          for hh in range(h))
            return eq_seen, heads

        init = (jnp.zeros((tq, 1), F32), tuple(_flash_init(tq, 256) for _ in range(h)))
        _, heads = lax.fori_loop(0, n_chunks, chunk, init)
        l = jnp.concatenate([hd[1] for hd in heads], axis=0)
        acc = jnp.concatenate([hd[2] for hd in heads], axis=0)
    o = _finish(l, acc)[:, 0:128]
    o = jnp.where(_iota((1, 128), 1) >= D_HEAD, o, 0.0)
    return jnp.concatenate([o[hh * tq:(hh + 1) * tq] for hh in range(h)], axis=1)


def _dsa_prompt_kernel(qc_ref, qi_ref, misc_ref, dk_ref, pqd_ref, pqi_ref, tri_ref, o_ref, key_s,
                       *, tq, tkc, n_top):
    i = pl.program_id(1)
    s0 = i * tq
    q_pos = s0 + _iota((tq, 1), 0)
    n_chunks = (s0 + tq + tkc - 1) // tkc
    o_ref[...] = _dsa_attend(qc_ref[...], qi_ref[...], misc_ref[...], q_pos, dk_ref, n_chunks, tkc, n_top,
                             pqd_ref[...], pqi_ref[...], tri_ref[...], key_s)


def _dsa_prompt(qc, qi, misc, dk, pqd, pqi, tri, nb, t, tq, tkc):
    nq = t // tq
    n_top = min(DSA_TOPK_MAX, t // 4)
    kern = functools.partial(_dsa_prompt_kernel, tq=tq, tkc=tkc, n_top=n_top)
    full = lambda a: pl.BlockSpec(a.shape, lambda b, i: (0, 0))
    return pl.pallas_call(
        kern,
        grid=(nb, nq),
        in_specs=[pl.BlockSpec((tq, 256), lambda b, i: (b * nq + i, 0)),
                  pl.BlockSpec((tq, 256), lambda b, i: (b * nq + i, 0)),
                  pl.BlockSpec((tq, 128), lambda b, i: (b * nq + i, 0)),
                  pl.BlockSpec((t, 256), lambda b, i: (b, 0)),
                  full(pqd), full(pqi), full(tri)],
        out_specs=pl.BlockSpec((tq, 512), lambda b, i: (b * nq + i, 0)),
        out_shape=jax.ShapeDtypeStruct((nb * t, 512), F32),
        scratch_shapes=[pltpu.VMEM((t // tkc, tq, tkc), I32)],
        compiler_params=_cparams(("parallel", "arbitrary"), 48),
        name="dsa_prompt",
    )(qc, qi, misc, dk, pqd, pqi, tri)


def _dsa_sample_kernel(pt_ref, *refs, pps, past, tkc, n_top):
    pages = refs[:pps]
    (qc_ref, qi_ref, misc_ref, newd_ref, pqd_ref, pqi_ref, tri_ref, o_ref, key_t, key_s) = refs[pps:]
    j = pl.program_id(1)
    nsteps = pl.num_programs(1)
    ppc = tkc // PAGE
    n_chunks = past // tkc + 1

    @pl.when(j == 0)
    def _():
        key_t[:, 160:256, :] = jnp.zeros((n_chunks, 96, tkc), key_t.dtype)
        key_t[n_chunks - 1] = jnp.zeros((256, tkc), key_t.dtype)

    @pl.when(j < nsteps - 1)
    def _():
        for p in range(pps):
            c = j * (pps // ppc) + p // ppc
            key_t[c, 0:160, (p % ppc) * PAGE:(p % ppc + 1) * PAGE] = pages[p][...].astype(key_t.dtype)

    @pl.when(j == nsteps - 1)
    def _():
        new_rows = jnp.concatenate([newd_ref[0].astype(F32), jnp.zeros((PAGE - SAMPLE_T, 256), F32)], axis=0)
        key_t[n_chunks - 1, :, 0:PAGE] = new_rows.T.astype(key_t.dtype)
        q_pos = past + _iota((SAMPLE_T, 1), 0)
        o_ref[0] = _dsa_attend(qc_ref[0], qi_ref[0], misc_ref[0], q_pos, key_t, n_chunks, tkc, n_top,
                               pqd_ref[...], pqi_ref[...], tri_ref[...], key_s, kt=True)


def _dsa_sample(page_table, cache_t, layer, qc, qi, misc, newd, pqd, pqi, tri, pps, tkc, ts):
    db, n_pages = page_table.shape
    past = n_pages * PAGE
    assert past % tkc == 0 and pps % (tkc // PAGE) == 0
    nsteps = n_pages // pps + 1
    last = n_pages - 1
    n_chunks = past // tkc + 1
    n_top = min(DSA_TOPK_MAX, (past + ts) // 4)

    def page_spec(p):
        return pl.BlockSpec((None, None, 160, PAGE),
                            lambda b, j, pt: (layer, pt[b, jnp.minimum(j * pps + p, last)], 0, 0))

    full = lambda a: pl.BlockSpec(a.shape, lambda b, j, pt: (0, 0))
    per_b = lambda w: pl.BlockSpec((1, SAMPLE_T, w), lambda b, j, pt: (b, 0, 0))
    kern = functools.partial(_dsa_sample_kernel, pps=pps, past=past, tkc=tkc, n_top=n_top)
    return pl.pallas_call(
        kern,
        grid_spec=pltpu.PrefetchScalarGridSpec(
            num_scalar_prefetch=1,
            grid=(db, nsteps),
            in_specs=[page_spec(p) for p in range(pps)] + [
                per_b(256), per_b(256), per_b(128), per_b(256), full(pqd), full(pqi), full(tri)],
            out_specs=per_b(512),
            scratch_shapes=[pltpu.VMEM((n_chunks, 256, tkc), MXU_DTYPE), pltpu.VMEM((n_chunks, SAMPLE_T, tkc), I32)],
        ),
        out_shape=jax.ShapeDtypeStruct((db, SAMPLE_T, 512), F32),
        compiler_params=_cparams(("parallel", "arbitrary"), 48),
        name="dsa_sample",
    )(page_table, *([cache_t] * pps), qc, qi, misc, newd, pqd, pqi, tri)


def _out_kernel(h_ref, oa_ref, om_ref, od_ref, wa_ref, wm_ref, wd_ref, g_ref, h1_o, xn_o):
    h1 = h_ref[...] + (_mm(oa_ref[...], wa_ref[...]) + _mm(om_ref[...], wm_ref[...]) + _mm(od_ref[...], wd_ref[...]))
    h1_o[...] = h1
    xn = h1 * lax.rsqrt(jnp.mean(h1 * h1, axis=-1, keepdims=True) + EPS) * g_ref[...]
    xn_o[...] = xn.astype(xn_o.dtype)


def _out_proj(h, oa, om, od, wa, wm, wd, g, tm):
    n = h.shape[0]
    row = lambda w: pl.BlockSpec((tm, w), lambda i: (i, 0))
    full = lambda a: pl.BlockSpec(a.shape, lambda i: (0, 0))
    return pl.pallas_call(
        _out_kernel,
        grid=(n // tm,),
        in_specs=[row(D_MODEL), row(512), row(512), row(512), full(wa), full(wm), full(wd), full(g)],
        out_specs=[row(D_MODEL), row(D_MODEL)],
        out_shape=[jax.ShapeDtypeStruct((n, D_MODEL), F32), jax.ShapeDtypeStruct((n, D_MODEL), MXU_DTYPE)],
        compiler_params=_cparams(("parallel",), 40),
        name="out_proj",
    )(h, oa, om, od, wa, wm, wd, g)


def _ffn_kernel(*refs, tm, seq, sample):
    if sample:
        (xn_ref, h1_ref, p_ref, wu_ref, wg_ref, wd_ref, cw_ref, cb_ref, gp_ref, wpg_ref, wpp_ref,
         p1_ref, p2_ref, h_o, a_o, acc_s) = refs
    else:
        (xn_ref, halo_ref, h1_ref, p_ref, wu_ref, wg_ref, wd_ref, cw_ref, cb_ref, gp_ref, wpg_ref, wpp_ref,
         h_o, a_o, acc_s) = refs
    i = pl.program_id(0)
    j = pl.program_id(1)
    xn = xn_ref[...]
    a = _mm(xn, wu_ref[...])
    b = _mm(xn, wg_ref[...])
    a_o[...] = a
    row = _iota((tm, 1), 0)
    r1 = pltpu.roll(a, 1, 0)
    r2 = pltpu.roll(a, 2, 0)
    if sample:
        tpos = row % SAMPLE_T
        prev1 = jnp.where(tpos == 0, p1_ref[...], r1)
        prev2 = jnp.where(tpos < 2, p2_ref[...], r2)
    else:
        ah = _mm(halo_ref[...], wu_ref[...])
        ah = jnp.where((i * tm) % seq == 0, 0.0, ah)
        prev1 = jnp.where(row == 0, ah[7:8, :], r1)
        prev2 = jnp.where(row == 0, ah[6:7, :], jnp.where(row == 1, ah[7:8, :], r2))
    c = cb_ref[...] + cw_ref[2:3, :] * a
    c = c + cw_ref[0:1, :] * prev2
    c = c + cw_ref[1:2, :] * prev1
    u = (c * jax.nn.sigmoid(c)) * b
    y = _mm(u, wd_ref[...])

    @pl.when(j == 0)
    def _():
        acc_s[...] = y

    @pl.when(j > 0)
    def _():
        acc_s[...] = acc_s[...] + y

    @pl.when(j == pl.num_programs(1) - 1)
    def _():
        h2 = h1_ref[...] + acc_s[...]
        xn3 = h2 * lax.rsqrt(jnp.mean(h2 * h2, axis=-1, keepdims=True) + EPS) * gp_ref[...]
        gate = jax.nn.sigmoid(_mm(xn3, wpg_ref[...]))
        h_o[...] = h2 + gate * _mm(p_ref[...], wpp_ref[...])


def _ffn(xn, h1, p, wu, wg, wd, cw, cb, gp, wpg, wpp, tm, tf, seq, prefix=None):
    n = xn.shape[0]
    nf = D_FF // tf
    sample = prefix is not None
    row = lambda w: pl.BlockSpec((tm, w), lambda i, j: (i, 0))
    full = lambda a: pl.BlockSpec(a.shape, lambda i, j: (0, 0))
    ff_col = lambda r: pl.BlockSpec((r, tf), lambda i, j: (0, j))
    in_specs = [row(D_MODEL)]
    args = [xn]
    if not sample:
        in_specs.append(pl.BlockSpec((SUB, D_MODEL), lambda i, j: (jnp.maximum(i * (tm // SUB) - 1, 0), 0)))
        args.append(xn)
    in_specs += [row(D_MODEL), row(PLE_DIM), ff_col(D_MODEL), ff_col(D_MODEL),
                 pl.BlockSpec((tf, D_MODEL), lambda i, j: (j, 0)), ff_col(CONV_W), ff_col(1),
                 full(gp), full(wpg), full(wpp)]
    args += [h1, p, wu, wg, wd, cw, cb, gp, wpg, wpp]
    if sample:
        in_specs += [pl.BlockSpec((tm, tf), lambda i, j: (i, j))] * 2
        args += list(prefix)
    kern = functools.partial(_ffn_kernel, tm=tm, seq=seq, sample=sample)
    return pl.pallas_call(
        kern,
        grid=(n // tm, nf),
        in_specs=in_specs,
        out_specs=[row(D_MODEL), pl.BlockSpec((tm, tf), lambda i, j: (i, j))],
        out_shape=[jax.ShapeDtypeStruct((n, D_MODEL), F32), jax.ShapeDtypeStruct((n, D_FF), F32)],
        scratch_shapes=[pltpu.VMEM((tm, D_MODEL), F32)],
        compiler_params=_cparams(("parallel", "arbitrary"), 56),
        name="ffn",
    )(*args)


def _pick(n, prefs):
    for p in prefs:
        if n % p == 0:
            return p
    return n


def kernel(x_prompt, x_sample, cache_nsa, cache_mla, cache_dsa, state_win, state_conv, page_table, p_prompt, p_sample, norm_mix, w_in, nsa_qn, nsa_kn, nsa_cmp_pos, mla_cqn, mla_ckvn, mla_w_uq, mla_w_ukv, mla_qn, mla_kn, dsa_qn, dsa_kn, w_out, norm_ffn, ffn_w_up, ffn_w_gate, ffn_conv_w, ffn_conv_b, ffn_w_down, norm_ple, ple_w_gate, ple_w_proj):
    depth = w_in.shape[0]
    B, T, _ = x_prompt.shape
    DB, TS, _ = x_sample.shape
    n_pages = page_table.shape[1]
    past = n_pages * PAGE
    assert TS <= 4 and T % 128 == 0 and T >= NSA_WINDOW + 128 and past >= NSA_WINDOW
    mm = MXU_DTYPE
    bf = lambda a: jnp.asarray(a, jnp.bfloat16)

    win_cols, wuq_cols = _win_cols(), _wuq_cols()
    g256 = bf(_blockdiag(256, 64))
    g1024 = bf(_blockdiag(1024, 128))
    g8 = bf((np.arange(1024)[:, None] // 128 == np.arange(128)[None, :]).astype(np.float32))
    pk = bf(_placement(128, 1024, [(j, h * 128 + 64 + j) for h in range(8) for j in range(32)]))
    swc = bf(_placement(256, 256, [(j, (j + 32) % 64) for j in range(64)]))
    pq_c = bf(_q_placement(4, 64, 256, 0))
    pq_s = bf(_q_placement(4, 64, 256, 128))
    pq_w = bf(_q_placement(4, 64, 128, 0))
    pq_d = bf(_q_placement(4, 64, 256, 0))
    pq_i = bf(_q_placement(8, 32, 256, 128))
    tkc = _pick(T, (512, 256, 128))
    tri = bf(np.triu(np.ones((tkc, tkc), np.float32)))
    tab_p = _rope_tables(jnp.arange(T))
    pos_s = past + (jnp.arange(DB * SAMPLE_T) % SAMPLE_T)
    tab_s = _rope_tables(pos_s)
    nblk_p = T // NSA_BLK
    ctab_p = _cmp_tables(nblk_p)
    nbp_s = ((past // NSA_BLK + 1 + LANE - 1) // LANE) * LANE
    ctab_s = _cmp_tables(nbp_s)
    lpad_s = ((past + SAMPLE_T + tkc - 1) // tkc) * tkc
    blk_of = lambda n_keys: np.arange(n_keys) // NSA_BLK
    expand_p = bf((blk_of(T).reshape(T // tkc, 1, tkc) == np.arange(nblk_p)[None, :, None]).astype(np.float32))
    expand_s = bf((blk_of(lpad_s)[None, None, :] == np.arange(nbp_s)[None, :, None]).astype(np.float32))
    exp_rows = np.full((512,), -1, np.int64)
    for h in range(4):
        exp_rows[h * 128 + 64: (h + 1) * 128] = h * 64 + np.arange(64)

    tm_p = _pick(B * T, (256, 128))
    tq = 256
    tq_m = _pick(T, (512, 256, 128))
    tm_o = _pick(B * T, (512, 256, 128))
    tf = D_FF // 2
    ns = DB * SAMPLE_T

    cache_mla_t = jnp.swapaxes(cache_mla, 2, 3)
    cache_dsa_t = jnp.swapaxes(cache_dsa, 2, 3)
    hp = x_prompt.reshape(B * T, D_MODEL)
    hs = jnp.pad(x_sample, ((0, 0), (0, SAMPLE_T - TS), (0, 0))).reshape(ns, D_MODEL)
    outs = {k: [] for k in ("nsa_p", "nsa_s", "mla_p", "mla_s", "dsa_p", "dsa_s", "win_p", "win_s", "conv_p", "conv_s")}
    unpad = lambda a: a.reshape(DB, SAMPLE_T, -1)[:, :TS]

    for i in range(depth):
        wbig = _take_cols(w_in[i], win_cols).astype(mm)
        wuq = _take_cols(mla_w_uq[i], wuq_cols).astype(mm)
        gv = _gain_vector(norm_mix[i], nsa_qn[i], nsa_kn[i], mla_cqn[i], mla_ckvn[i], dsa_qn[i], dsa_kn[i], mla_qn[i])
        ukv = mla_w_ukv[i].reshape(MLA_KV_RANK, MLA_HEADS, MLA_NOPE + MLA_V)
        wk = jnp.pad(ukv[:, :, :MLA_NOPE], ((0, 0), (0, 0), (0, LANE - MLA_NOPE))).reshape(MLA_KV_RANK, 1024).astype(mm)
        wv = ukv[:, :, MLA_NOPE:].reshape(MLA_KV_RANK, 512).astype(mm)
        gk = jnp.tile(jnp.concatenate([mla_kn[i], jnp.zeros((32,), F32)]), 8)[None, :]
        wkvt = jnp.concatenate([ukv[:, :, :MLA_NOPE].reshape(MLA_KV_RANK, 512),
                                ukv[:, :, MLA_NOPE:].reshape(MLA_KV_RANK, 512)], axis=1).T.astype(mm)
        gkc = mla_kn[i][:, None]
        wpos = jnp.concatenate([nsa_cmp_pos[i, 0], nsa_cmp_pos[i, 1], jnp.zeros((NSA_BLK, 128), F32)], axis=1)
        wo = w_out[i]
        expand = lambda w: jnp.where(jnp.asarray(exp_rows >= 0)[:, None],
                                     jnp.take(w, jnp.asarray(np.maximum(exp_rows, 0)), axis=0), 0.0).astype(mm)
        wo_a, wo_m, wo_d = expand(wo[0:256]), wo[256:768].astype(mm), expand(wo[768:1024])
        ffn_w = (ffn_w_up[i].astype(mm), ffn_w_gate[i].astype(mm), ffn_w_down[i].astype(mm), ffn_conv_w[i],
                 ffn_conv_b[i][None, :], norm_ple[i][None, :], ple_w_gate[i].astype(mm), ple_w_proj[i].astype(mm))
        g_ffn = norm_ffn[i][None, :]

        (qa, qm, qc, qi, misc, rn, rm, rd, rw, nk, wkk, dk, kp) = _proj(
            hp, tab_p, T // tm_p, gv, wbig, wuq, g256, g1024, tm_p)
        km, vm = _mla_prep(rm, kp, wk, wv, pk, g8, gk, _pick(B * T, (512, 256, 128)))
        o_m = _mla_flash(qm, km, vm, B, T, tq_m, _pick(T, (512, 256, 128)))
        cmp = _nsa_cmp(rn, wpos, ctab_p, swc, B, T)
        o_a = _nsa_prompt(qa, misc, cmp, nk, wkk, pq_c, pq_s, pq_w, expand_p, B, T, tq, tkc)
        o_d = _dsa_prompt(qc, qi, misc, dk, pq_d, pq_i, tri, B, T, tq, tkc)
        h1, xn2 = _out_proj(hp, o_a, o_m, o_d, wo_a, wo_m, wo_d, g_ffn, tm_o)
        hp, a_p = _ffn(xn2, h1, p_prompt[i].reshape(B * T, PLE_DIM), *ffn_w, tm_o, tf, T)
        outs["nsa_p"].append(rn.reshape(B, T, -1))
        outs["mla_p"].append(rm.reshape(B, T, -1))
        outs["dsa_p"].append(rd.reshape(B, T, -1))
        outs["win_p"].append(rw.reshape(B, T, -1)[:, T - min(NSA_WINDOW, T):])
        outs["conv_p"].append(a_p.reshape(B, T, D_FF)[:, T - (CONV_W - 1):])

        (qa, qm, qc, qi, misc, rn, rm, rd, rw, nk, wkk, dk, kp) = _proj(
            hs, tab_s, 1, gv, wbig, wuq, g256, g1024, _pick(ns, (256, 128)))
        r3 = lambda a: a.reshape(DB, SAMPLE_T, -1)
        pps = _pick(n_pages, (32, 16, 8, 4))
        o_m = _mla_sample(page_table, cache_mla_t, i, r3(qm), r3(rm), r3(kp), wkvt, gkc, wk, wv, pk, g8, gk, pps)
        o_a = _nsa_sample(page_table, cache_nsa, i, r3(qa), r3(misc), r3(rn), state_win, r3(rw), wpos, ctab_s, swc,
                          pq_c, pq_s, pq_w, expand_s, pps)
        o_d = _dsa_sample(page_table, cache_dsa_t, i, r3(qc), r3(qi), r3(misc), r3(dk), pq_d, pq_i, tri, pps, tkc, TS)
        tm_s = _pick(ns, (256, 128))
        h1, xn2 = _out_proj(hs, o_a.reshape(ns, 512), o_m.reshape(ns, 512), o_d.reshape(ns, 512),
                            wo_a, wo_m, wo_d, g_ffn, tm_s)
        sc = state_conv[i]
        zrow = jnp.zeros((DB, SAMPLE_T - 1, D_FF), F32)
        p1 = jnp.concatenate([sc[:, 1:2], zrow], axis=1).reshape(ns, D_FF)
        p2 = jnp.concatenate([sc[:, 0:2], zrow[:, 1:]], axis=1).reshape(ns, D_FF)
        p_s = jnp.pad(p_sample[i], ((0, 0), (0, SAMPLE_T - TS), (0, 0))).reshape(ns, PLE_DIM)
        hs, a_s = _ffn(xn2, h1, p_s, *ffn_w, tm_s, tf, SAMPLE_T, prefix=(p1, p2))
        outs["nsa_s"].append(unpad(rn))
        outs["mla_s"].append(unpad(rm))
        outs["dsa_s"].append(unpad(rd))
        win_all = jnp.concatenate([state_win[i], unpad(rw)], axis=1)
        outs["win_s"].append(win_all[:, TS:])
        a_ext = jnp.concatenate([sc, unpad(a_s)], axis=1)
        outs["conv_s"].append(a_ext[:, TS:])

    st = lambda k: jnp.stack(outs[k])
    return (hp.reshape(B, T, D_MODEL), unpad(hs),
            st("nsa_p"), st("nsa_s"), st("mla_p"), st("mla_s"), st("dsa_p"), st("dsa_s"),
            st("win_p"), st("win_s"), st("conv_p"), st("conv_s"))
```

```python
import functools
import math

import numpy as np
import jax
import jax.numpy as jnp
from jax import lax
from jax.experimental import pallas as pl
from jax.experimental.pallas import tpu as pltpu

F32 = jnp.float32
I32 = jnp.int32
MXU_DTYPE = jnp.bfloat16

D_MODEL = 1024
D_HEAD = 64
ROPE_THETA = 10000.0
EPS = 1e-6
NEG = -1e30
MASK_FLOOR = -1e29
NSA_HEADS = 4
NSA_BLK = 64
NSA_TOPN = 16
NSA_WINDOW = 512
NSA_FORCE = 1e9
MLA_HEADS = 8
MLA_Q_RANK = 256
MLA_KV_RANK = 256
MLA_NOPE = 64
MLA_ROPE = 32
MLA_V = 64
MLA_QK = MLA_NOPE + MLA_ROPE
DSA_HEADS = 4
IDX_HEADS = 8
IDX_DIM = 32
DSA_TOPK_MAX = 256
D_FF = 2816
CONV_W = 3
PLE_DIM = 256
PAGE = 128
LANE = 128
SUB = 8
SAMPLE_T = 8

C_QA, C_KC, C_VC, C_KS, C_VS, C_KW, C_VW, C_GA = 0, 256, 320, 384, 448, 512, 576, 640
C_CQ, C_CKV, C_KPE, C_QC, C_KD, C_VD, C_QI, C_KI, C_WI = 652, 908, 1164, 1196, 1452, 1516, 1580, 1836, 1868

(Z_QA, Z_QAS, Z_NSA, Z_NSAS, Z_WIN, Z_WINS, Z_MISC, Z_CQ, Z_CKV, Z_KPE, Z_KPES, Z_QC, Z_QCS,
 Z_DKV, Z_DKVS, Z_KI, Z_KIS, Z_QI, Z_QIS, Z_TOTAL) = (
    0, 256, 512, 768, 1024, 1152, 1280, 1408, 1664, 1920, 2048, 2176, 2432, 2688, 2816, 2944,
    3072, 3200, 3456, 3712)

(T_64C, T_64S, T_NSAC, T_NSAS, T_KVC, T_KVS, T_QC, T_QS, T_32C, T_32S, T_328C, T_328S, T_TOTAL) = (
    0, 256, 512, 768, 1024, 1152, 1280, 2304, 3328, 3456, 3584, 3840, 4096)

(GV_NMIX, GV_QA_A, GV_QA_B, GV_NSA_A, GV_NSA_B, GV_WIN_A, GV_WIN_B, GV_CQ, GV_CKV, GV_QC_A, GV_QC_B,
 GV_DKV_A, GV_DKV_B, GV_Q, GV_TOTAL) = (
    0, 1024, 1280, 1536, 1792, 2048, 2176, 2304, 2560, 2816, 3072, 3328, 3456, 3584, 4608)


def _swap_idx(base, d, n):
    l = np.arange(n * d)
    return base + (l // d) * d + ((l % d) + d // 2) % d


def _win_cols():
    cols = np.full((Z_TOTAL,), -1, np.int64)

    def put(off, idx, at=0):
        cols[off + at: off + at + len(idx)] = idx

    put(Z_QA, C_QA + np.arange(256))
    put(Z_QAS, _swap_idx(C_QA, 64, 4))
    put(Z_NSA, C_KC + np.arange(256))
    put(Z_NSAS, _swap_idx(C_KS, 64, 1), at=128)
    put(Z_WIN, C_KW + np.arange(128))
    put(Z_WINS, _swap_idx(C_KW, 64, 1))
    put(Z_MISC, C_GA + np.arange(12))
    put(Z_MISC, C_WI + np.arange(8), at=12)
    put(Z_CQ, C_CQ + np.arange(256))
    put(Z_CKV, C_CKV + np.arange(256))
    put(Z_KPE, C_KPE + np.arange(32))
    put(Z_KPES, _swap_idx(C_KPE, 32, 1))
    put(Z_QC, C_QC + np.arange(256))
    put(Z_QCS, _swap_idx(C_QC, 64, 4))
    put(Z_DKV, C_KD + np.arange(128))
    put(Z_DKVS, _swap_idx(C_KD, 64, 1))
    put(Z_KI, C_KI + np.arange(32))
    put(Z_KIS, _swap_idx(C_KI, 32, 1))
    put(Z_QI, C_QI + np.arange(256))
    put(Z_QIS, _swap_idx(C_QI, 32, 8))
    return cols


def _wuq_cols():
    cols = np.full((2 * MLA_HEADS * LANE,), -1, np.int64)
    for h in range(MLA_HEADS):
        cols[h * LANE: h * LANE + MLA_QK] = h * MLA_QK + np.arange(MLA_QK)
        cols[1024 + h * LANE + MLA_NOPE: 1024 + h * LANE + MLA_QK] = _swap_idx(h * MLA_QK + MLA_NOPE, MLA_ROPE, 1)
    return cols


def _take_cols(w, cols):
    g = jnp.take(w, jnp.asarray(np.maximum(cols, 0)), axis=1)
    return jnp.where(jnp.asarray(cols >= 0)[None, :], g, 0.0)


def _blockdiag(n, d):
    i = np.arange(n)
    return (i[:, None] // d == i[None, :] // d).astype(np.float32)


def _placement(n_in, n_out, pairs):
    m = np.zeros((n_in, n_out), np.float32)
    for s, d in pairs:
        m[s, d] = 1.0
    return m


def _q_placement(n_heads, d, width, at):
    pairs = [(h * d + j, h * width + at + j) for h in range(n_heads) for j in range(d)]
    return _placement(n_heads * d, n_heads * width, pairs)


def _rope_tables(pos):
    pos = pos.astype(F32)[:, None]
    P = pos.shape[0]

    def cs(d):
        half = d // 2
        inv = ROPE_THETA ** (-jnp.arange(half, dtype=F32) / half)
        ang = pos * inv[None, :]
        c, s = jnp.cos(ang), jnp.sin(ang)
        return jnp.concatenate([c, c], axis=1), jnp.concatenate([-s, s], axis=1)

    c64, s64 = cs(64)
    c32, s32 = cs(32)
    one = lambda n: jnp.ones((P, n), F32)
    zero = lambda n: jnp.zeros((P, n), F32)
    segs = [
        jnp.tile(c64, (1, 4)), jnp.tile(s64, (1, 4)),
        jnp.concatenate([one(128), c64, one(64)], 1), jnp.concatenate([zero(128), s64, zero(64)], 1),
        jnp.concatenate([c64, one(64)], 1), jnp.concatenate([s64, zero(64)], 1),
        jnp.tile(jnp.concatenate([one(64), c32, one(32)], 1), (1, 8)),
        jnp.tile(jnp.concatenate([zero(64), s32, zero(32)], 1), (1, 8)),
        jnp.concatenate([c32, one(96)], 1), jnp.concatenate([s32, zero(96)], 1),
        jnp.tile(c32, (1, 8)), jnp.tile(s32, (1, 8)),
    ]
    return jnp.concatenate(segs, axis=1)


def _cmp_tables(nbp):
    pos = (jnp.arange(nbp) * NSA_BLK + (NSA_BLK - 1)).astype(F32)[:, None]
    inv = ROPE_THETA ** (-jnp.arange(32, dtype=F32) / 32)
    ang = pos * inv[None, :]
    c, s = jnp.cos(ang), jnp.sin(ang)
    one = jnp.ones((nbp, 192), F32)
    zero = jnp.zeros((nbp, 192), F32)
    return jnp.concatenate([c, c, one, -s, s, zero], axis=1)


def _swap64(g):
    return jnp.concatenate([g[32:], g[:32]])


def _gain_vector(norm_mix, nsa_qn, nsa_kn, mla_cqn, mla_ckvn, dsa_qn, dsa_kn, mla_qn):
    o64, z64 = jnp.ones((64,), F32), jnp.zeros((64,), F32)
    qpad = jnp.concatenate([mla_qn, jnp.zeros((32,), F32)])
    segs = [
        norm_mix,
        jnp.tile(nsa_qn, 4), jnp.tile(_swap64(nsa_qn), 4),
        jnp.concatenate([nsa_kn[0], o64, nsa_kn[1], o64]), jnp.concatenate([z64, z64, _swap64(nsa_kn[1]), z64]),
        jnp.concatenate([nsa_kn[2], o64]), jnp.concatenate([_swap64(nsa_kn[2]), z64]),
        mla_cqn, mla_ckvn,
        jnp.tile(dsa_qn, 4), jnp.tile(_swap64(dsa_qn), 4),
        jnp.concatenate([dsa_kn, o64]), jnp.concatenate([_swap64(dsa_kn), z64]),
        jnp.tile(qpad, 8),
    ]
    return jnp.concatenate(segs)[None, :]


def _mm(a, b):
    return jnp.dot(a.astype(MXU_DTYPE), b.astype(MXU_DTYPE), preferred_element_type=F32)


def _mm_nt(a, b):
    return lax.dot_general(a.astype(MXU_DTYPE), b.astype(MXU_DTYPE), (((1,), (1,)), ((), ())),
                           preferred_element_type=F32)


def _split3(x):
    x1 = x.astype(jnp.bfloat16)
    r1 = x - x1.astype(F32)
    x2 = r1.astype(jnp.bfloat16)
    x3 = (r1 - x2.astype(F32)).astype(jnp.bfloat16)
    return x1, x2, x3


def _place(x, p):
    x1, x2, x3 = _split3(x)
    d = lambda a: jnp.dot(a, p, preferred_element_type=F32)
    return d(x1) + d(x2) + d(x3)


def _gsum(x2, g):
    hi = x2.astype(jnp.bfloat16)
    lo = (x2 - hi.astype(F32)).astype(jnp.bfloat16)
    return jnp.dot(hi, g, preferred_element_type=F32) + jnp.dot(lo, g, preferred_element_type=F32)


def _iota(shape, dim):
    return lax.broadcasted_iota(I32, shape, dim)


def _stack_heads(x, n, w):
    return jnp.concatenate([x[:, h * w:(h + 1) * w] for h in range(n)], axis=0)


def _softmax_rows(s, mask):
    s = jnp.where(mask, s, NEG)
    m = jnp.max(s, axis=-1, keepdims=True)
    e = jnp.where(mask, jnp.exp(s - m), 0.0)
    l = jnp.sum(e, axis=-1, keepdims=True)
    return e / jnp.where(l > 0.0, l, 1.0)


def _online_update(carry, s, mask, v, vt=False):
    m, l, acc = carry
    if mask is not None:
        s = jnp.where(mask, s, NEG)
    m_new = jnp.maximum(m, jnp.max(s, axis=-1, keepdims=True))
    alpha = jnp.exp(m - m_new)
    p = jnp.exp(s - m_new)
    if mask is not None:
        p = jnp.where(mask, p, 0.0)
    l = alpha * l + jnp.sum(p, axis=-1, keepdims=True)
    acc = alpha * acc + (_mm_nt(p, v) if vt else _mm(p, v))
    return m_new, l, acc


def _flash_init(rows, width):
    return (jnp.full((rows, 1), MASK_FLOOR, F32), jnp.zeros((rows, 1), F32), jnp.zeros((rows, width), F32))


def _finish(l, acc):
    return acc / jnp.where(l > 0.0, l, 1.0)


def _sortable(x):
    b = lax.bitcast_convert_type(x, I32)
    return jnp.where(b < 0, b ^ jnp.int32(0x7FFFFFFF), b)


def _cparams(sem, vmem_mb):
    return pltpu.CompilerParams(dimension_semantics=sem, vmem_limit_bytes=vmem_mb * 1024 * 1024)


def _proj_kernel(x_ref, tab_ref, gv_ref, w_ref, wuq_ref, g256_ref, g1024_ref,
                 qa_o, qm_o, qc_o, qi_o, misc_o, rn_o, rm_o, rd_o, rw_o, nk_o, wk_o, dk_o, kp_o):
    gv = lambda off, w: gv_ref[:, off:off + w]
    tab = lambda off, w: tab_ref[:, off:off + w]
    x = x_ref[...]
    xn = x * lax.rsqrt(jnp.mean(x * x, axis=-1, keepdims=True) + EPS) * gv(GV_NMIX, D_MODEL)
    z = _mm(xn, w_ref[...])
    zs = lambda off, w: z[:, off:off + w]
    g256 = g256_ref[...]
    g128 = g256_ref[0:128, 0:128]

    def group(zo, zso, w, gmat, d, ga, gb, tc, ts, normmask=None):
        a, asw = zs(zo, w), zs(zso, w)
        if gmat is None:
            rs = None
        else:
            rs = lax.rsqrt(_gsum(a * a, gmat) / d + EPS)
            if normmask is not None:
                rs = jnp.where(normmask, rs, 1.0)
        ca, cb = tab(tc, w), tab(ts, w)
        if ga is not None:
            ca, cb = gv(ga, w) * ca, gv(gb, w) * cb
        if rs is not None:
            ca, cb = rs * ca, rs * cb
        return a * ca + asw * cb

    qa_o[...] = group(Z_QA, Z_QAS, 256, g256, 64.0, GV_QA_A, GV_QA_B, T_64C, T_64S).astype(qa_o.dtype)
    lane = _iota((1, 256), 1)
    nm = (lane < 64) | ((lane >= 128) & (lane < 192))
    rn = group(Z_NSA, Z_NSAS, 256, g256, 64.0, GV_NSA_A, GV_NSA_B, T_NSAC, T_NSAS, nm)
    rn_o[...] = rn
    nk_o[...] = rn.astype(nk_o.dtype)
    nm128 = _iota((1, 128), 1) < 64
    rw = group(Z_WIN, Z_WINS, 128, g128, 64.0, GV_WIN_A, GV_WIN_B, T_KVC, T_KVS, nm128)
    rw_o[...] = rw
    wk_o[...] = rw.astype(wk_o.dtype)
    zm = zs(Z_MISC, 128)
    misc_o[...] = jnp.where(_iota((1, 128), 1) < 12, jax.nn.sigmoid(zm), zm * (IDX_HEADS ** -0.5))
    cq = zs(Z_CQ, 256)
    cqn = cq * lax.rsqrt(jnp.mean(cq * cq, axis=-1, keepdims=True) + EPS) * gv(GV_CQ, 256)
    q2 = _mm(cqn, wuq_ref[...])
    qr = q2[:, 0:1024] * tab(T_QC, 1024) + q2[:, 1024:2048] * tab(T_QS, 1024)
    rs = lax.rsqrt(_gsum(qr * qr, g1024_ref[...]) / float(MLA_QK) + EPS)
    qm_o[...] = (qr * rs * gv(GV_Q, 1024)).astype(qm_o.dtype)
    ckv = zs(Z_CKV, 256)
    rm_o[:, 0:256] = ckv * lax.rsqrt(jnp.mean(ckv * ckv, axis=-1, keepdims=True) + EPS) * gv(GV_CKV, 256)
    kp = group(Z_KPE, Z_KPES, 128, None, 0.0, None, None, T_32C, T_32S)
    rm_o[:, 256:288] = kp[:, 0:32]
    kp_o[...] = kp
    qc_o[...] = group(Z_QC, Z_QCS, 256, g256, 64.0, GV_QC_A, GV_QC_B, T_64C, T_64S).astype(qc_o.dtype)
    dkv = group(Z_DKV, Z_DKVS, 128, g128, 64.0, GV_DKV_A, GV_DKV_B, T_KVC, T_KVS, nm128)
    ki = group(Z_KI, Z_KIS, 128, None, 0.0, None, None, T_32C, T_32S)
    rd_o[:, 0:128] = dkv
    rd_o[:, 128:160] = ki[:, 0:32]
    dk_o[:, 0:128] = dkv.astype(dk_o.dtype)
    dk_o[:, 128:256] = ki.astype(dk_o.dtype)
    qi_o[...] = group(Z_QI, Z_QIS, 256, None, 0.0, None, None, T_328C, T_328S).astype(qi_o.dtype)


def _proj(x, tab, n_pos_tiles, gv, wbig, wuq, g256, g1024, tm):
    n = x.shape[0]
    row = lambda w: pl.BlockSpec((tm, w), lambda i: (i, 0))
    full = lambda a: pl.BlockSpec(a.shape, lambda i: (0, 0))
    widths = [(256, MXU_DTYPE), (1024, MXU_DTYPE), (256, MXU_DTYPE), (256, MXU_DTYPE), (128, F32),
              (256, F32), (288, F32), (160, F32), (128, F32),
              (256, MXU_DTYPE), (128, MXU_DTYPE), (256, MXU_DTYPE), (128, F32)]
    return pl.pallas_call(
        _proj_kernel,
        grid=(n // tm,),
        in_specs=[row(D_MODEL),
                  pl.BlockSpec((tm, T_TOTAL), lambda i: (i % n_pos_tiles, 0)),
                  full(gv), full(wbig), full(wuq), full(g256), full(g1024)],
        out_specs=[row(w) for w, _ in widths],
        out_shape=[jax.ShapeDtypeStruct((n, w), dt) for w, dt in widths],
        compiler_params=_cparams(("parallel",), 56),
        name="proj",
    )(x, tab, gv, wbig, wuq, g256, g1024)


def _mla_keys(ckv, kpe128, wk, wv, pk, g8, gk):
    kraw = _mm(ckv, wk) + _place(kpe128, pk)
    ss = _gsum(kraw * kraw, g8)
    rs = lax.rsqrt(ss / float(MLA_QK) + EPS)
    parts = [kraw[:, h * LANE:(h + 1) * LANE] * rs[:, h:h + 1] for h in range(MLA_HEADS)]
    km = jnp.concatenate(parts, axis=1) * gk
    return km.astype(MXU_DTYPE), _mm(ckv, wv).astype(MXU_DTYPE)


def _mla_prep_kernel(rm_ref, kp_ref, wk_ref, wv_ref, wvt_ref, pk_ref, g8_ref, gk_ref, km_o, vt_o):
    ckv = rm_ref[:, 0:256]
    km, _ = _mla_keys(ckv, kp_ref[...], wk_ref[...], wv_ref[...], pk_ref[...], g8_ref[...], gk_ref[...])
    km_o[...] = km
    vt_o[0] = _mm_nt(wvt_ref[...], ckv).astype(vt_o.dtype)


def _mla_prep(rm, kp, wk, wv, wvt, pk, g8, gk, tk):
    n = rm.shape[0]
    full = lambda a: pl.BlockSpec(a.shape, lambda i: (0, 0))
    return pl.pallas_call(
        _mla_prep_kernel,
        grid=(n // tk,),
        in_specs=[pl.BlockSpec((tk, 288), lambda i: (i, 0)), pl.BlockSpec((tk, 128), lambda i: (i, 0)),
                  full(wk), full(wv), full(wvt), full(pk), full(g8), full(gk)],
        out_specs=[pl.BlockSpec((tk, 1024), lambda i: (i, 0)), pl.BlockSpec((1, 512, tk), lambda i: (i, 0, 0))],
        out_shape=[jax.ShapeDtypeStruct((n, 1024), MXU_DTYPE), jax.ShapeDtypeStruct((n // tk, 512, tk), MXU_DTYPE)],
        compiler_params=_cparams(("parallel",), 40),
        name="mla_prep",
    )(rm, kp, wk, wv, wvt, pk, g8, gk)


def _mla_flash_kernel(q_ref, k_ref, vt_ref, o_ref, *, tq, tk, hps):
    i = pl.program_id(2)
    scale = MLA_QK ** -0.5
    q_pos = i * tq + _iota((1, tq), 1)
    qs = [q_ref[:, hq * LANE:(hq + 1) * LANE] for hq in range(hps)]

    def chunk(c, carry, masked):
        k0 = pl.multiple_of(c * tk, tk)
        mask = ((k0 + _iota((tk, 1), 0)) <= q_pos) if masked else None
        heads = range(hps)
        sts = [_mm_nt(k_ref[pl.ds(k0, tk), hq * LANE:(hq + 1) * LANE], qs[hq]) * scale for hq in heads]
        if masked:
            sts = [jnp.where(mask, st, NEG) for st in sts]
        m_new = [jnp.maximum(carry[hq][0], jnp.max(sts[hq], axis=0, keepdims=True)) for hq in heads]
        ps = [jnp.exp(sts[hq] - m_new[hq]) for hq in heads]
        if masked:
            ps = [jnp.where(mask, p, 0.0) for p in ps]
        out = []
        for hq in heads:
            m, l, acc = carry[hq]
            alpha = jnp.exp(m - m_new[hq])
            vt = vt_ref[c, (hq // 2) * LANE:(hq // 2 + 1) * LANE, :]
            out.append((m_new[hq], alpha * l + jnp.sum(ps[hq], axis=0, keepdims=True),
                        alpha * acc + _mm(vt, ps[hq])))
        return tuple(out)

    one = (jnp.full((1, tq), NEG, F32), jnp.zeros((1, tq), F32), jnp.zeros((LANE, tq), F32))
    n_full = (i * tq) // tk
    n_all = ((i + 1) * tq + tk - 1) // tk
    carry = lax.fori_loop(0, n_full, lambda c, cr: chunk(c, cr, False), (one,) * hps)
    carry = lax.fori_loop(n_full, n_all, lambda c, cr: chunk(c, cr, True), carry)
    outs = [_finish(cr[1], cr[2]) for cr in carry]
    for pair in range(hps // 2):
        ot = jnp.where(_iota((LANE, 1), 0) < MLA_V, outs[2 * pair], outs[2 * pair + 1])
        o_ref[:, pair * LANE:(pair + 1) * LANE] = ot.T


def _mla_flash(qm, km, vt, nb, t, tq, tk, hps):
    nq = t // tq
    nc = t // tk
    kern = functools.partial(_mla_flash_kernel, tq=tq, tk=tk, hps=hps)
    return pl.pallas_call(
        kern,
        grid=(nb, MLA_HEADS // hps, nq),
        in_specs=[pl.BlockSpec((tq, hps * LANE), lambda b, h, i: (b * nq + i, h)),
                  pl.BlockSpec((t, hps * LANE), lambda b, h, i: (b, h)),
                  pl.BlockSpec((nc, hps * MLA_V, tk), lambda b, h, i: (b, h, 0))],
        out_specs=pl.BlockSpec((tq, hps * MLA_V), lambda b, h, i: (b * nq + i, h)),
        out_shape=jax.ShapeDtypeStruct((nb * t, 512), F32),
        compiler_params=_cparams(("parallel", "parallel", "arbitrary"), 56),
        name="mla_flash",
    )(qm, km, vt)


def _mla_sample_kernel(pt_ref, *refs, pps, past):
    pages = refs[:pps]
    (q_ref, newm_ref, newkp_ref, wkvt_ref, gkc_ref, wk_ref, wv_ref, pk_ref, g8_ref, gk_ref, o_ref,
     m_s, l_s, acc_s) = refs[pps:]
    j = pl.program_id(1)
    nsteps = pl.num_programs(1)
    rows = SAMPLE_T * MLA_HEADS
    q = q_ref[0].astype(F32)
    head_of_lane = _iota((SUB, 1024), 1) // LANE
    sub = _iota((SUB, 1024), 0)
    qbd = jnp.concatenate(
        [jnp.where(head_of_lane == sub, jnp.broadcast_to(q[t:t + 1, :], (SUB, 1024)), 0.0)
         for t in range(SAMPLE_T)], axis=0).astype(MXU_DTYPE)
    q_pos = past + _iota((rows, 1), 0) // MLA_HEADS
    scale = MLA_QK ** -0.5

    @pl.when(j == 0)
    def _():
        m_s[...] = jnp.full(m_s.shape, NEG, F32)
        l_s[...] = jnp.zeros(l_s.shape, F32)
        acc_s[...] = jnp.zeros(acc_s.shape, F32)

    def update(s, mask, v, vt):
        m, l, acc = _online_update((m_s[...], l_s[...], acc_s[...]), s, mask, v, vt)
        m_s[...] = m
        l_s[...] = l
        acc_s[...] = acc

    @pl.when(j < nsteps - 1)
    def _():
        ppg = min(pps, 8)
        n = ppg * PAGE
        s_parts, v_parts = [], []
        for g in range(pps // ppg):
            xt = jnp.concatenate([pages[g * ppg + p][...] for p in range(ppg)], axis=1)
            kpet = xt[256:288]
            kv = jnp.dot(wkvt_ref[...], xt[0:256].astype(MXU_DTYPE), preferred_element_type=F32)
            pe2 = jnp.sum(kpet * kpet, axis=0, keepdims=True)
            parts = []
            for h in range(MLA_HEADS):
                kn = kv[h * MLA_NOPE:(h + 1) * MLA_NOPE]
                rs = lax.rsqrt((jnp.sum(kn * kn, axis=0, keepdims=True) + pe2) / float(MLA_QK) + EPS)
                parts += [(kn * rs) * gkc_ref[0:MLA_NOPE, :], (kpet * rs) * gkc_ref[MLA_NOPE:MLA_QK, :],
                          jnp.zeros((LANE - MLA_QK, n), F32)]
            kmt = jnp.concatenate(parts, axis=0).astype(MXU_DTYPE)
            s_parts.append(jnp.dot(qbd, kmt, preferred_element_type=F32) * scale)
            v_parts.append(kv[512:1024].astype(MXU_DTYPE))
        s = jnp.concatenate(s_parts, axis=1)
        m_new = jnp.maximum(m_s[...], jnp.max(s, axis=-1, keepdims=True))
        alpha = jnp.exp(m_s[...] - m_new)
        p = jnp.exp(s - m_new)
        acc = alpha * acc_s[...]
        for g, vmt in enumerate(v_parts):
            acc = acc + _mm_nt(p[:, g * n:(g + 1) * n], vmt)
        m_s[...] = m_new
        l_s[...] = alpha * l_s[...] + jnp.sum(p, axis=-1, keepdims=True)
        acc_s[...] = acc

    @pl.when(j == nsteps - 1)
    def _():
        npad = 2 * SAMPLE_T
        ckv = jnp.concatenate([newm_ref[0][:, 0:256], jnp.zeros((npad - SAMPLE_T, 256), F32)], axis=0)
        kpe = jnp.concatenate([newkp_ref[0], jnp.zeros((npad - SAMPLE_T, 128), F32)], axis=0)
        km, vm = _mla_keys(ckv, kpe, wk_ref[...], wv_ref[...], pk_ref[...], g8_ref[...], gk_ref[...])
        update(_mm_nt(qbd, km) * scale, (past + _iota((1, npad), 1)) <= q_pos, vm, False)
        o = _finish(l_s[...], acc_s[...])
        lane_head = _iota((SUB, 512), 1) // MLA_V
        sub8 = _iota((SUB, 512), 0)
        out = jnp.zeros((SAMPLE_T, 512), F32)
        for t in range(SAMPLE_T):
            blk = jnp.where(lane_head == sub8, o[t * SUB:(t + 1) * SUB, :], 0.0)
            r = jnp.sum(blk, axis=0, keepdims=True)
            out = jnp.where(_iota((SAMPLE_T, 512), 0) == t, jnp.broadcast_to(r, (SAMPLE_T, 512)), out)
        o_ref[0] = out


def _mla_sample(page_table, cache_t, layer, qm, newm, newkp, wkvt, gkc, wk, wv, pk, g8, gk, pps):
    db, n_pages = page_table.shape
    past = n_pages * PAGE
    nsteps = n_pages // pps + 1
    last = n_pages - 1

    def page_spec(p):
        return pl.BlockSpec((None, None, 288, PAGE),
                            lambda b, j, pt: (layer, pt[b, jnp.minimum(j * pps + p, last)], 0, 0))

    full = lambda a: pl.BlockSpec(a.shape, lambda b, j, pt: (0, 0))
    per_b = lambda w: pl.BlockSpec((1, SAMPLE_T, w), lambda b, j, pt: (b, 0, 0))
    rows = SAMPLE_T * MLA_HEADS
    kern = functools.partial(_mla_sample_kernel, pps=pps, past=past)
    return pl.pallas_call(
        kern,
        grid_spec=pltpu.PrefetchScalarGridSpec(
            num_scalar_prefetch=1,
            grid=(db, nsteps),
            in_specs=[page_spec(p) for p in range(pps)] + [per_b(1024), per_b(288), per_b(128), full(wkvt), full(gkc),
                                                          full(wk), full(wv), full(pk), full(g8), full(gk)],
            out_specs=per_b(512),
            scratch_shapes=[pltpu.VMEM((rows, 1), F32), pltpu.VMEM((rows, 1), F32), pltpu.VMEM((rows, 512), F32)],
        ),
        out_shape=jax.ShapeDtypeStruct((db, SAMPLE_T, 512), F32),
        compiler_params=_cparams(("parallel", "arbitrary"), 40),
        name="mla_sample",
    )(page_table, *([cache_t] * pps), qm, newm, newkp, wkvt, gkc, wk, wv, pk, g8, gk)


def _cmp_blocks(rows, wpos):
    n = rows.shape[0] // NSA_BLK
    x = rows.reshape(n, NSA_BLK, 256) * wpos[None]
    return jnp.sum(x, axis=1) / float(NSA_BLK)


def _cmp_rope(craw, ctab, swc):
    return craw * ctab[:, 0:256] + _place(craw, swc) * ctab[:, 256:512]


def _nsa_cmp_kernel(rn_ref, wpos_ref, ctab_ref, swc_ref, o_ref):
    craw = _cmp_blocks(rn_ref[...], wpos_ref[...])
    o_ref[...] = _cmp_rope(craw, ctab_ref[...], swc_ref[...]).astype(o_ref.dtype)


def _nsa_cmp(rn, wpos, ctab, swc, nb, t):
    nblk = t // NSA_BLK
    tb = min(nblk, 16)
    nt = nblk // tb
    return pl.pallas_call(
        _nsa_cmp_kernel,
        grid=(nb, nt),
        in_specs=[pl.BlockSpec((tb * NSA_BLK, 256), lambda b, i: (b * nt + i, 0)),
                  pl.BlockSpec(wpos.shape, lambda b, i: (0, 0)),
                  pl.BlockSpec((tb, 512), lambda b, i: (i, 0)),
                  pl.BlockSpec(swc.shape, lambda b, i: (0, 0))],
        out_specs=pl.BlockSpec((tb, 256), lambda b, i: (b * nt + i, 0)),
        out_shape=jax.ShapeDtypeStruct((nb * nblk, 256), MXU_DTYPE),
        compiler_params=_cparams(("parallel", "parallel"), 32),
        name="nsa_cmp",
    )(rn, wpos, ctab, swc)


def _top_n_mask(x, n_sel):
    t, w = x.shape
    lane = _iota((t, w), 1).astype(F32)

    def body(_, carry):
        x, sel = carry
        m = jnp.max(x, axis=-1, keepdims=True)
        first = jnp.min(jnp.where(x == m, lane, float(w)), axis=-1, keepdims=True)
        hit = lane == first
        return jnp.where(hit, -jnp.inf, x), jnp.where(hit, 1.0, sel)

    _, sel = lax.fori_loop(0, n_sel, body, (x, jnp.zeros((t, w), F32)))
    return sel


def _nsa_attend(qa, misc, q_pos, cmp, nblk, key_ref, n_chunks, tkc, win, w_pos, pqc, pqs, pqw, expand_ref):
    tq = qa.shape[0]
    h = NSA_HEADS
    scale = D_HEAD ** -0.5
    nbp = cmp.shape[0]
    qp4 = jnp.concatenate([q_pos] * h, axis=0)
    qc_st = _stack_heads(_mm(qa, pqc), h, 256)
    blk = _iota((1, nbp), 1)
    s = _mm_nt(qc_st, cmp) * scale
    p_c = _softmax_rows(s, ((blk * NSA_BLK + (NSA_BLK - 1)) <= qp4) & (blk < nblk))
    o_c = _mm(p_c, cmp)
    imp = p_c[0:tq]
    for hh in range(1, h):
        imp = imp + p_c[hh * tq:(hh + 1) * tq]
    cur = q_pos // NSA_BLK
    imp = jnp.where(blk == cur, NSA_FORCE, jnp.where(blk < cur, imp, -1.0))
    imp = jnp.where(blk < nblk, imp, -jnp.inf)
    sel = _top_n_mask(imp, min(NSA_TOPN, nblk))
    qs_st = _stack_heads(_mm(qa, pqs), h, 256) * scale
    def scores(c):
        k0 = c * tkc if isinstance(c, int) else pl.multiple_of(c * tkc, tkc)
        keys = key_ref[pl.ds(k0, tkc), :]
        tok = _mm(sel, expand_ref[c])
        bias1 = jnp.where((tok > 0.5) & ((k0 + _iota((1, tkc), 1)) <= q_pos), 0.0, NEG)
        return _mm_nt(qs_st, keys), bias1, keys

    if isinstance(n_chunks, int):
        def chunk(c, carry):
            s, bias1, keys = scores(c)
            return _online_update(carry, s + jnp.concatenate([bias1] * h, axis=0), None, keys)

        _, l, acc = _loop(n_chunks, chunk, _flash_init(h * tq, 256))
    else:
        def chunk(c, carry):
            s, bias1, keys = scores(c)
            return tuple(_online_update(carry[hh], s[hh * tq:(hh + 1) * tq] + bias1, None, keys) for hh in range(h))

        heads = lax.fori_loop(0, n_chunks, chunk, tuple(_flash_init(tq, 256) for _ in range(h)))
        l = jnp.concatenate([hd[1] for hd in heads], axis=0)
        acc = jnp.concatenate([hd[2] for hd in heads], axis=0)
    o_s = _finish(l, acc)
    qw_st = _stack_heads(_mm(qa, pqw), h, 128)
    rel = qp4 - w_pos
    s = _mm_nt(qw_st, win) * scale
    p_w = _softmax_rows(s, (rel >= 0) & (rel < NSA_WINDOW) & (w_pos >= 0))
    o_w = _mm(p_w, win)
    gate = lambda jj: jnp.concatenate([misc[:, 3 * hh + jj:3 * hh + jj + 1] for hh in range(h)], axis=0)
    tsum = gate(0) * o_c[:, 0:128] + gate(1) * o_s[:, 128:256] + gate(2) * o_w
    tsum = jnp.where(_iota((1, 128), 1) >= D_HEAD, tsum, 0.0)
    return jnp.concatenate([tsum[hh * tq:(hh + 1) * tq] for hh in range(h)], axis=1)


def _nsa_prompt_kernel(qa_ref, misc_ref, cmp_ref, nk_ref, wk_ref, pqc_ref, pqs_ref, pqw_ref, exp_ref, o_ref,
                       *, tq, tkc, t):
    i = pl.program_id(1)
    s0 = i * tq
    q_pos = s0 + _iota((tq, 1), 0)
    w = NSA_WINDOW + tq
    kstart = pl.multiple_of(jnp.clip(s0 - NSA_WINDOW, 0, t - w), SUB)
    win = wk_ref[pl.ds(kstart, w), :]
    w_pos = kstart + _iota((1, w), 1)
    n_chunks = (s0 + tq + tkc - 1) // tkc
    o_ref[...] = _nsa_attend(qa_ref[...], misc_ref[...], q_pos, cmp_ref[...], t // NSA_BLK, nk_ref, n_chunks,
                             tkc, win, w_pos, pqc_ref[...], pqs_ref[...], pqw_ref[...], exp_ref)


def _nsa_prompt(qa, misc, cmp, nk, wk, pqc, pqs, pqw, expand, nb, t, tq, tkc):
    nq = t // tq
    nblk = t // NSA_BLK
    kern = functools.partial(_nsa_prompt_kernel, tq=tq, tkc=tkc, t=t)
    full = lambda a: pl.BlockSpec(a.shape, lambda b, i: (0, 0))
    return pl.pallas_call(
        kern,
        grid=(nb, nq),
        in_specs=[pl.BlockSpec((tq, 256), lambda b, i: (b * nq + i, 0)),
                  pl.BlockSpec((tq, 128), lambda b, i: (b * nq + i, 0)),
                  pl.BlockSpec((nblk, 256), lambda b, i: (b, 0)),
                  pl.BlockSpec((t, 256), lambda b, i: (b, 0)),
                  pl.BlockSpec((t, 128), lambda b, i: (b, 0)),
                  full(pqc), full(pqs), full(pqw), pl.BlockSpec(expand.shape, lambda b, i: (0, 0, 0))],
        out_specs=pl.BlockSpec((tq, 512), lambda b, i: (b * nq + i, 0)),
        out_shape=jax.ShapeDtypeStruct((nb * t, 512), F32),
        compiler_params=_cparams(("parallel", "arbitrary"), 48),
        name="nsa_prompt",
    )(qa, misc, cmp, nk, wk, pqc, pqs, pqw, expand)


def _nsa_sample_kernel(pt_ref, *refs, pps, past):
    pages = refs[:pps]
    (qa_ref, misc_ref, newn_ref, sw_ref, neww_ref, wpos_ref, ctab_ref, swc_ref, pqc_ref, pqs_ref, pqw_ref, exp_ref,
     o_ref, key_s, cmp_s, win_s) = refs[pps:]
    j = pl.program_id(1)
    nsteps = pl.num_programs(1)
    nblk = past // NSA_BLK + 1
    bps = pps * PAGE // NSA_BLK
    lpad = key_s.shape[0]

    @pl.when(j == 0)
    def _():
        key_s[past:lpad, :] = jnp.zeros((lpad - past, 256), key_s.dtype)
        cmp_s[...] = jnp.zeros(cmp_s.shape, F32)

    @pl.when(j < nsteps - 1)
    def _():
        rows = jnp.concatenate([pages[p][...] for p in range(pps)], axis=0)
        r0 = pl.multiple_of(j * pps * PAGE, pps * PAGE)
        key_s[pl.ds(r0, pps * PAGE), :] = rows.astype(key_s.dtype)
        b0 = pl.multiple_of(j * bps, bps)
        cmp_s[pl.ds(b0, bps), :] = _cmp_blocks(rows, wpos_ref[...])

    @pl.when(j == nsteps - 1)
    def _():
        newn = newn_ref[0]
        tail = jnp.concatenate([newn, jnp.zeros((NSA_BLK - SAMPLE_T, 256), F32)], axis=0)
        key_s[past:past + NSA_BLK, :] = tail.astype(key_s.dtype)
        cmp_s[past // NSA_BLK: past // NSA_BLK + 1, :] = _cmp_blocks(tail, wpos_ref[...])
        cmp = _cmp_rope(cmp_s[...], ctab_ref[...], swc_ref[...]).astype(MXU_DTYPE)
        win_s[0:NSA_WINDOW, :] = sw_ref[0].astype(win_s.dtype)
        win_s[NSA_WINDOW:NSA_WINDOW + 2 * SAMPLE_T, :] = jnp.concatenate(
            [neww_ref[0], jnp.zeros((SAMPLE_T, 128), F32)], axis=0).astype(win_s.dtype)
        w = NSA_WINDOW + 2 * SAMPLE_T
        w_pos = past - NSA_WINDOW + _iota((1, w), 1)
        q_pos = past + _iota((SAMPLE_T, 1), 0)
        o_ref[0] = _nsa_attend(qa_ref[0], misc_ref[0], q_pos, cmp, nblk, key_s, 1, lpad, win_s[...], w_pos,
                               pqc_ref[...], pqs_ref[...], pqw_ref[...], exp_ref)


def _nsa_sample(page_table, cache, layer, qa, misc, newn, state_win, neww, wpos, ctab, swc, pqc, pqs, pqw,
                expand, pps):
    db, n_pages = page_table.shape
    past = n_pages * PAGE
    nsteps = n_pages // pps + 1
    last = n_pages - 1
    nbp = ctab.shape[0]
    lpad = expand.shape[2]

    def page_spec(p):
        return pl.BlockSpec((None, None, PAGE, 256),
                            lambda b, j, pt: (layer, pt[b, jnp.minimum(j * pps + p, last)], 0, 0))

    full = lambda a: pl.BlockSpec(a.shape, lambda b, j, pt: (0, 0))
    per_b = lambda w: pl.BlockSpec((1, SAMPLE_T, w), lambda b, j, pt: (b, 0, 0))
    kern = functools.partial(_nsa_sample_kernel, pps=pps, past=past)
    return pl.pallas_call(
        kern,
        grid_spec=pltpu.PrefetchScalarGridSpec(
            num_scalar_prefetch=1,
            grid=(db, nsteps),
            in_specs=[page_spec(p) for p in range(pps)] + [
                per_b(256), per_b(128), per_b(256),
                pl.BlockSpec((None, 1, NSA_WINDOW, 128), lambda b, j, pt: (layer, b, 0, 0)),
                per_b(128), full(wpos), full(ctab), full(swc), full(pqc), full(pqs), full(pqw),
                pl.BlockSpec(expand.shape, lambda b, j, pt: (0, 0, 0))],
            out_specs=per_b(512),
            scratch_shapes=[pltpu.VMEM((lpad, 256), MXU_DTYPE), pltpu.VMEM((nbp, 256), F32),
                            pltpu.VMEM((NSA_WINDOW + 2 * SAMPLE_T, 128), MXU_DTYPE)],
        ),
        out_shape=jax.ShapeDtypeStruct((db, SAMPLE_T, 512), F32),
        compiler_params=_cparams(("parallel", "arbitrary"), 48),
        name="nsa_sample",
    )(page_table, *([cache] * pps), qa, misc, newn, state_win, neww, wpos, ctab, swc, pqc, pqs, pqw, expand)


def _loop(n, body, init):
    if isinstance(n, int):
        for c in range(n):
            init = body(c, init)
        return init
    return lax.fori_loop(0, n, body, init)


def _dsa_attend(qc, qi, misc, q_pos, key_ref, n_chunks, tkc, n_top, pqd, pqi, tri, key_s, kt=False):
    tq = qc.shape[0]
    h = DSA_HEADS
    qi_st = _stack_heads(_mm(qi, pqi), IDX_HEADS, 256)
    wcol = [misc[:, 12 + g:13 + g] for g in range(IDX_HEADS)]
    qk = _mm if kt else _mm_nt

    def get_keys(c):
        if kt:
            return key_ref[c]
        return key_ref[pl.ds(c * tkc if isinstance(c, int) else pl.multiple_of(c * tkc, tkc), tkc), :]

    def score_chunk(c, _):
        k0 = c * tkc
        rel = jnp.maximum(qk(qi_st, get_keys(c)), 0.0)
        sc = wcol[0] * rel[0:tq]
        for g in range(1, IDX_HEADS):
            sc = sc + wcol[g] * rel[g * tq:(g + 1) * tq]
        sc = jnp.where(sc == 0.0, 0.0, sc)
        sc = jnp.where((k0 + _iota((1, tkc), 1)) <= q_pos, sc, NEG)
        key_s[c] = _sortable(sc)
        return 0

    _loop(n_chunks, score_chunk, 0)

    def count(pred_fn):
        def body(c, acc):
            hit = jnp.where(pred_fn(key_s[c]), 1.0, 0.0)
            for u in range(tkc // LANE):
                acc = acc + hit[:, u * LANE:(u + 1) * LANE]
            return acc
        acc = _loop(n_chunks, body, jnp.zeros((tq, LANE), F32))
        return jnp.sum(acc, axis=-1, keepdims=True)

    static = isinstance(n_chunks, int)
    if static:
        def digit_step(it, thr):
            d = jnp.left_shift(jnp.int32(1), jnp.int32(30) - 2 * it)
            for mult in (1, 2, 3):
                cnt = count(lambda k: k >= thr + mult * d)
                step = jnp.where(cnt >= float(n_top), d, 0)
                new = thr + step if mult == 1 else new + step
            return new

        thr = lax.fori_loop(0, 16, digit_step, jnp.full((tq, 1), -2 ** 31, I32))
    else:
        def bit_step(it, thr):
            cand = thr + jnp.left_shift(jnp.int32(1), jnp.int32(31) - it)
            cnt = count(lambda k: k >= cand)
            return jnp.where(cnt >= float(n_top), cand, thr)

        thr = lax.fori_loop(0, 32, bit_step, jnp.full((tq, 1), -2 ** 31, I32))
    need = float(n_top) - count(lambda k: k > thr)

    qd_st = _stack_heads(_mm(qc, pqd), h, 256) * (D_HEAD ** -0.5)

    def scores(c, eq_seen):
        kk = key_s[c]
        eq = jnp.where(kk == thr, 1.0, 0.0)
        rank = eq_seen + _mm(eq, tri)
        sel = (kk > thr) | ((kk == thr) & (rank <= need))
        bias1 = jnp.where(sel & ((c * tkc + _iota((1, tkc), 1)) <= q_pos), 0.0, NEG)
        return eq_seen + jnp.sum(eq, axis=-1, keepdims=True), qk(qd_st, get_keys(c)), bias1

    if static:
        eq_seen, parts = jnp.zeros((tq, 1), F32), []
        for c in range(n_chunks):
            eq_seen, s, bias1 = scores(c, eq_seen)
            parts.append(s + jnp.concatenate([bias1] * h, axis=0))
        s = jnp.concatenate(parts, axis=1)
        m = jnp.maximum(jnp.max(s, axis=-1, keepdims=True), MASK_FLOOR)
        p = jnp.exp(s - m)
        l = jnp.sum(p, axis=-1, keepdims=True)
        pv = _mm_nt if kt else _mm
        acc = pv(p[:, 0:tkc], get_keys(0))
        for c in range(1, n_chunks):
            acc = acc + pv(p[:, c * tkc:(c + 1) * tkc], get_keys(c))
    else:
        def chunk(c, carry):
            eq_seen, s, bias1 = scores(c, carry[0])
            keys = get_keys(c)
            heads = tuple(_online_update(carry[1][hh], s[hh * tq:(hh + 1) * tq] + bias1, None, keys, kt)
                          for hh in range(h))
            return eq_seen, heads

        init = (jnp.zeros((tq, 1), F32), tuple(_flash_init(tq, 256) for _ in range(h)))
        _, heads = lax.fori_loop(0, n_chunks, chunk, init)
        l = jnp.concatenate([hd[1] for hd in heads], axis=0)
        acc = jnp.concatenate([hd[2] for hd in heads], axis=0)
    o = _finish(l, acc)[:, 0:128]
    o = jnp.where(_iota((1, 128), 1) >= D_HEAD, o, 0.0)
    return jnp.concatenate([o[hh * tq:(hh + 1) * tq] for hh in range(h)], axis=1)


def _dsa_prompt_kernel(qc_ref, qi_ref, misc_ref, dk_ref, pqd_ref, pqi_ref, tri_ref, o_ref, key_s,
                       *, tq, tkc, n_top):
    i = pl.program_id(1)
    s0 = i * tq
    q_pos = s0 + _iota((tq, 1), 0)
    n_chunks = (s0 + tq + tkc - 1) // tkc
    o_ref[...] = _dsa_attend(qc_ref[...], qi_ref[...], misc_ref[...], q_pos, dk_ref, n_chunks, tkc, n_top,
                             pqd_ref[...], pqi_ref[...], tri_ref[...], key_s)


def _dsa_prompt(qc, qi, misc, dk, pqd, pqi, tri, nb, t, tq, tkc):
    nq = t // tq
    n_top = min(DSA_TOPK_MAX, t // 4)
    kern = functools.partial(_dsa_prompt_kernel, tq=tq, tkc=tkc, n_top=n_top)
    full = lambda a: pl.BlockSpec(a.shape, lambda b, i: (0, 0))
    return pl.pallas_call(
        kern,
        grid=(nb, nq),
        in_specs=[pl.BlockSpec((tq, 256), lambda b, i: (b * nq + i, 0)),
                  pl.BlockSpec((tq, 256), lambda b, i: (b * nq + i, 0)),
                  pl.BlockSpec((tq, 128), lambda b, i: (b * nq + i, 0)),
                  pl.BlockSpec((t, 256), lambda b, i: (b, 0)),
                  full(pqd), full(pqi), full(tri)],
        out_specs=pl.BlockSpec((tq, 512), lambda b, i: (b * nq + i, 0)),
        out_shape=jax.ShapeDtypeStruct((nb * t, 512), F32),
        scratch_shapes=[pltpu.VMEM((t // tkc, tq, tkc), I32)],
        compiler_params=_cparams(("parallel", "arbitrary"), 48),
        name="dsa_prompt",
    )(qc, qi, misc, dk, pqd, pqi, tri)


def _dsa_sample_kernel(pt_ref, *refs, pps, past, tkc, n_top):
    pages = refs[:pps]
    (qc_ref, qi_ref, misc_ref, newd_ref, pqd_ref, pqi_ref, tri_ref, o_ref, key_t, key_s) = refs[pps:]
    j = pl.program_id(1)
    nsteps = pl.num_programs(1)
    ppc = tkc // PAGE
    n_chunks = past // tkc + 1

    @pl.when(j == 0)
    def _():
        key_t[:, 160:256, :] = jnp.zeros((n_chunks, 96, tkc), key_t.dtype)
        key_t[n_chunks - 1] = jnp.zeros((256, tkc), key_t.dtype)

    @pl.when(j < nsteps - 1)
    def _():
        for p in range(pps):
            c = j * (pps // ppc) + p // ppc
            key_t[c, 0:160, (p % ppc) * PAGE:(p % ppc + 1) * PAGE] = pages[p][...].astype(key_t.dtype)

    @pl.when(j == nsteps - 1)
    def _():
        new_rows = jnp.concatenate([newd_ref[0].astype(F32), jnp.zeros((PAGE - SAMPLE_T, 256), F32)], axis=0)
        key_t[n_chunks - 1, :, 0:PAGE] = new_rows.T.astype(key_t.dtype)
        q_pos = past + _iota((SAMPLE_T, 1), 0)
        o_ref[0] = _dsa_attend(qc_ref[0], qi_ref[0], misc_ref[0], q_pos, key_t, n_chunks, tkc, n_top,
                               pqd_ref[...], pqi_ref[...], tri_ref[...], key_s, kt=True)


def _dsa_sample(page_table, cache_t, layer, qc, qi, misc, newd, pqd, pqi, tri, pps, tkc, ts):
    db, n_pages = page_table.shape
    past = n_pages * PAGE
    assert past % tkc == 0 and pps % (tkc // PAGE) == 0
    nsteps = n_pages // pps + 1
    last = n_pages - 1
    n_chunks = past // tkc + 1
    n_top = min(DSA_TOPK_MAX, (past + ts) // 4)

    def page_spec(p):
        return pl.BlockSpec((None, None, 160, PAGE),
                            lambda b, j, pt: (layer, pt[b, jnp.minimum(j * pps + p, last)], 0, 0))

    full = lambda a: pl.BlockSpec(a.shape, lambda b, j, pt: (0, 0))
    per_b = lambda w: pl.BlockSpec((1, SAMPLE_T, w), lambda b, j, pt: (b, 0, 0))
    kern = functools.partial(_dsa_sample_kernel, pps=pps, past=past, tkc=tkc, n_top=n_top)
    return pl.pallas_call(
        kern,
        grid_spec=pltpu.PrefetchScalarGridSpec(
            num_scalar_prefetch=1,
            grid=(db, nsteps),
            in_specs=[page_spec(p) for p in range(pps)] + [
                per_b(256), per_b(256), per_b(128), per_b(256), full(pqd), full(pqi), full(tri)],
            out_specs=per_b(512),
            scratch_shapes=[pltpu.VMEM((n_chunks, 256, tkc), MXU_DTYPE), pltpu.VMEM((n_chunks, SAMPLE_T, tkc), I32)],
        ),
        out_shape=jax.ShapeDtypeStruct((db, SAMPLE_T, 512), F32),
        compiler_params=_cparams(("parallel", "arbitrary"), 48),
        name="dsa_sample",
    )(page_table, *([cache_t] * pps), qc, qi, misc, newd, pqd, pqi, tri)


def _out_kernel(h_ref, oa_ref, om_ref, od_ref, wa_ref, wm_ref, wd_ref, g_ref, h1_o, xn_o):
    h1 = h_ref[...] + (_mm(oa_ref[...], wa_ref[...]) + _mm(om_ref[...], wm_ref[...]) + _mm(od_ref[...], wd_ref[...]))
    h1_o[...] = h1
    xn = h1 * lax.rsqrt(jnp.mean(h1 * h1, axis=-1, keepdims=True) + EPS) * g_ref[...]
    xn_o[...] = xn.astype(xn_o.dtype)


def _out_proj(h, oa, om, od, wa, wm, wd, g, tm):
    n = h.shape[0]
    row = lambda w: pl.BlockSpec((tm, w), lambda i: (i, 0))
    full = lambda a: pl.BlockSpec(a.shape, lambda i: (0, 0))
    return pl.pallas_call(
        _out_kernel,
        grid=(n // tm,),
        in_specs=[row(D_MODEL), row(512), row(512), row(512), full(wa), full(wm), full(wd), full(g)],
        out_specs=[row(D_MODEL), row(D_MODEL)],
        out_shape=[jax.ShapeDtypeStruct((n, D_MODEL), F32), jax.ShapeDtypeStruct((n, D_MODEL), MXU_DTYPE)],
        compiler_params=_cparams(("parallel",), 40),
        name="out_proj",
    )(h, oa, om, od, wa, wm, wd, g)


def _ffn_kernel(*refs, tm, seq, sample):
    if sample:
        (xn_ref, h1_ref, p_ref, wu_ref, wg_ref, wd_ref, cw_ref, cb_ref, gp_ref, wpg_ref, wpp_ref,
         p1_ref, p2_ref, h_o, a_o, acc_s) = refs
    else:
        (xn_ref, halo_ref, h1_ref, p_ref, wu_ref, wg_ref, wd_ref, cw_ref, cb_ref, gp_ref, wpg_ref, wpp_ref,
         h_o, a_o, acc_s) = refs
    i = pl.program_id(0)
    j = pl.program_id(1)
    xn = xn_ref[...]
    a = _mm(xn, wu_ref[...])
    b = _mm(xn, wg_ref[...])
    a_o[...] = a
    row = _iota((tm, 1), 0)
    r1 = pltpu.roll(a, 1, 0)
    r2 = pltpu.roll(a, 2, 0)
    if sample:
        tpos = row % SAMPLE_T
        prev1 = jnp.where(tpos == 0, p1_ref[...], r1)
        prev2 = jnp.where(tpos < 2, p2_ref[...], r2)
    else:
        ah = _mm(halo_ref[...], wu_ref[...])
        ah = jnp.where((i * tm) % seq == 0, 0.0, ah)
        prev1 = jnp.where(row == 0, ah[7:8, :], r1)
        prev2 = jnp.where(row == 0, ah[6:7, :], jnp.where(row == 1, ah[7:8, :], r2))
    c = cb_ref[...] + cw_ref[2:3, :] * a
    c = c + cw_ref[0:1, :] * prev2
    c = c + cw_ref[1:2, :] * prev1
    u = (c * jax.nn.sigmoid(c)) * b
    y = _mm(u, wd_ref[...])

    @pl.when(j == 0)
    def _():
        acc_s[...] = y

    @pl.when(j > 0)
    def _():
        acc_s[...] = acc_s[...] + y

    @pl.when(j == pl.num_programs(1) - 1)
    def _():
        h2 = h1_ref[...] + acc_s[...]
        xn3 = h2 * lax.rsqrt(jnp.mean(h2 * h2, axis=-1, keepdims=True) + EPS) * gp_ref[...]
        gate = jax.nn.sigmoid(_mm(xn3, wpg_ref[...]))
        h_o[...] = h2 + gate * _mm(p_ref[...], wpp_ref[...])


def _ffn(xn, h1, p, wu, wg, wd, cw, cb, gp, wpg, wpp, tm, tf, seq, prefix=None):
    n = xn.shape[0]
    nf = D_FF // tf
    sample = prefix is not None
    row = lambda w: pl.BlockSpec((tm, w), lambda i, j: (i, 0))
    full = lambda a: pl.BlockSpec(a.shape, lambda i, j: (0, 0))
    ff_col = lambda r: pl.BlockSpec((r, tf), lambda i, j: (0, j))
    in_specs = [row(D_MODEL)]
    args = [xn]
    if not sample:
        in_specs.append(pl.BlockSpec((SUB, D_MODEL), lambda i, j: (jnp.maximum(i * (tm // SUB) - 1, 0), 0)))
        args.append(xn)
    in_specs += [row(D_MODEL), row(PLE_DIM), ff_col(D_MODEL), ff_col(D_MODEL),
                 pl.BlockSpec((tf, D_MODEL), lambda i, j: (j, 0)), ff_col(CONV_W), ff_col(1),
                 full(gp), full(wpg), full(wpp)]
    args += [h1, p, wu, wg, wd, cw, cb, gp, wpg, wpp]
    if sample:
        in_specs += [pl.BlockSpec((tm, tf), lambda i, j: (i, j))] * 2
        args += list(prefix)
    kern = functools.partial(_ffn_kernel, tm=tm, seq=seq, sample=sample)
    return pl.pallas_call(
        kern,
        grid=(n // tm, nf),
        in_specs=in_specs,
        out_specs=[row(D_MODEL), pl.BlockSpec((tm, tf), lambda i, j: (i, j))],
        out_shape=[jax.ShapeDtypeStruct((n, D_MODEL), F32), jax.ShapeDtypeStruct((n, D_FF), F32)],
        scratch_shapes=[pltpu.VMEM((tm, D_MODEL), F32)],
        compiler_params=_cparams(("parallel", "arbitrary"), 56),
        name="ffn",
    )(*args)


def _pick(n, prefs):
    for p in prefs:
        if n % p == 0:
            return p
    return n


def kernel(x_prompt, x_sample, cache_nsa, cache_mla, cache_dsa, state_win, state_conv, page_table, p_prompt, p_sample, norm_mix, w_in, nsa_qn, nsa_kn, nsa_cmp_pos, mla_cqn, mla_ckvn, mla_w_uq, mla_w_ukv, mla_qn, mla_kn, dsa_qn, dsa_kn, w_out, norm_ffn, ffn_w_up, ffn_w_gate, ffn_conv_w, ffn_conv_b, ffn_w_down, norm_ple, ple_w_gate, ple_w_proj):
    depth = w_in.shape[0]
    B, T, _ = x_prompt.shape
    DB, TS, _ = x_sample.shape
    n_pages = page_table.shape[1]
    past = n_pages * PAGE
    assert TS <= 4 and T % 128 == 0 and T >= NSA_WINDOW + 128 and past >= NSA_WINDOW
    mm = MXU_DTYPE
    bf = lambda a: jnp.asarray(a, jnp.bfloat16)

    win_cols, wuq_cols = _win_cols(), _wuq_cols()
    g256 = bf(_blockdiag(256, 64))
    g1024 = bf(_blockdiag(1024, 128))
    g8 = bf((np.arange(1024)[:, None] // 128 == np.arange(128)[None, :]).astype(np.float32))
    pk = bf(_placement(128, 1024, [(j, h * 128 + 64 + j) for h in range(8) for j in range(32)]))
    swc = bf(_placement(256, 256, [(j, (j + 32) % 64) for j in range(64)]))
    pq_c = bf(_q_placement(4, 64, 256, 0))
    pq_s = bf(_q_placement(4, 64, 256, 128))
    pq_w = bf(_q_placement(4, 64, 128, 0))
    pq_d = bf(_q_placement(4, 64, 256, 0))
    pq_i = bf(_q_placement(8, 32, 256, 128))
    tkc = _pick(T, (512, 256, 128))
    tri = bf(np.triu(np.ones((tkc, tkc), np.float32)))
    tab_p = _rope_tables(jnp.arange(T))
    pos_s = past + (jnp.arange(DB * SAMPLE_T) % SAMPLE_T)
    tab_s = _rope_tables(pos_s)
    nblk_p = T // NSA_BLK
    ctab_p = _cmp_tables(nblk_p)
    nbp_s = ((past // NSA_BLK + 1 + LANE - 1) // LANE) * LANE
    ctab_s = _cmp_tables(nbp_s)
    lpad_s = ((past + SAMPLE_T + tkc - 1) // tkc) * tkc
    blk_of = lambda n_keys: np.arange(n_keys) // NSA_BLK
    expand_p = bf((blk_of(T).reshape(T // tkc, 1, tkc) == np.arange(nblk_p)[None, :, None]).astype(np.float32))
    expand_s = bf((blk_of(lpad_s)[None, None, :] == np.arange(nbp_s)[None, :, None]).astype(np.float32))
    exp_rows = np.full((512,), -1, np.int64)
    for h in range(4):
        exp_rows[h * 128 + 64: (h + 1) * 128] = h * 64 + np.arange(64)

    tm_p = _pick(B * T, (256, 128))
    tq = 256
    tq_m = _pick(T, (512, 256, 128))
    tm_o = _pick(B * T, (512, 256, 128))
    tf = D_FF // 2
    ns = DB * SAMPLE_T

    cache_mla_t = jnp.swapaxes(cache_mla, 2, 3)
    cache_dsa_t = jnp.swapaxes(cache_dsa, 2, 3)
    hp = x_prompt.reshape(B * T, D_MODEL)
    hs = jnp.pad(x_sample, ((0, 0), (0, SAMPLE_T - TS), (0, 0))).reshape(ns, D_MODEL)
    outs = {k: [] for k in ("nsa_p", "nsa_s", "mla_p", "mla_s", "dsa_p", "dsa_s", "win_p", "win_s", "conv_p", "conv_s")}
    unpad = lambda a: a.reshape(DB, SAMPLE_T, -1)[:, :TS]

    for i in range(depth):
        wbig = _take_cols(w_in[i], win_cols).astype(mm)
        wuq = _take_cols(mla_w_uq[i], wuq_cols).astype(mm)
        gv = _gain_vector(norm_mix[i], nsa_qn[i], nsa_kn[i], mla_cqn[i], mla_ckvn[i], dsa_qn[i], dsa_kn[i], mla_qn[i])
        ukv = mla_w_ukv[i].reshape(MLA_KV_RANK, MLA_HEADS, MLA_NOPE + MLA_V)
        wk = jnp.pad(ukv[:, :, :MLA_NOPE], ((0, 0), (0, 0), (0, LANE - MLA_NOPE))).reshape(MLA_KV_RANK, 1024).astype(mm)
        wv = ukv[:, :, MLA_NOPE:].reshape(MLA_KV_RANK, 512).astype(mm)
        gk = jnp.tile(jnp.concatenate([mla_kn[i], jnp.zeros((32,), F32)]), 8)[None, :]
        wkvt = jnp.concatenate([ukv[:, :, :MLA_NOPE].reshape(MLA_KV_RANK, 512),
                                ukv[:, :, MLA_NOPE:].reshape(MLA_KV_RANK, 512)], axis=1).T.astype(mm)
        gkc = mla_kn[i][:, None]
        wpos = jnp.concatenate([nsa_cmp_pos[i, 0], nsa_cmp_pos[i, 1], jnp.zeros((NSA_BLK, 128), F32)], axis=1)
        wo = w_out[i]
        expand = lambda w: jnp.where(jnp.asarray(exp_rows >= 0)[:, None],
                                     jnp.take(w, jnp.asarray(np.maximum(exp_rows, 0)), axis=0), 0.0).astype(mm)
        wo_a, wo_m, wo_d = expand(wo[0:256]), wo[256:768].astype(mm), expand(wo[768:1024])
        ffn_w = (ffn_w_up[i].astype(mm), ffn_w_gate[i].astype(mm), ffn_w_down[i].astype(mm), ffn_conv_w[i],
                 ffn_conv_b[i][None, :], norm_ple[i][None, :], ple_w_gate[i].astype(mm), ple_w_proj[i].astype(mm))
        g_ffn = norm_ffn[i][None, :]

        (qa, qm, qc, qi, misc, rn, rm, rd, rw, nk, wkk, dk, kp) = _proj(
            hp, tab_p, T // tm_p, gv, wbig, wuq, g256, g1024, tm_p)
        tk_m = _pick(T, (512, 256, 128))
        km, vt = _mla_prep(rm, kp, wk, wv, wv.T, pk, g8, gk, tk_m)
        o_m = _mla_flash(qm, km, vt, B, T, tq_m, tk_m, 4)
        cmp = _nsa_cmp(rn, wpos, ctab_p, swc, B, T)
        o_a = _nsa_prompt(qa, misc, cmp, nk, wkk, pq_c, pq_s, pq_w, expand_p, B, T, tq, tkc)
        o_d = _dsa_prompt(qc, qi, misc, dk, pq_d, pq_i, tri, B, T, tq, tkc)
        h1, xn2 = _out_proj(hp, o_a, o_m, o_d, wo_a, wo_m, wo_d, g_ffn, tm_o)
        hp, a_p = _ffn(xn2, h1, p_prompt[i].reshape(B * T, PLE_DIM), *ffn_w, tm_o, tf, T)
        outs["nsa_p"].append(rn.reshape(B, T, -1))
        outs["mla_p"].append(rm.reshape(B, T, -1))
        outs["dsa_p"].append(rd.reshape(B, T, -1))
        outs["win_p"].append(rw.reshape(B, T, -1)[:, T - min(NSA_WINDOW, T):])
        outs["conv_p"].append(a_p.reshape(B, T, D_FF)[:, T - (CONV_W - 1):])

        (qa, qm, qc, qi, misc, rn, rm, rd, rw, nk, wkk, dk, kp) = _proj(
            hs, tab_s, 1, gv, wbig, wuq, g256, g1024, _pick(ns, (256, 128)))
        r3 = lambda a: a.reshape(DB, SAMPLE_T, -1)
        pps_m = _pick(n_pages, (32, 16, 8, 4))
        pps = _pick(n_pages, (8, 4))
        o_m = _mla_sample(page_table, cache_mla_t, i, r3(qm), r3(rm), r3(kp), wkvt, gkc, wk, wv, pk, g8, gk, pps_m)
        o_a = _nsa_sample(page_table, cache_nsa, i, r3(qa), r3(misc), r3(rn), state_win, r3(rw), wpos, ctab_s, swc,
                          pq_c, pq_s, pq_w, expand_s, pps)
        o_d = _dsa_sample(page_table, cache_dsa_t, i, r3(qc), r3(qi), r3(misc), r3(dk), pq_d, pq_i, tri, pps, tkc, TS)
        tm_s = _pick(ns, (256, 128))
        h1, xn2 = _out_proj(hs, o_a.reshape(ns, 512), o_m.reshape(ns, 512), o_d.reshape(ns, 512),
                            wo_a, wo_m, wo_d, g_ffn, tm_s)
        sc = state_conv[i]
        zrow = jnp.zeros((DB, SAMPLE_T - 1, D_FF), F32)
        p1 = jnp.concatenate([sc[:, 1:2], zrow], axis=1).reshape(ns, D_FF)
        p2 = jnp.concatenate([sc[:, 0:2], zrow[:, 1:]], axis=1).reshape(ns, D_FF)
        p_s = jnp.pad(p_sample[i], ((0, 0), (0, SAMPLE_T - TS), (0, 0))).reshape(ns, PLE_DIM)
        hs, a_s = _ffn(xn2, h1, p_s, *ffn_w, tm_s, tf, SAMPLE_T, prefix=(p1, p2))
        outs["nsa_s"].append(unpad(rn))
        outs["mla_s"].append(unpad(rm))
        outs["dsa_s"].append(unpad(rd))
        win_all = jnp.concatenate([state_win[i], unpad(rw)], axis=1)
        outs["win_s"].append(win_all[:, TS:])
        a_ext = jnp.concatenate([sc, unpad(a_s)], axis=1)
        outs["conv_s"].append(a_ext[:, TS:])

    st = lambda k: jnp.stack(outs[k])
    return (hp.reshape(B, T, D_MODEL), unpad(hs),
            st("nsa_p"), st("nsa_s"), st("mla_p"), st("mla_s"), st("dsa_p"), st("dsa_s"),
            st("win_p"), st("win_s"), st("conv_p"), st("conv_s"))
```

```python
import functools
import math

import numpy as np
import jax
import jax.numpy as jnp
from jax import lax
from jax.experimental import pallas as pl
from jax.experimental.pallas import tpu as pltpu

F32 = jnp.float32
I32 = jnp.int32
MXU_DTYPE = jnp.bfloat16

D_MODEL = 1024
D_HEAD = 64
ROPE_THETA = 10000.0
EPS = 1e-6
NEG = -1e30
MASK_FLOOR = -1e29
NSA_HEADS = 4
NSA_BLK = 64
NSA_TOPN = 16
NSA_WINDOW = 512
NSA_FORCE = 1e9
MLA_HEADS = 8
MLA_Q_RANK = 256
MLA_KV_RANK = 256
MLA_NOPE = 64
MLA_ROPE = 32
MLA_V = 64
MLA_QK = MLA_NOPE + MLA_ROPE
DSA_HEADS = 4
IDX_HEADS = 8
IDX_DIM = 32
DSA_TOPK_MAX = 256
D_FF = 2816
CONV_W = 3
PLE_DIM = 256
PAGE = 128
LANE = 128
SUB = 8
SAMPLE_T = 8

C_QA, C_KC, C_VC, C_KS, C_VS, C_KW, C_VW, C_GA = 0, 256, 320, 384, 448, 512, 576, 640
C_CQ, C_CKV, C_KPE, C_QC, C_KD, C_VD, C_QI, C_KI, C_WI = 652, 908, 1164, 1196, 1452, 1516, 1580, 1836, 1868

(Z_QA, Z_QAS, Z_NSA, Z_NSAS, Z_WIN, Z_WINS, Z_MISC, Z_CQ, Z_CKV, Z_KPE, Z_KPES, Z_QC, Z_QCS,
 Z_DKV, Z_DKVS, Z_KI, Z_KIS, Z_QI, Z_QIS, Z_TOTAL) = (
    0, 256, 512, 768, 1024, 1152, 1280, 1408, 1664, 1920, 2048, 2176, 2432, 2688, 2816, 2944,
    3072, 3200, 3456, 3712)

(T_64C, T_64S, T_NSAC, T_NSAS, T_KVC, T_KVS, T_QC, T_QS, T_32C, T_32S, T_328C, T_328S, T_TOTAL) = (
    0, 256, 512, 768, 1024, 1152, 1280, 2304, 3328, 3456, 3584, 3840, 4096)

(GV_NMIX, GV_QA_A, GV_QA_B, GV_NSA_A, GV_NSA_B, GV_WIN_A, GV_WIN_B, GV_CQ, GV_CKV, GV_QC_A, GV_QC_B,
 GV_DKV_A, GV_DKV_B, GV_Q, GV_TOTAL) = (
    0, 1024, 1280, 1536, 1792, 2048, 2176, 2304, 2560, 2816, 3072, 3328, 3456, 3584, 4608)


def _swap_idx(base, d, n):
    l = np.arange(n * d)
    return base + (l // d) * d + ((l % d) + d // 2) % d


def _win_cols():
    cols = np.full((Z_TOTAL,), -1, np.int64)

    def put(off, idx, at=0):
        cols[off + at: off + at + len(idx)] = idx

    put(Z_QA, C_QA + np.arange(256))
    put(Z_QAS, _swap_idx(C_QA, 64, 4))
    put(Z_NSA, C_KC + np.arange(256))
    put(Z_NSAS, _swap_idx(C_KS, 64, 1), at=128)
    put(Z_WIN, C_KW + np.arange(128))
    put(Z_WINS, _swap_idx(C_KW, 64, 1))
    put(Z_MISC, C_GA + np.arange(12))
    put(Z_MISC, C_WI + np.arange(8), at=12)
    put(Z_CQ, C_CQ + np.arange(256))
    put(Z_CKV, C_CKV + np.arange(256))
    put(Z_KPE, C_KPE + np.arange(32))
    put(Z_KPES, _swap_idx(C_KPE, 32, 1))
    put(Z_QC, C_QC + np.arange(256))
    put(Z_QCS, _swap_idx(C_QC, 64, 4))
    put(Z_DKV, C_KD + np.arange(128))
    put(Z_DKVS, _swap_idx(C_KD, 64, 1))
    put(Z_KI, C_KI + np.arange(32))
    put(Z_KIS, _swap_idx(C_KI, 32, 1))
    put(Z_QI, C_QI + np.arange(256))
    put(Z_QIS, _swap_idx(C_QI, 32, 8))
    return cols


def _wuq_cols():
    cols = np.full((2 * MLA_HEADS * LANE,), -1, np.int64)
    for h in range(MLA_HEADS):
        cols[h * LANE: h * LANE + MLA_QK] = h * MLA_QK + np.arange(MLA_QK)
        cols[1024 + h * LANE + MLA_NOPE: 1024 + h * LANE + MLA_QK] = _swap_idx(h * MLA_QK + MLA_NOPE, MLA_ROPE, 1)
    return cols


def _take_cols(w, cols):
    g = jnp.take(w, jnp.asarray(np.maximum(cols, 0)), axis=1)
    return jnp.where(jnp.asarray(cols >= 0)[None, :], g, 0.0)


def _blockdiag(n, d):
    i = np.arange(n)
    return (i[:, None] // d == i[None, :] // d).astype(np.float32)


def _placement(n_in, n_out, pairs):
    m = np.zeros((n_in, n_out), np.float32)
    for s, d in pairs:
        m[s, d] = 1.0
    return m


def _q_placement(n_heads, d, width, at):
    pairs = [(h * d + j, h * width + at + j) for h in range(n_heads) for j in range(d)]
    return _placement(n_heads * d, n_heads * width, pairs)


def _rope_tables(pos):
    pos = pos.astype(F32)[:, None]
    P = pos.shape[0]

    def cs(d):
        half = d // 2
        inv = ROPE_THETA ** (-jnp.arange(half, dtype=F32) / half)
        ang = pos * inv[None, :]
        c, s = jnp.cos(ang), jnp.sin(ang)
        return jnp.concatenate([c, c], axis=1), jnp.concatenate([-s, s], axis=1)

    c64, s64 = cs(64)
    c32, s32 = cs(32)
    one = lambda n: jnp.ones((P, n), F32)
    zero = lambda n: jnp.zeros((P, n), F32)
    segs = [
        jnp.tile(c64, (1, 4)), jnp.tile(s64, (1, 4)),
        jnp.concatenate([one(128), c64, one(64)], 1), jnp.concatenate([zero(128), s64, zero(64)], 1),
        jnp.concatenate([c64, one(64)], 1), jnp.concatenate([s64, zero(64)], 1),
        jnp.tile(jnp.concatenate([one(64), c32, one(32)], 1), (1, 8)),
        jnp.tile(jnp.concatenate([zero(64), s32, zero(32)], 1), (1, 8)),
        jnp.concatenate([c32, one(96)], 1), jnp.concatenate([s32, zero(96)], 1),
        jnp.tile(c32, (1, 8)), jnp.tile(s32, (1, 8)),
    ]
    return jnp.concatenate(segs, axis=1)


def _cmp_tables(nbp):
    pos = (jnp.arange(nbp) * NSA_BLK + (NSA_BLK - 1)).astype(F32)[:, None]
    inv = ROPE_THETA ** (-jnp.arange(32, dtype=F32) / 32)
    ang = pos * inv[None, :]
    c, s = jnp.cos(ang), jnp.sin(ang)
    one = jnp.ones((nbp, 192), F32)
    zero = jnp.zeros((nbp, 192), F32)
    return jnp.concatenate([c, c, one, -s, s, zero], axis=1)


def _swap64(g):
    return jnp.concatenate([g[32:], g[:32]])


def _gain_vector(norm_mix, nsa_qn, nsa_kn, mla_cqn, mla_ckvn, dsa_qn, dsa_kn, mla_qn):
    o64, z64 = jnp.ones((64,), F32), jnp.zeros((64,), F32)
    qpad = jnp.concatenate([mla_qn, jnp.zeros((32,), F32)])
    segs = [
        norm_mix,
        jnp.tile(nsa_qn, 4), jnp.tile(_swap64(nsa_qn), 4),
        jnp.concatenate([nsa_kn[0], o64, nsa_kn[1], o64]), jnp.concatenate([z64, z64, _swap64(nsa_kn[1]), z64]),
        jnp.concatenate([nsa_kn[2], o64]), jnp.concatenate([_swap64(nsa_kn[2]), z64]),
        mla_cqn, mla_ckvn,
        jnp.tile(dsa_qn, 4), jnp.tile(_swap64(dsa_qn), 4),
        jnp.concatenate([dsa_kn, o64]), jnp.concatenate([_swap64(dsa_kn), z64]),
        jnp.tile(qpad, 8),
    ]
    return jnp.concatenate(segs)[None, :]


def _mm(a, b):
    return jnp.dot(a.astype(MXU_DTYPE), b.astype(MXU_DTYPE), preferred_element_type=F32)


def _mm_nt(a, b):
    return lax.dot_general(a.astype(MXU_DTYPE), b.astype(MXU_DTYPE), (((1,), (1,)), ((), ())),
                           preferred_element_type=F32)


def _split3(x):
    x1 = x.astype(jnp.bfloat16)
    r1 = x - x1.astype(F32)
    x2 = r1.astype(jnp.bfloat16)
    x3 = (r1 - x2.astype(F32)).astype(jnp.bfloat16)
    return x1, x2, x3


def _place(x, p):
    x1, x2, x3 = _split3(x)
    d = lambda a: jnp.dot(a, p, preferred_element_type=F32)
    return d(x1) + d(x2) + d(x3)


def _gsum(x2, g):
    hi = x2.astype(jnp.bfloat16)
    lo = (x2 - hi.astype(F32)).astype(jnp.bfloat16)
    return jnp.dot(hi, g, preferred_element_type=F32) + jnp.dot(lo, g, preferred_element_type=F32)


def _iota(shape, dim):
    return lax.broadcasted_iota(I32, shape, dim)


def _stack_heads(x, n, w):
    return jnp.concatenate([x[:, h * w:(h + 1) * w] for h in range(n)], axis=0)


def _softmax_rows(s, mask):
    s = jnp.where(mask, s, NEG)
    m = jnp.max(s, axis=-1, keepdims=True)
    e = jnp.where(mask, jnp.exp(s - m), 0.0)
    l = jnp.sum(e, axis=-1, keepdims=True)
    return e / jnp.where(l > 0.0, l, 1.0)


def _online_update(carry, s, mask, v, vt=False):
    m, l, acc = carry
    if mask is not None:
        s = jnp.where(mask, s, NEG)
    m_new = jnp.maximum(m, jnp.max(s, axis=-1, keepdims=True))
    alpha = jnp.exp(m - m_new)
    p = jnp.exp(s - m_new)
    if mask is not None:
        p = jnp.where(mask, p, 0.0)
    l = alpha * l + jnp.sum(p, axis=-1, keepdims=True)
    acc = alpha * acc + (_mm_nt(p, v) if vt else _mm(p, v))
    return m_new, l, acc


def _flash_init(rows, width):
    return (jnp.full((rows, 1), MASK_FLOOR, F32), jnp.zeros((rows, 1), F32), jnp.zeros((rows, width), F32))


def _finish(l, acc):
    return acc / jnp.where(l > 0.0, l, 1.0)


def _sortable(x):
    b = lax.bitcast_convert_type(x, I32)
    return jnp.where(b < 0, b ^ jnp.int32(0x7FFFFFFF), b)


def _cparams(sem, vmem_mb):
    return pltpu.CompilerParams(dimension_semantics=sem, vmem_limit_bytes=vmem_mb * 1024 * 1024)


def _proj_kernel(x_ref, tab_ref, gv_ref, w_ref, wuq_ref, g256_ref, g1024_ref,
                 qa_o, qm_o, qc_o, qi_o, misc_o, rn_o, rm_o, rd_o, rw_o, nk_o, wk_o, dk_o, kp_o):
    gv = lambda off, w: gv_ref[:, off:off + w]
    tab = lambda off, w: tab_ref[:, off:off + w]
    x = x_ref[...]
    xn = x * lax.rsqrt(jnp.mean(x * x, axis=-1, keepdims=True) + EPS) * gv(GV_NMIX, D_MODEL)
    z = _mm(xn, w_ref[...])
    zs = lambda off, w: z[:, off:off + w]
    g256 = g256_ref[...]
    g128 = g256_ref[0:128, 0:128]

    def group(zo, zso, w, gmat, d, ga, gb, tc, ts, normmask=None):
        a, asw = zs(zo, w), zs(zso, w)
        if gmat is None:
            rs = None
        else:
            rs = lax.rsqrt(_gsum(a * a, gmat) / d + EPS)
            if normmask is not None:
                rs = jnp.where(normmask, rs, 1.0)
        ca, cb = tab(tc, w), tab(ts, w)
        if ga is not None:
            ca, cb = gv(ga, w) * ca, gv(gb, w) * cb
        if rs is not None:
            ca, cb = rs * ca, rs * cb
        return a * ca + asw * cb

    qa_o[...] = group(Z_QA, Z_QAS, 256, g256, 64.0, GV_QA_A, GV_QA_B, T_64C, T_64S).astype(qa_o.dtype)
    lane = _iota((1, 256), 1)
    nm = (lane < 64) | ((lane >= 128) & (lane < 192))
    rn = group(Z_NSA, Z_NSAS, 256, g256, 64.0, GV_NSA_A, GV_NSA_B, T_NSAC, T_NSAS, nm)
    rn_o[...] = rn
    nk_o[...] = rn.astype(nk_o.dtype)
    nm128 = _iota((1, 128), 1) < 64
    rw = group(Z_WIN, Z_WINS, 128, g128, 64.0, GV_WIN_A, GV_WIN_B, T_KVC, T_KVS, nm128)
    rw_o[...] = rw
    wk_o[...] = rw.astype(wk_o.dtype)
    zm = zs(Z_MISC, 128)
    misc_o[...] = jnp.where(_iota((1, 128), 1) < 12, jax.nn.sigmoid(zm), zm * (IDX_HEADS ** -0.5))
    cq = zs(Z_CQ, 256)
    cqn = cq * lax.rsqrt(jnp.mean(cq * cq, axis=-1, keepdims=True) + EPS) * gv(GV_CQ, 256)
    q2 = _mm(cqn, wuq_ref[...])
    qr = q2[:, 0:1024] * tab(T_QC, 1024) + q2[:, 1024:2048] * tab(T_QS, 1024)
    rs = lax.rsqrt(_gsum(qr * qr, g1024_ref[...]) / float(MLA_QK) + EPS)
    qm_o[...] = (qr * rs * gv(GV_Q, 1024)).astype(qm_o.dtype)
    ckv = zs(Z_CKV, 256)
    rm_o[:, 0:256] = ckv * lax.rsqrt(jnp.mean(ckv * ckv, axis=-1, keepdims=True) + EPS) * gv(GV_CKV, 256)
    kp = group(Z_KPE, Z_KPES, 128, None, 0.0, None, None, T_32C, T_32S)
    rm_o[:, 256:288] = kp[:, 0:32]
    kp_o[...] = kp
    qc_o[...] = group(Z_QC, Z_QCS, 256, g256, 64.0, GV_QC_A, GV_QC_B, T_64C, T_64S).astype(qc_o.dtype)
    dkv = group(Z_DKV, Z_DKVS, 128, g128, 64.0, GV_DKV_A, GV_DKV_B, T_KVC, T_KVS, nm128)
    ki = group(Z_KI, Z_KIS, 128, None, 0.0, None, None, T_32C, T_32S)
    rd_o[:, 0:128] = dkv
    rd_o[:, 128:160] = ki[:, 0:32]
    dk_o[:, 0:128] = dkv.astype(dk_o.dtype)
    dk_o[:, 128:256] = ki.astype(dk_o.dtype)
    qi_o[...] = group(Z_QI, Z_QIS, 256, None, 0.0, None, None, T_328C, T_328S).astype(qi_o.dtype)


def _proj(x, tab, n_pos_tiles, gv, wbig, wuq, g256, g1024, tm):
    n = x.shape[0]
    row = lambda w: pl.BlockSpec((tm, w), lambda i: (i, 0))
    full = lambda a: pl.BlockSpec(a.shape, lambda i: (0, 0))
    widths = [(256, MXU_DTYPE), (1024, MXU_DTYPE), (256, MXU_DTYPE), (256, MXU_DTYPE), (128, F32),
              (256, F32), (288, F32), (160, F32), (128, F32),
              (256, MXU_DTYPE), (128, MXU_DTYPE), (256, MXU_DTYPE), (128, F32)]
    return pl.pallas_call(
        _proj_kernel,
        grid=(n // tm,),
        in_specs=[row(D_MODEL),
                  pl.BlockSpec((tm, T_TOTAL), lambda i: (i % n_pos_tiles, 0)),
                  full(gv), full(wbig), full(wuq), full(g256), full(g1024)],
        out_specs=[row(w) for w, _ in widths],
        out_shape=[jax.ShapeDtypeStruct((n, w), dt) for w, dt in widths],
        compiler_params=_cparams(("parallel",), 56),
        name="proj",
    )(x, tab, gv, wbig, wuq, g256, g1024)


def _mla_keys(ckv, kpe128, wk, wv, pk, g8, gk):
    kraw = _mm(ckv, wk) + _place(kpe128, pk)
    ss = _gsum(kraw * kraw, g8)
    rs = lax.rsqrt(ss / float(MLA_QK) + EPS)
    parts = [kraw[:, h * LANE:(h + 1) * LANE] * rs[:, h:h + 1] for h in range(MLA_HEADS)]
    km = jnp.concatenate(parts, axis=1) * gk
    return km.astype(MXU_DTYPE), _mm(ckv, wv).astype(MXU_DTYPE)


def _mla_prep_kernel(rm_ref, kp_ref, wk_ref, wv_ref, wvt_ref, pk_ref, g8_ref, gk_ref, km_o, vt_o):
    ckv = rm_ref[:, 0:256]
    km, _ = _mla_keys(ckv, kp_ref[...], wk_ref[...], wv_ref[...], pk_ref[...], g8_ref[...], gk_ref[...])
    km_o[...] = km
    vt_o[0] = _mm_nt(wvt_ref[...], ckv).astype(vt_o.dtype)


def _mla_prep(rm, kp, wk, wv, wvt, pk, g8, gk, tk):
    n = rm.shape[0]
    full = lambda a: pl.BlockSpec(a.shape, lambda i: (0, 0))
    return pl.pallas_call(
        _mla_prep_kernel,
        grid=(n // tk,),
        in_specs=[pl.BlockSpec((tk, 288), lambda i: (i, 0)), pl.BlockSpec((tk, 128), lambda i: (i, 0)),
                  full(wk), full(wv), full(wvt), full(pk), full(g8), full(gk)],
        out_specs=[pl.BlockSpec((tk, 1024), lambda i: (i, 0)), pl.BlockSpec((1, 512, tk), lambda i: (i, 0, 0))],
        out_shape=[jax.ShapeDtypeStruct((n, 1024), MXU_DTYPE), jax.ShapeDtypeStruct((n // tk, 512, tk), MXU_DTYPE)],
        compiler_params=_cparams(("parallel",), 40),
        name="mla_prep",
    )(rm, kp, wk, wv, wvt, pk, g8, gk)


def _mla_flash_kernel(q_ref, k_ref, vt_ref, o_ref, *, tq, tk, hps):
    i = pl.program_id(2)
    scale = MLA_QK ** -0.5
    q_pos = i * tq + _iota((1, tq), 1)
    qs = [q_ref[:, hq * LANE:(hq + 1) * LANE] for hq in range(hps)]

    def chunk(c, carry, masked):
        k0 = pl.multiple_of(c * tk, tk)
        mask = ((k0 + _iota((tk, 1), 0)) <= q_pos) if masked else None
        heads = range(hps)
        sts = [_mm_nt(k_ref[pl.ds(k0, tk), hq * LANE:(hq + 1) * LANE], qs[hq]) * scale for hq in heads]
        if masked:
            sts = [jnp.where(mask, st, NEG) for st in sts]
        m_new = [jnp.maximum(carry[hq][0], jnp.max(sts[hq], axis=0, keepdims=True)) for hq in heads]
        ps = [jnp.exp(sts[hq] - m_new[hq]) for hq in heads]
        if masked:
            ps = [jnp.where(mask, p, 0.0) for p in ps]
        out = []
        for hq in heads:
            m, l, acc = carry[hq]
            alpha = jnp.exp(m - m_new[hq])
            vt = vt_ref[c, (hq // 2) * LANE:(hq // 2 + 1) * LANE, :]
            out.append((m_new[hq], alpha * l + jnp.sum(ps[hq], axis=0, keepdims=True),
                        alpha * acc + _mm(vt, ps[hq])))
        return tuple(out)

    one = (jnp.full((1, tq), NEG, F32), jnp.zeros((1, tq), F32), jnp.zeros((LANE, tq), F32))
    n_full = (i * tq) // tk
    n_all = ((i + 1) * tq + tk - 1) // tk
    carry = lax.fori_loop(0, n_full, lambda c, cr: chunk(c, cr, False), (one,) * hps)
    carry = lax.fori_loop(n_full, n_all, lambda c, cr: chunk(c, cr, True), carry)
    outs = [_finish(cr[1], cr[2]) for cr in carry]
    for pair in range(hps // 2):
        ot = jnp.where(_iota((LANE, 1), 0) < MLA_V, outs[2 * pair], outs[2 * pair + 1])
        o_ref[:, pair * LANE:(pair + 1) * LANE] = ot.T


def _mla_flash(qm, km, vt, nb, t, tq, tk, hps):
    nq = t // tq
    nc = t // tk
    kern = functools.partial(_mla_flash_kernel, tq=tq, tk=tk, hps=hps)
    return pl.pallas_call(
        kern,
        grid=(nb, MLA_HEADS // hps, nq),
        in_specs=[pl.BlockSpec((tq, hps * LANE), lambda b, h, i: (b * nq + i, h)),
                  pl.BlockSpec((t, hps * LANE), lambda b, h, i: (b, h)),
                  pl.BlockSpec((nc, hps * MLA_V, tk), lambda b, h, i: (b, h, 0))],
        out_specs=pl.BlockSpec((tq, hps * MLA_V), lambda b, h, i: (b * nq + i, h)),
        out_shape=jax.ShapeDtypeStruct((nb * t, 512), F32),
        compiler_params=_cparams(("parallel", "parallel", "arbitrary"), 56),
        name="mla_flash",
    )(qm, km, vt)


def _mla_sample_kernel(pt_ref, *refs, pps, past):
    pages = refs[:pps]
    (q_ref, newm_ref, newkp_ref, wkvt_ref, gkc_ref, wk_ref, wv_ref, pk_ref, g8_ref, gk_ref, o_ref,
     m_s, l_s, acc_s) = refs[pps:]
    j = pl.program_id(1)
    nsteps = pl.num_programs(1)
    rows = SAMPLE_T * MLA_HEADS
    q = q_ref[0].astype(F32)
    head_of_lane = _iota((SUB, 1024), 1) // LANE
    sub = _iota((SUB, 1024), 0)
    qbd = jnp.concatenate(
        [jnp.where(head_of_lane == sub, jnp.broadcast_to(q[t:t + 1, :], (SUB, 1024)), 0.0)
         for t in range(SAMPLE_T)], axis=0).astype(MXU_DTYPE)
    q_pos = past + _iota((rows, 1), 0) // MLA_HEADS
    scale = MLA_QK ** -0.5

    @pl.when(j == 0)
    def _():
        m_s[...] = jnp.full(m_s.shape, NEG, F32)
        l_s[...] = jnp.zeros(l_s.shape, F32)
        acc_s[...] = jnp.zeros(acc_s.shape, F32)

    def update(s, mask, v, vt):
        m, l, acc = _online_update((m_s[...], l_s[...], acc_s[...]), s, mask, v, vt)
        m_s[...] = m
        l_s[...] = l
        acc_s[...] = acc

    @pl.when(j < nsteps - 1)
    def _():
        ppg = min(pps, 8)
        n = ppg * PAGE
        s_parts, v_parts = [], []
        for g in range(pps // ppg):
            xt = jnp.concatenate([pages[g * ppg + p][...] for p in range(ppg)], axis=1)
            kpet = xt[256:288]
            kv = jnp.dot(wkvt_ref[...], xt[0:256].astype(MXU_DTYPE), preferred_element_type=F32)
            pe2 = jnp.sum(kpet * kpet, axis=0, keepdims=True)
            parts = []
            for h in range(MLA_HEADS):
                kn = kv[h * MLA_NOPE:(h + 1) * MLA_NOPE]
                rs = lax.rsqrt((jnp.sum(kn * kn, axis=0, keepdims=True) + pe2) / float(MLA_QK) + EPS)
                parts += [(kn * rs) * gkc_ref[0:MLA_NOPE, :], (kpet * rs) * gkc_ref[MLA_NOPE:MLA_QK, :],
                          jnp.zeros((LANE - MLA_QK, n), F32)]
            kmt = jnp.concatenate(parts, axis=0).astype(MXU_DTYPE)
            s_parts.append(jnp.dot(qbd, kmt, preferred_element_type=F32) * scale)
            v_parts.append(kv[512:1024].astype(MXU_DTYPE))
        s = jnp.concatenate(s_parts, axis=1)
        m_new = jnp.maximum(m_s[...], jnp.max(s, axis=-1, keepdims=True))
        alpha = jnp.exp(m_s[...] - m_new)
        p = jnp.exp(s - m_new)
        acc = alpha * acc_s[...]
        for g, vmt in enumerate(v_parts):
            acc = acc + _mm_nt(p[:, g * n:(g + 1) * n], vmt)
        m_s[...] = m_new
        l_s[...] = alpha * l_s[...] + jnp.sum(p, axis=-1, keepdims=True)
        acc_s[...] = acc

    @pl.when(j == nsteps - 1)
    def _():
        npad = 2 * SAMPLE_T
        ckv = jnp.concatenate([newm_ref[0][:, 0:256], jnp.zeros((npad - SAMPLE_T, 256), F32)], axis=0)
        kpe = jnp.concatenate([newkp_ref[0], jnp.zeros((npad - SAMPLE_T, 128), F32)], axis=0)
        km, vm = _mla_keys(ckv, kpe, wk_ref[...], wv_ref[...], pk_ref[...], g8_ref[...], gk_ref[...])
        update(_mm_nt(qbd, km) * scale, (past + _iota((1, npad), 1)) <= q_pos, vm, False)
        o = _finish(l_s[...], acc_s[...])
        lane_head = _iota((SUB, 512), 1) // MLA_V
        sub8 = _iota((SUB, 512), 0)
        out = jnp.zeros((SAMPLE_T, 512), F32)
        for t in range(SAMPLE_T):
            blk = jnp.where(lane_head == sub8, o[t * SUB:(t + 1) * SUB, :], 0.0)
            r = jnp.sum(blk, axis=0, keepdims=True)
            out = jnp.where(_iota((SAMPLE_T, 512), 0) == t, jnp.broadcast_to(r, (SAMPLE_T, 512)), out)
        o_ref[0] = out


def _mla_sample(page_table, cache_t, layer, qm, newm, newkp, wkvt, gkc, wk, wv, pk, g8, gk, pps):
    db, n_pages = page_table.shape
    past = n_pages * PAGE
    nsteps = n_pages // pps + 1
    last = n_pages - 1

    def page_spec(p):
        return pl.BlockSpec((None, None, 288, PAGE),
                            lambda b, j, pt: (layer, pt[b, jnp.minimum(j * pps + p, last)], 0, 0))

    full = lambda a: pl.BlockSpec(a.shape, lambda b, j, pt: (0, 0))
    per_b = lambda w: pl.BlockSpec((1, SAMPLE_T, w), lambda b, j, pt: (b, 0, 0))
    rows = SAMPLE_T * MLA_HEADS
    kern = functools.partial(_mla_sample_kernel, pps=pps, past=past)
    return pl.pallas_call(
        kern,
        grid_spec=pltpu.PrefetchScalarGridSpec(
            num_scalar_prefetch=1,
            grid=(db, nsteps),
            in_specs=[page_spec(p) for p in range(pps)] + [per_b(1024), per_b(288), per_b(128), full(wkvt), full(gkc),
                                                          full(wk), full(wv), full(pk), full(g8), full(gk)],
            out_specs=per_b(512),
            scratch_shapes=[pltpu.VMEM((rows, 1), F32), pltpu.VMEM((rows, 1), F32), pltpu.VMEM((rows, 512), F32)],
        ),
        out_shape=jax.ShapeDtypeStruct((db, SAMPLE_T, 512), F32),
        compiler_params=_cparams(("parallel", "arbitrary"), 40),
        name="mla_sample",
    )(page_table, *([cache_t] * pps), qm, newm, newkp, wkvt, gkc, wk, wv, pk, g8, gk)


def _cmp_blocks(rows, wpos):
    n = rows.shape[0] // NSA_BLK
    x = rows.reshape(n, NSA_BLK, 256) * wpos[None]
    return jnp.sum(x, axis=1) / float(NSA_BLK)


def _cmp_rope(craw, ctab, swc):
    return craw * ctab[:, 0:256] + _place(craw, swc) * ctab[:, 256:512]


def _nsa_cmp_kernel(rn_ref, wpos_ref, ctab_ref, swc_ref, o_ref):
    craw = _cmp_blocks(rn_ref[...], wpos_ref[...])
    o_ref[...] = _cmp_rope(craw, ctab_ref[...], swc_ref[...]).astype(o_ref.dtype)


def _nsa_cmp(rn, wpos, ctab, swc, nb, t):
    nblk = t // NSA_BLK
    tb = min(nblk, 16)
    nt = nblk // tb
    return pl.pallas_call(
        _nsa_cmp_kernel,
        grid=(nb, nt),
        in_specs=[pl.BlockSpec((tb * NSA_BLK, 256), lambda b, i: (b * nt + i, 0)),
                  pl.BlockSpec(wpos.shape, lambda b, i: (0, 0)),
                  pl.BlockSpec((tb, 512), lambda b, i: (i, 0)),
                  pl.BlockSpec(swc.shape, lambda b, i: (0, 0))],
        out_specs=pl.BlockSpec((tb, 256), lambda b, i: (b * nt + i, 0)),
        out_shape=jax.ShapeDtypeStruct((nb * nblk, 256), MXU_DTYPE),
        compiler_params=_cparams(("parallel", "parallel"), 32),
        name="nsa_cmp",
    )(rn, wpos, ctab, swc)


def _top_n_mask(x, n_sel):
    t, w = x.shape
    lane = _iota((t, w), 1).astype(F32)

    def body(_, carry):
        x, sel = carry
        first = jnp.argmax(x, axis=-1, keepdims=True).astype(F32)
        hit = lane == first
        return jnp.where(hit, -jnp.inf, x), jnp.where(hit, 1.0, sel)

    _, sel = lax.fori_loop(0, n_sel, body, (x, jnp.zeros((t, w), F32)))
    return sel


def _nsa_attend(qa, misc, q_pos, cmp, nblk, key_ref, n_chunks, tkc, win, w_pos, pqc, pqs, pqw, expand_ref):
    tq = qa.shape[0]
    h = NSA_HEADS
    scale = D_HEAD ** -0.5
    nbp = cmp.shape[0]
    qp4 = jnp.concatenate([q_pos] * h, axis=0)
    qc_st = _stack_heads(_mm(qa, pqc), h, 256)
    blk = _iota((1, nbp), 1)
    s = _mm_nt(qc_st, cmp) * scale
    p_c = _softmax_rows(s, ((blk * NSA_BLK + (NSA_BLK - 1)) <= qp4) & (blk < nblk))
    o_c = _mm(p_c, cmp)
    imp = p_c[0:tq]
    for hh in range(1, h):
        imp = imp + p_c[hh * tq:(hh + 1) * tq]
    cur = q_pos // NSA_BLK
    imp = jnp.where(blk == cur, NSA_FORCE, jnp.where(blk < cur, imp, -1.0))
    imp = jnp.where(blk < nblk, imp, -jnp.inf)
    sel = _top_n_mask(imp, min(NSA_TOPN, nblk))
    qs_st = _stack_heads(_mm(qa, pqs), h, 256) * scale
    def scores(c):
        k0 = c * tkc if isinstance(c, int) else pl.multiple_of(c * tkc, tkc)
        keys = key_ref[pl.ds(k0, tkc), :]
        tok = _mm(sel, expand_ref[c])
        bias1 = jnp.where((tok > 0.5) & ((k0 + _iota((1, tkc), 1)) <= q_pos), 0.0, NEG)
        return _mm_nt(qs_st, keys), bias1, keys

    if isinstance(n_chunks, int):
        def chunk(c, carry):
            s, bias1, keys = scores(c)
            return _online_update(carry, s + jnp.concatenate([bias1] * h, axis=0), None, keys)

        _, l, acc = _loop(n_chunks, chunk, _flash_init(h * tq, 256))
    else:
        def chunk(c, carry):
            s, bias1, keys = scores(c)
            return tuple(_online_update(carry[hh], s[hh * tq:(hh + 1) * tq] + bias1, None, keys) for hh in range(h))

        heads = lax.fori_loop(0, n_chunks, chunk, tuple(_flash_init(tq, 256) for _ in range(h)))
        l = jnp.concatenate([hd[1] for hd in heads], axis=0)
        acc = jnp.concatenate([hd[2] for hd in heads], axis=0)
    o_s = _finish(l, acc)
    qw_st = _stack_heads(_mm(qa, pqw), h, 128)
    rel = qp4 - w_pos
    s = _mm_nt(qw_st, win) * scale
    p_w = _softmax_rows(s, (rel >= 0) & (rel < NSA_WINDOW) & (w_pos >= 0))
    o_w = _mm(p_w, win)
    gate = lambda jj: jnp.concatenate([misc[:, 3 * hh + jj:3 * hh + jj + 1] for hh in range(h)], axis=0)
    tsum = gate(0) * o_c[:, 0:128] + gate(1) * o_s[:, 128:256] + gate(2) * o_w
    tsum = jnp.where(_iota((1, 128), 1) >= D_HEAD, tsum, 0.0)
    return jnp.concatenate([tsum[hh * tq:(hh + 1) * tq] for hh in range(h)], axis=1)


def _nsa_prompt_kernel(qa_ref, misc_ref, cmp_ref, nk_ref, wk_ref, pqc_ref, pqs_ref, pqw_ref, exp_ref, o_ref,
                       *, tq, tkc, t):
    i = pl.program_id(1)
    s0 = i * tq
    q_pos = s0 + _iota((tq, 1), 0)
    w = NSA_WINDOW + tq
    kstart = pl.multiple_of(jnp.clip(s0 - NSA_WINDOW, 0, t - w), SUB)
    win = wk_ref[pl.ds(kstart, w), :]
    w_pos = kstart + _iota((1, w), 1)
    n_chunks = (s0 + tq + tkc - 1) // tkc
    o_ref[...] = _nsa_attend(qa_ref[...], misc_ref[...], q_pos, cmp_ref[...], t // NSA_BLK, nk_ref, n_chunks,
                             tkc, win, w_pos, pqc_ref[...], pqs_ref[...], pqw_ref[...], exp_ref)


def _nsa_prompt(qa, misc, cmp, nk, wk, pqc, pqs, pqw, expand, nb, t, tq, tkc):
    nq = t // tq
    nblk = t // NSA_BLK
    kern = functools.partial(_nsa_prompt_kernel, tq=tq, tkc=tkc, t=t)
    full = lambda a: pl.BlockSpec(a.shape, lambda b, i: (0, 0))
    return pl.pallas_call(
        kern,
        grid=(nb, nq),
        in_specs=[pl.BlockSpec((tq, 256), lambda b, i: (b * nq + i, 0)),
                  pl.BlockSpec((tq, 128), lambda b, i: (b * nq + i, 0)),
                  pl.BlockSpec((nblk, 256), lambda b, i: (b, 0)),
                  pl.BlockSpec((t, 256), lambda b, i: (b, 0)),
                  pl.BlockSpec((t, 128), lambda b, i: (b, 0)),
                  full(pqc), full(pqs), full(pqw), pl.BlockSpec(expand.shape, lambda b, i: (0, 0, 0))],
        out_specs=pl.BlockSpec((tq, 512), lambda b, i: (b * nq + i, 0)),
        out_shape=jax.ShapeDtypeStruct((nb * t, 512), F32),
        compiler_params=_cparams(("parallel", "arbitrary"), 48),
        name="nsa_prompt",
    )(qa, misc, cmp, nk, wk, pqc, pqs, pqw, expand)


def _nsa_sample_kernel(pt_ref, *refs, pps, past):
    pages = refs[:pps]
    (qa_ref, misc_ref, newn_ref, sw_ref, neww_ref, wpos_ref, ctab_ref, swc_ref, pqc_ref, pqs_ref, pqw_ref, exp_ref,
     o_ref, key_s, cmp_s, win_s) = refs[pps:]
    j = pl.program_id(1)
    nsteps = pl.num_programs(1)
    nblk = past // NSA_BLK + 1
    bps = pps * PAGE // NSA_BLK
    lpad = key_s.shape[0]

    @pl.when(j == 0)
    def _():
        key_s[past:lpad, :] = jnp.zeros((lpad - past, 256), key_s.dtype)
        cmp_s[...] = jnp.zeros(cmp_s.shape, F32)

    @pl.when(j < nsteps - 1)
    def _():
        rows = jnp.concatenate([pages[p][...] for p in range(pps)], axis=0)
        r0 = pl.multiple_of(j * pps * PAGE, pps * PAGE)
        key_s[pl.ds(r0, pps * PAGE), :] = rows.astype(key_s.dtype)
        b0 = pl.multiple_of(j * bps, bps)
        cmp_s[pl.ds(b0, bps), :] = _cmp_blocks(rows, wpos_ref[...])

    @pl.when(j == nsteps - 1)
    def _():
        newn = newn_ref[0]
        tail = jnp.concatenate([newn, jnp.zeros((NSA_BLK - SAMPLE_T, 256), F32)], axis=0)
        key_s[past:past + NSA_BLK, :] = tail.astype(key_s.dtype)
        cmp_s[past // NSA_BLK: past // NSA_BLK + 1, :] = _cmp_blocks(tail, wpos_ref[...])
        cmp = _cmp_rope(cmp_s[...], ctab_ref[...], swc_ref[...]).astype(MXU_DTYPE)
        win_s[0:NSA_WINDOW, :] = sw_ref[0].astype(win_s.dtype)
        win_s[NSA_WINDOW:NSA_WINDOW + 2 * SAMPLE_T, :] = jnp.concatenate(
            [neww_ref[0], jnp.zeros((SAMPLE_T, 128), F32)], axis=0).astype(win_s.dtype)
        w = NSA_WINDOW + 2 * SAMPLE_T
        w_pos = past - NSA_WINDOW + _iota((1, w), 1)
        q_pos = past + _iota((SAMPLE_T, 1), 0)
        o_ref[0] = _nsa_attend(qa_ref[0], misc_ref[0], q_pos, cmp, nblk, key_s, 1, lpad, win_s[...], w_pos,
                               pqc_ref[...], pqs_ref[...], pqw_ref[...], exp_ref)


def _nsa_sample(page_table, cache, layer, qa, misc, newn, state_win, neww, wpos, ctab, swc, pqc, pqs, pqw,
                expand, pps):
    db, n_pages = page_table.shape
    past = n_pages * PAGE
    nsteps = n_pages // pps + 1
    last = n_pages - 1
    nbp = ctab.shape[0]
    lpad = expand.shape[2]

    def page_spec(p):
        return pl.BlockSpec((None, None, PAGE, 256),
                            lambda b, j, pt: (layer, pt[b, jnp.minimum(j * pps + p, last)], 0, 0))

    full = lambda a: pl.BlockSpec(a.shape, lambda b, j, pt: (0, 0))
    per_b = lambda w: pl.BlockSpec((1, SAMPLE_T, w), lambda b, j, pt: (b, 0, 0))
    kern = functools.partial(_nsa_sample_kernel, pps=pps, past=past)
    return pl.pallas_call(
        kern,
        grid_spec=pltpu.PrefetchScalarGridSpec(
            num_scalar_prefetch=1,
            grid=(db, nsteps),
            in_specs=[page_spec(p) for p in range(pps)] + [
                per_b(256), per_b(128), per_b(256),
                pl.BlockSpec((None, 1, NSA_WINDOW, 128), lambda b, j, pt: (layer, b, 0, 0)),
                per_b(128), full(wpos), full(ctab), full(swc), full(pqc), full(pqs), full(pqw),
                pl.BlockSpec(expand.shape, lambda b, j, pt: (0, 0, 0))],
            out_specs=per_b(512),
            scratch_shapes=[pltpu.VMEM((lpad, 256), MXU_DTYPE), pltpu.VMEM((nbp, 256), F32),
                            pltpu.VMEM((NSA_WINDOW + 2 * SAMPLE_T, 128), MXU_DTYPE)],
        ),
        out_shape=jax.ShapeDtypeStruct((db, SAMPLE_T, 512), F32),
        compiler_params=_cparams(("parallel", "arbitrary"), 48),
        name="nsa_sample",
    )(page_table, *([cache] * pps), qa, misc, newn, state_win, neww, wpos, ctab, swc, pqc, pqs, pqw, expand)


def _loop(n, body, init):
    if isinstance(n, int):
        for c in range(n):
            init = body(c, init)
        return init
    return lax.fori_loop(0, n, body, init)


def _dsa_attend(qc, qi, misc, q_pos, key_ref, n_chunks, tkc, n_top, pqd, pqi, tri, key_s, kt=False):
    tq = qc.shape[0]
    h = DSA_HEADS
    qi_st = _stack_heads(_mm(qi, pqi), IDX_HEADS, 256)
    wcol = [misc[:, 12 + g:13 + g] for g in range(IDX_HEADS)]
    qk = _mm if kt else _mm_nt

    def get_keys(c):
        if kt:
            return key_ref[c]
        return key_ref[pl.ds(c * tkc if isinstance(c, int) else pl.multiple_of(c * tkc, tkc), tkc), :]

    def score_chunk(c, _):
        k0 = c * tkc
        rel = jnp.maximum(qk(qi_st, get_keys(c)), 0.0)
        sc = wcol[0] * rel[0:tq]
        for g in range(1, IDX_HEADS):
            sc = sc + wcol[g] * rel[g * tq:(g + 1) * tq]
        sc = jnp.where(sc == 0.0, 0.0, sc)
        sc = jnp.where((k0 + _iota((1, tkc), 1)) <= q_pos, sc, NEG)
        key_s[c] = _sortable(sc)
        return 0

    _loop(n_chunks, score_chunk, 0)

    def count(pred_fn):
        def body(c, acc):
            hit = jnp.where(pred_fn(key_s[c]), 1.0, 0.0)
            parts = [hit[:, u * LANE:(u + 1) * LANE] for u in range(tkc // LANE)]
            if isinstance(n_chunks, int):
                while len(parts) > 1:
                    parts = [parts[u] + parts[u + 1] for u in range(0, len(parts), 2)]
                return acc + parts[0]
            for part in parts:
                acc = acc + part
            return acc
        acc = _loop(n_chunks, body, jnp.zeros((tq, LANE), F32))
        return jnp.sum(acc, axis=-1, keepdims=True)

    static = isinstance(n_chunks, int)
    if static:
        def digit_step(it, thr):
            d = jnp.left_shift(jnp.int32(1), jnp.int32(30) - 2 * it)
            for mult in (1, 2, 3):
                cnt = count(lambda k: k >= thr + mult * d)
                step = jnp.where(cnt >= float(n_top), d, 0)
                new = thr + step if mult == 1 else new + step
            return new

        thr = lax.fori_loop(0, 16, digit_step, jnp.full((tq, 1), -2 ** 31, I32))
    else:
        def bit_step(it, thr):
            cand = thr + jnp.left_shift(jnp.int32(1), jnp.int32(31) - it)
            cnt = count(lambda k: k >= cand)
            return jnp.where(cnt >= float(n_top), cand, thr)

        thr = lax.fori_loop(0, 32, bit_step, jnp.full((tq, 1), -2 ** 31, I32))
    need = float(n_top) - count(lambda k: k > thr)

    qd_st = _stack_heads(_mm(qc, pqd), h, 256) * (D_HEAD ** -0.5)

    def scores(c, eq_seen):
        kk = key_s[c]
        eq = jnp.where(kk == thr, 1.0, 0.0)
        rank = eq_seen + _mm(eq, tri)
        sel = (kk > thr) | ((kk == thr) & (rank <= need))
        bias1 = jnp.where(sel & ((c * tkc + _iota((1, tkc), 1)) <= q_pos), 0.0, NEG)
        return eq_seen + jnp.sum(eq, axis=-1, keepdims=True), qk(qd_st, get_keys(c)), bias1

    if static:
        eq_seen, parts = jnp.zeros((tq, 1), F32), []
        for c in range(n_chunks):
            eq_seen, s, bias1 = scores(c, eq_seen)
            parts.append(s + jnp.concatenate([bias1] * h, axis=0))
        s = jnp.concatenate(parts, axis=1)
        m = jnp.maximum(jnp.max(s, axis=-1, keepdims=True), MASK_FLOOR)
        p = jnp.exp(s - m)
        l = jnp.sum(p, axis=-1, keepdims=True)
        pv = _mm_nt if kt else _mm
        acc = pv(p[:, 0:tkc], get_keys(0))
        for c in range(1, n_chunks):
            acc = acc + pv(p[:, c * tkc:(c + 1) * tkc], get_keys(c))
    else:
        def chunk(c, carry):
            eq_seen, s, bias1 = scores(c, carry[0])
            keys = get_keys(c)
            heads = tuple(_online_update(carry[1][hh], s[hh * tq:(hh + 1) * tq] + bias1, None, keys, kt)
                          for hh in range(h))
            return eq_seen, heads

        init = (jnp.zeros((tq, 1), F32), tuple(_flash_init(tq, 256) for _ in range(h)))
        _, heads = lax.fori_loop(0, n_chunks, chunk, init)
        l = jnp.concatenate([hd[1] for hd in heads], axis=0)
        acc = jnp.concatenate([hd[2] for hd in heads], axis=0)
    o = _finish(l, acc)[:, 0:128]
    o = jnp.where(_iota((1, 128), 1) >= D_HEAD, o, 0.0)
    return jnp.concatenate([o[hh * tq:(hh + 1) * tq] for hh in range(h)], axis=1)


def _dsa_prompt_kernel(qc_ref, qi_ref, misc_ref, dk_ref, pqd_ref, pqi_ref, tri_ref, o_ref, key_s,
                       *, tq, tkc, n_top):
    i = pl.program_id(1)
    s0 = i * tq
    q_pos = s0 + _iota((tq, 1), 0)
    n_chunks = (s0 + tq + tkc - 1) // tkc
    o_ref[...] = _dsa_attend(qc_ref[...], qi_ref[...], misc_ref[...], q_pos, dk_ref, n_chunks, tkc, n_top,
                             pqd_ref[...], pqi_ref[...], tri_ref[...], key_s)


def _dsa_prompt(qc, qi, misc, dk, pqd, pqi, tri, nb, t, tq, tkc):
    nq = t // tq
    n_top = min(DSA_TOPK_MAX, t // 4)
    kern = functools.partial(_dsa_prompt_kernel, tq=tq, tkc=tkc, n_top=n_top)
    full = lambda a: pl.BlockSpec(a.shape, lambda b, i: (0, 0))
    return pl.pallas_call(
        kern,
        grid=(nb, nq),
        in_specs=[pl.BlockSpec((tq, 256), lambda b, i: (b * nq + i, 0)),
                  pl.BlockSpec((tq, 256), lambda b, i: (b * nq + i, 0)),
                  pl.BlockSpec((tq, 128), lambda b, i: (b * nq + i, 0)),
                  pl.BlockSpec((t, 256), lambda b, i: (b, 0)),
                  full(pqd), full(pqi), full(tri)],
        out_specs=pl.BlockSpec((tq, 512), lambda b, i: (b * nq + i, 0)),
        out_shape=jax.ShapeDtypeStruct((nb * t, 512), F32),
        scratch_shapes=[pltpu.VMEM((t // tkc, tq, tkc), I32)],
        compiler_params=_cparams(("parallel", "arbitrary"), 48),
        name="dsa_prompt",
    )(qc, qi, misc, dk, pqd, pqi, tri)


def _dsa_sample_kernel(pt_ref, *refs, pps, past, tkc, n_top):
    pages = refs[:pps]
    (qc_ref, qi_ref, misc_ref, newd_ref, pqd_ref, pqi_ref, tri_ref, o_ref, key_t, key_s) = refs[pps:]
    j = pl.program_id(1)
    nsteps = pl.num_programs(1)
    ppc = tkc // PAGE
    n_chunks = past // tkc + 1

    @pl.when(j == 0)
    def _():
        key_t[:, 160:256, :] = jnp.zeros((n_chunks, 96, tkc), key_t.dtype)
        key_t[n_chunks - 1] = jnp.zeros((256, tkc), key_t.dtype)

    @pl.when(j < nsteps - 1)
    def _():
        for p in range(pps):
            c = j * (pps // ppc) + p // ppc
            key_t[c, 0:160, (p % ppc) * PAGE:(p % ppc + 1) * PAGE] = pages[p][...].astype(key_t.dtype)

    @pl.when(j == nsteps - 1)
    def _():
        new_rows = jnp.concatenate([newd_ref[0].astype(F32), jnp.zeros((PAGE - SAMPLE_T, 256), F32)], axis=0)
        key_t[n_chunks - 1, :, 0:PAGE] = new_rows.T.astype(key_t.dtype)
        q_pos = past + _iota((SAMPLE_T, 1), 0)
        o_ref[0] = _dsa_attend(qc_ref[0], qi_ref[0], misc_ref[0], q_pos, key_t, n_chunks, tkc, n_top,
                               pqd_ref[...], pqi_ref[...], tri_ref[...], key_s, kt=True)


def _dsa_sample(page_table, cache_t, layer, qc, qi, misc, newd, pqd, pqi, tri, pps, tkc, ts):
    db, n_pages = page_table.shape
    past = n_pages * PAGE
    assert past % tkc == 0 and pps % (tkc // PAGE) == 0
    nsteps = n_pages // pps + 1
    last = n_pages - 1
    n_chunks = past // tkc + 1
    n_top = min(DSA_TOPK_MAX, (past + ts) // 4)

    def page_spec(p):
        return pl.BlockSpec((None, None, 160, PAGE),
                            lambda b, j, pt: (layer, pt[b, jnp.minimum(j * pps + p, last)], 0, 0))

    full = lambda a: pl.BlockSpec(a.shape, lambda b, j, pt: (0, 0))
    per_b = lambda w: pl.BlockSpec((1, SAMPLE_T, w), lambda b, j, pt: (b, 0, 0))
    kern = functools.partial(_dsa_sample_kernel, pps=pps, past=past, tkc=tkc, n_top=n_top)
    return pl.pallas_call(
        kern,
        grid_spec=pltpu.PrefetchScalarGridSpec(
            num_scalar_prefetch=1,
            grid=(db, nsteps),
            in_specs=[page_spec(p) for p in range(pps)] + [
                per_b(256), per_b(256), per_b(128), per_b(256), full(pqd), full(pqi), full(tri)],
            out_specs=per_b(512),
            scratch_shapes=[pltpu.VMEM((n_chunks, 256, tkc), MXU_DTYPE), pltpu.VMEM((n_chunks, SAMPLE_T, tkc), I32)],
        ),
        out_shape=jax.ShapeDtypeStruct((db, SAMPLE_T, 512), F32),
        compiler_params=_cparams(("parallel", "arbitrary"), 48),
        name="dsa_sample",
    )(page_table, *([cache_t] * pps), qc, qi, misc, newd, pqd, pqi, tri)


def _out_kernel(h_ref, oa_ref, om_ref, od_ref, wa_ref, wm_ref, wd_ref, g_ref, h1_o, xn_o):
    h1 = h_ref[...] + (_mm(oa_ref[...], wa_ref[...]) + _mm(om_ref[...], wm_ref[...]) + _mm(od_ref[...], wd_ref[...]))
    h1_o[...] = h1
    xn = h1 * lax.rsqrt(jnp.mean(h1 * h1, axis=-1, keepdims=True) + EPS) * g_ref[...]
    xn_o[...] = xn.astype(xn_o.dtype)


def _out_proj(h, oa, om, od, wa, wm, wd, g, tm):
    n = h.shape[0]
    row = lambda w: pl.BlockSpec((tm, w), lambda i: (i, 0))
    full = lambda a: pl.BlockSpec(a.shape, lambda i: (0, 0))
    return pl.pallas_call(
        _out_kernel,
        grid=(n // tm,),
        in_specs=[row(D_MODEL), row(512), row(512), row(512), full(wa), full(wm), full(wd), full(g)],
        out_specs=[row(D_MODEL), row(D_MODEL)],
        out_shape=[jax.ShapeDtypeStruct((n, D_MODEL), F32), jax.ShapeDtypeStruct((n, D_MODEL), MXU_DTYPE)],
        compiler_params=_cparams(("parallel",), 40),
        name="out_proj",
    )(h, oa, om, od, wa, wm, wd, g)


def _ffn_kernel(*refs, tm, seq, sample):
    if sample:
        (xn_ref, h1_ref, p_ref, wu_ref, wg_ref, wd_ref, cw_ref, cb_ref, gp_ref, wpg_ref, wpp_ref,
         p1_ref, p2_ref, h_o, a_o, acc_s) = refs
    else:
        (xn_ref, halo_ref, h1_ref, p_ref, wu_ref, wg_ref, wd_ref, cw_ref, cb_ref, gp_ref, wpg_ref, wpp_ref,
         h_o, a_o, acc_s) = refs
    i = pl.program_id(0)
    j = pl.program_id(1)
    xn = xn_ref[...]
    a = _mm(xn, wu_ref[...])
    b = _mm(xn, wg_ref[...])
    a_o[...] = a
    row = _iota((tm, 1), 0)
    r1 = pltpu.roll(a, 1, 0)
    r2 = pltpu.roll(a, 2, 0)
    if sample:
        tpos = row % SAMPLE_T
        prev1 = jnp.where(tpos == 0, p1_ref[...], r1)
        prev2 = jnp.where(tpos < 2, p2_ref[...], r2)
    else:
        ah = _mm(halo_ref[...], wu_ref[...])
        ah = jnp.where((i * tm) % seq == 0, 0.0, ah)
        prev1 = jnp.where(row == 0, ah[7:8, :], r1)
        prev2 = jnp.where(row == 0, ah[6:7, :], jnp.where(row == 1, ah[7:8, :], r2))
    c = cb_ref[...] + cw_ref[2:3, :] * a
    c = c + cw_ref[0:1, :] * prev2
    c = c + cw_ref[1:2, :] * prev1
    u = (c * jax.nn.sigmoid(c)) * b
    y = _mm(u, wd_ref[...])

    @pl.when(j == 0)
    def _():
        acc_s[...] = y

    @pl.when(j > 0)
    def _():
        acc_s[...] = acc_s[...] + y

    @pl.when(j == pl.num_programs(1) - 1)
    def _():
        h2 = h1_ref[...] + acc_s[...]
        xn3 = h2 * lax.rsqrt(jnp.mean(h2 * h2, axis=-1, keepdims=True) + EPS) * gp_ref[...]
        gate = jax.nn.sigmoid(_mm(xn3, wpg_ref[...]))
        h_o[...] = h2 + gate * _mm(p_ref[...], wpp_ref[...])


def _ffn(xn, h1, p, wu, wg, wd, cw, cb, gp, wpg, wpp, tm, tf, seq, prefix=None):
    n = xn.shape[0]
    nf = D_FF // tf
    sample = prefix is not None
    row = lambda w: pl.BlockSpec((tm, w), lambda i, j: (i, 0))
    full = lambda a: pl.BlockSpec(a.shape, lambda i, j: (0, 0))
    ff_col = lambda r: pl.BlockSpec((r, tf), lambda i, j: (0, j))
    in_specs = [row(D_MODEL)]
    args = [xn]
    if not sample:
        in_specs.append(pl.BlockSpec((SUB, D_MODEL), lambda i, j: (jnp.maximum(i * (tm // SUB) - 1, 0), 0)))
        args.append(xn)
    in_specs += [row(D_MODEL), row(PLE_DIM), ff_col(D_MODEL), ff_col(D_MODEL),
                 pl.BlockSpec((tf, D_MODEL), lambda i, j: (j, 0)), ff_col(CONV_W), ff_col(1),
                 full(gp), full(wpg), full(wpp)]
    args += [h1, p, wu, wg, wd, cw, cb, gp, wpg, wpp]
    if sample:
        in_specs += [pl.BlockSpec((tm, tf), lambda i, j: (i, j))] * 2
        args += list(prefix)
    kern = functools.partial(_ffn_kernel, tm=tm, seq=seq, sample=sample)
    return pl.pallas_call(
        kern,
        grid=(n // tm, nf),
        in_specs=in_specs,
        out_specs=[row(D_MODEL), pl.BlockSpec((tm, tf), lambda i, j: (i, j))],
        out_shape=[jax.ShapeDtypeStruct((n, D_MODEL), F32), jax.ShapeDtypeStruct((n, D_FF), F32)],
        scratch_shapes=[pltpu.VMEM((tm, D_MODEL), F32)],
        compiler_params=_cparams(("parallel", "arbitrary"), 56),
        name="ffn",
    )(*args)


def _pick(n, prefs):
    for p in prefs:
        if n % p == 0:
            return p
    return n


def kernel(x_prompt, x_sample, cache_nsa, cache_mla, cache_dsa, state_win, state_conv, page_table, p_prompt, p_sample, norm_mix, w_in, nsa_qn, nsa_kn, nsa_cmp_pos, mla_cqn, mla_ckvn, mla_w_uq, mla_w_ukv, mla_qn, mla_kn, dsa_qn, dsa_kn, w_out, norm_ffn, ffn_w_up, ffn_w_gate, ffn_conv_w, ffn_conv_b, ffn_w_down, norm_ple, ple_w_gate, ple_w_proj):
    depth = w_in.shape[0]
    B, T, _ = x_prompt.shape
    DB, TS, _ = x_sample.shape
    n_pages = page_table.shape[1]
    past = n_pages * PAGE
    assert TS <= 4 and T % 128 == 0 and T >= NSA_WINDOW + 128 and past >= NSA_WINDOW
    mm = MXU_DTYPE
    bf = lambda a: jnp.asarray(a, jnp.bfloat16)

    win_cols, wuq_cols = _win_cols(), _wuq_cols()
    g256 = bf(_blockdiag(256, 64))
    g1024 = bf(_blockdiag(1024, 128))
    g8 = bf((np.arange(1024)[:, None] // 128 == np.arange(128)[None, :]).astype(np.float32))
    pk = bf(_placement(128, 1024, [(j, h * 128 + 64 + j) for h in range(8) for j in range(32)]))
    swc = bf(_placement(256, 256, [(j, (j + 32) % 64) for j in range(64)]))
    pq_c = bf(_q_placement(4, 64, 256, 0))
    pq_s = bf(_q_placement(4, 64, 256, 128))
    pq_w = bf(_q_placement(4, 64, 128, 0))
    pq_d = bf(_q_placement(4, 64, 256, 0))
    pq_i = bf(_q_placement(8, 32, 256, 128))
    tkc = _pick(T, (512, 256, 128))
    tri = bf(np.triu(np.ones((tkc, tkc), np.float32)))
    tab_p = _rope_tables(jnp.arange(T))
    pos_s = past + (jnp.arange(DB * SAMPLE_T) % SAMPLE_T)
    tab_s = _rope_tables(pos_s)
    nblk_p = T // NSA_BLK
    ctab_p = _cmp_tables(nblk_p)
    nbp_s = ((past // NSA_BLK + 1 + LANE - 1) // LANE) * LANE
    ctab_s = _cmp_tables(nbp_s)
    lpad_s = ((past + SAMPLE_T + tkc - 1) // tkc) * tkc
    blk_of = lambda n_keys: np.arange(n_keys) // NSA_BLK
    expand_p = bf((blk_of(T).reshape(T // tkc, 1, tkc) == np.arange(nblk_p)[None, :, None]).astype(np.float32))
    expand_s = bf((blk_of(lpad_s)[None, None, :] == np.arange(nbp_s)[None, :, None]).astype(np.float32))
    exp_rows = np.full((512,), -1, np.int64)
    for h in range(4):
        exp_rows[h * 128 + 64: (h + 1) * 128] = h * 64 + np.arange(64)

    tm_p = _pick(B * T, (256, 128))
    tq = 256
    tq_m = _pick(T, (512, 256, 128))
    tm_o = _pick(B * T, (512, 256, 128))
    tf = D_FF // 2
    ns = DB * SAMPLE_T

    cache_mla_t = jnp.swapaxes(cache_mla, 2, 3)
    cache_dsa_t = jnp.swapaxes(cache_dsa, 2, 3)
    hp = x_prompt.reshape(B * T, D_MODEL)
    hs = jnp.pad(x_sample, ((0, 0), (0, SAMPLE_T - TS), (0, 0))).reshape(ns, D_MODEL)
    outs = {k: [] for k in ("nsa_p", "nsa_s", "mla_p", "mla_s", "dsa_p", "dsa_s", "win_p", "win_s", "conv_p", "conv_s")}
    unpad = lambda a: a.reshape(DB, SAMPLE_T, -1)[:, :TS]

    for i in range(depth):
        wbig = _take_cols(w_in[i], win_cols).astype(mm)
        wuq = _take_cols(mla_w_uq[i], wuq_cols).astype(mm)
        gv = _gain_vector(norm_mix[i], nsa_qn[i], nsa_kn[i], mla_cqn[i], mla_ckvn[i], dsa_qn[i], dsa_kn[i], mla_qn[i])
        ukv = mla_w_ukv[i].reshape(MLA_KV_RANK, MLA_HEADS, MLA_NOPE + MLA_V)
        wk = jnp.pad(ukv[:, :, :MLA_NOPE], ((0, 0), (0, 0), (0, LANE - MLA_NOPE))).reshape(MLA_KV_RANK, 1024).astype(mm)
        wv = ukv[:, :, MLA_NOPE:].reshape(MLA_KV_RANK, 512).astype(mm)
        gk = jnp.tile(jnp.concatenate([mla_kn[i], jnp.zeros((32,), F32)]), 8)[None, :]
        wkvt = jnp.concatenate([ukv[:, :, :MLA_NOPE].reshape(MLA_KV_RANK, 512),
                                ukv[:, :, MLA_NOPE:].reshape(MLA_KV_RANK, 512)], axis=1).T.astype(mm)
        gkc = mla_kn[i][:, None]
        wpos = jnp.concatenate([nsa_cmp_pos[i, 0], nsa_cmp_pos[i, 1], jnp.zeros((NSA_BLK, 128), F32)], axis=1)
        wo = w_out[i]
        expand = lambda w: jnp.where(jnp.asarray(exp_rows >= 0)[:, None],
                                     jnp.take(w, jnp.asarray(np.maximum(exp_rows, 0)), axis=0), 0.0).astype(mm)
        wo_a, wo_m, wo_d = expand(wo[0:256]), wo[256:768].astype(mm), expand(wo[768:1024])
        ffn_w = (ffn_w_up[i].astype(mm), ffn_w_gate[i].astype(mm), ffn_w_down[i].astype(mm), ffn_conv_w[i],
                 ffn_conv_b[i][None, :], norm_ple[i][None, :], ple_w_gate[i].astype(mm), ple_w_proj[i].astype(mm))
        g_ffn = norm_ffn[i][None, :]

        (qa, qm, qc, qi, misc, rn, rm, rd, rw, nk, wkk, dk, kp) = _proj(
            hp, tab_p, T // tm_p, gv, wbig, wuq, g256, g1024, tm_p)
        tk_m = _pick(T, (512, 256, 128))
        km, vt = _mla_prep(rm, kp, wk, wv, wv.T, pk, g8, gk, tk_m)
        o_m = _mla_flash(qm, km, vt, B, T, tq_m, tk_m, 4)
        cmp = _nsa_cmp(rn, wpos, ctab_p, swc, B, T)
        o_a = _nsa_prompt(qa, misc, cmp, nk, wkk, pq_c, pq_s, pq_w, expand_p, B, T, tq, tkc)
        o_d = _dsa_prompt(qc, qi, misc, dk, pq_d, pq_i, tri, B, T, tq, tkc)
        h1, xn2 = _out_proj(hp, o_a, o_m, o_d, wo_a, wo_m, wo_d, g_ffn, tm_o)
        hp, a_p = _ffn(xn2, h1, p_prompt[i].reshape(B * T, PLE_DIM), *ffn_w, tm_o, tf, T)
        outs["nsa_p"].append(rn.reshape(B, T, -1))
        outs["mla_p"].append(rm.reshape(B, T, -1))
        outs["dsa_p"].append(rd.reshape(B, T, -1))
        outs["win_p"].append(rw.reshape(B, T, -1)[:, T - min(NSA_WINDOW, T):])
        outs["conv_p"].append(a_p.reshape(B, T, D_FF)[:, T - (CONV_W - 1):])

        (qa, qm, qc, qi, misc, rn, rm, rd, rw, nk, wkk, dk, kp) = _proj(
            hs, tab_s, 1, gv, wbig, wuq, g256, g1024, _pick(ns, (256, 128)))
        r3 = lambda a: a.reshape(DB, SAMPLE_T, -1)
        pps_m = _pick(n_pages, (32, 16, 8, 4))
        pps = _pick(n_pages, (8, 4))
        o_m = _mla_sample(page_table, cache_mla_t, i, r3(qm), r3(rm), r3(kp), wkvt, gkc, wk, wv, pk, g8, gk, pps_m)
        o_a = _nsa_sample(page_table, cache_nsa, i, r3(qa), r3(misc), r3(rn), state_win, r3(rw), wpos, ctab_s, swc,
                          pq_c, pq_s, pq_w, expand_s, pps)
        o_d = _dsa_sample(page_table, cache_dsa_t, i, r3(qc), r3(qi), r3(misc), r3(dk), pq_d, pq_i, tri, pps, tkc, TS)
        tm_s = _pick(ns, (256, 128))
        h1, xn2 = _out_proj(hs, o_a.reshape(ns, 512), o_m.reshape(ns, 512), o_d.reshape(ns, 512),
                            wo_a, wo_m, wo_d, g_ffn, tm_s)
        sc = state_conv[i]
        zrow = jnp.zeros((DB, SAMPLE_T - 1, D_FF), F32)
        p1 = jnp.concatenate([sc[:, 1:2], zrow], axis=1).reshape(ns, D_FF)
        p2 = jnp.concatenate([sc[:, 0:2], zrow[:, 1:]], axis=1).reshape(ns, D_FF)
        p_s = jnp.pad(p_sample[i], ((0, 0), (0, SAMPLE_T - TS), (0, 0))).reshape(ns, PLE_DIM)
        hs, a_s = _ffn(xn2, h1, p_s, *ffn_w, tm_s, tf, SAMPLE_T, prefix=(p1, p2))
        outs["nsa_s"].append(unpad(rn))
        outs["mla_s"].append(unpad(rm))
        outs["dsa_s"].append(unpad(rd))
        win_all = jnp.concatenate([state_win[i], unpad(rw)], axis=1)
        outs["win_s"].append(win_all[:, TS:])
        a_ext = jnp.concatenate([sc, unpad(a_s)], axis=1)
        outs["conv_s"].append(a_ext[:, TS:])

    st = lambda k: jnp.stack(outs[k])
    return (hp.reshape(B, T, D_MODEL), unpad(hs),
            st("nsa_p"), st("nsa_s"), st("mla_p"), st("mla_s"), st("dsa_p"), st("dsa_s"),
            st("win_p"), st("win_s"), st("conv_p"), st("conv_s"))
```
